```python
import math
import jax, jax.numpy as jnp
from jax import lax
import numpy as np

D_MODEL = 2048
BATCH = 4
SEQ = 4096
DEPTH = 4

CHUNK = 64
Q_BLOCK = 128
ROPE_THETA = 10000.0
EPS = 1e-6

DIFF_HEADS = 8
DIFF_QK_DIM = 64
DIFF_V_DIM = 128

MLA_HEADS = 8
MLA_Q_RANK = 512
MLA_KV_RANK = 256
MLA_NOPE_DIM = 128
MLA_ROPE_DIM = 64
MLA_V_DIM = 128

MEM_TOKENS = 256
MEM_HEADS = 4
MEM_HEAD_DIM = D_MODEL // MEM_HEADS

D_FF = 5632
CONV_WIDTH = 3

DIFF_WIDTH = DIFF_HEADS * DIFF_V_DIM
MLA_WIDTH = MLA_HEADS * MLA_V_DIM
IN_WIDTHS = (DIFF_HEADS * 2 * DIFF_QK_DIM,
             DIFF_HEADS * 2 * DIFF_QK_DIM,
             DIFF_WIDTH,
             MLA_Q_RANK,
             MLA_KV_RANK,
             MLA_ROPE_DIM,
             2 * D_MODEL)
N_IN = sum(IN_WIDTHS)

kernel_name = "hybrid_diffattn_mla_gated_convffn"


def _rms(x, g):
    xf = x.astype(jnp.float32)
    y = xf * lax.rsqrt(jnp.mean(xf * xf, axis=-1, keepdims=True) + EPS)
    return (y * g.astype(jnp.float32)).astype(x.dtype)


def _rope(x, pos):
    d = x.shape[-1]
    inv = ROPE_THETA ** (-jnp.arange(0, d, 2, dtype=jnp.float32) / d)
    ang = pos.astype(jnp.float32)[..., None] * inv
    cos = jnp.cos(ang)[:, :, None, :]
    sin = jnp.sin(ang)[:, :, None, :]
    xf = x.astype(jnp.float32)
    x1, x2 = xf[..., : d // 2], xf[..., d // 2:]
    return jnp.concatenate([x1 * cos - x2 * sin, x2 * cos + x1 * sin], axis=-1).astype(x.dtype)


def _chunk_mask(blk, seq):
    q_chunk = (blk * Q_BLOCK + jnp.arange(Q_BLOCK)) // CHUNK
    k_chunk = jnp.arange(seq) // CHUNK
    return k_chunk[None, :] <= q_chunk[:, None]


def _masked_softmax(s, mask):
    return jax.nn.softmax(jnp.where(mask, s.astype(jnp.float32), -jnp.inf), axis=-1)


def _sweep(fn, *qs):
    b, s = qs[0].shape[:2]
    nb = s // Q_BLOCK
    blocks = tuple(q.reshape(b, nb, Q_BLOCK, *q.shape[2:]).swapaxes(0, 1) for q in qs)
    out = lax.map(lambda a: fn(a[0], *a[1:]), (jnp.arange(nb), *blocks))
    return out.swapaxes(0, 1).reshape(b, s, *out.shape[3:])


def _diff_attention(dq, dk, dv, lam, lam_init, g_sub, pos):
    b, s, _ = dq.shape
    q = dq.reshape(b, s, DIFF_HEADS, 2 * DIFF_QK_DIM)
    k = dk.reshape(b, s, DIFF_HEADS, 2 * DIFF_QK_DIM)
    v = dv.reshape(b, s, DIFF_HEADS, DIFF_V_DIM)
    q1 = _rope(q[..., :DIFF_QK_DIM], pos)
    q2 = _rope(q[..., DIFF_QK_DIM:], pos)
    k1 = _rope(k[..., :DIFF_QK_DIM], pos)
    k2 = _rope(k[..., DIFF_QK_DIM:], pos)
    scale = 1.0 / math.sqrt(DIFF_QK_DIM)

    def block(blk, q1b, q2b):
        mask = _chunk_mask(blk, s)
        p1 = _masked_softmax(jnp.einsum('bqhd,bkhd->bhqk', q1b, k1) * scale, mask)
        p2 = _masked_softmax(jnp.einsum('bqhd,bkhd->bhqk', q2b, k2) * scale, mask)
        p = (p1 - lam * p2).astype(v.dtype)
        return jnp.einsum('bhqk,bkhd->bqhd', p, v)

    o = _sweep(block, q1, q2)
    o = _rms(o, g_sub) * (1.0 - lam_init)
    return o.reshape(b, s, DIFF_WIDTH)


def _mla(c_q, c_kv, k_rope, g_cq, w_uq, g_ckv, w_ukv, pos):
    b, s, _ = c_q.shape
    q = (_rms(c_q, g_cq) @ w_uq).reshape(b, s, MLA_HEADS, MLA_NOPE_DIM + MLA_ROPE_DIM)
    q_nope = q[..., :MLA_NOPE_DIM]
    q_pe = _rope(q[..., MLA_NOPE_DIM:], pos)
    kv = (_rms(c_kv, g_ckv) @ w_ukv).reshape(b, s, MLA_HEADS, MLA_NOPE_DIM + MLA_V_DIM)
    k_nope = kv[..., :MLA_NOPE_DIM]
    v = kv[..., MLA_NOPE_DIM:]
    k_pe = _rope(k_rope[:, :, None, :], pos)[:, :, 0, :]
    scale = 1.0 / math.sqrt(MLA_NOPE_DIM + MLA_ROPE_DIM)

    def block(blk, qnb, qpb):
        mask = _chunk_mask(blk, s)
        sc = jnp.einsum('bqhd,bkhd->bhqk', qnb, k_nope) + jnp.einsum('bqhd,bkd->bhqk', qpb, k_pe)
        p = _masked_softmax(sc * scale, mask).astype(v.dtype)
        return jnp.einsum('bhqk,bkhd->bqhd', p, v)

    return _sweep(block, q_nope, q_pe).reshape(b, s, MLA_WIDTH)


def _memory_attention(h, m, w_q, w_kv, w_o):
    b, s, _ = h.shape
    q = (h @ w_q).reshape(b, s, MEM_HEADS, MEM_HEAD_DIM)
    kv = (m @ w_kv).reshape(b, m.shape[1], 2, MEM_HEADS, MEM_HEAD_DIM)
    k, v = kv[:, :, 0], kv[:, :, 1]
    sc = jnp.einsum('bqhd,bkhd->bhqk', q, k) * (1.0 / math.sqrt(MEM_HEAD_DIM))
    p = jax.nn.softmax(sc.astype(jnp.float32), axis=-1).astype(v.dtype)
    o = jnp.einsum('bhqk,bkhd->bqhd', p, v).reshape(b, s, D_MODEL)
    return o @ w_o


def _conv_glu(h, w_up, conv_w, conv_b, w_down):
    s = h.shape[1]
    u = h @ w_up
    up = jnp.pad(u, ((0, 0), (CONV_WIDTH - 1, 0), (0, 0)))
    c = sum(up[:, j:j + s, :] * conv_w[j] for j in range(CONV_WIDTH)) + conv_b
    a, g = c[..., :D_FF], c[..., D_FF:]
    return (jax.nn.silu(a) * g) @ w_down


def setup_inputs(seed: int = 0) -> dict:
    key = jax.random.key(seed)
    ks = iter(jax.random.split(key, 40))
    L = DEPTH

    def w(shape, fan_in):
        return jax.random.normal(next(ks), shape, jnp.float32) * fan_in ** -0.5

    def gain(shape):
        return 1.0 + 0.05 * jax.random.normal(next(ks), shape, jnp.float32)

    x = jax.random.normal(next(ks), (BATCH, SEQ, D_MODEL), jnp.float32)
    mem = jax.random.normal(next(ks), (BATCH, MEM_TOKENS, D_MODEL), jnp.float32)
    offset = jax.random.randint(next(ks), (BATCH, 1), 0, 64, dtype=jnp.int32) * CHUNK
    positions = (offset + jnp.arange(SEQ, dtype=jnp.int32)[None, :]).astype(jnp.int32)
    return {
        "x": x,
        "mem": mem,
        "positions": positions,
        "g_pre_mix": gain((L, D_MODEL)),
        "w_in": w((L, D_MODEL, N_IN), D_MODEL),
        "b_gate": 0.1 * jax.random.normal(next(ks), (L, 2 * D_MODEL), jnp.float32),
        "lam_q1": 0.1 * jax.random.normal(next(ks), (L, DIFF_QK_DIM), jnp.float32),
        "lam_k1": 0.1 * jax.random.normal(next(ks), (L, DIFF_QK_DIM), jnp.float32),
        "lam_q2": 0.1 * jax.random.normal(next(ks), (L, DIFF_QK_DIM), jnp.float32),
        "lam_k2": 0.1 * jax.random.normal(next(ks), (L, DIFF_QK_DIM), jnp.float32),
        "g_diff_sub": gain((L, DIFF_V_DIM)),
        "g_cq": gain((L, MLA_Q_RANK)),
        "w_uq": w((L, MLA_Q_RANK, MLA_HEADS * (MLA_NOPE_DIM + MLA_ROPE_DIM)), MLA_Q_RANK),
        "g_ckv": gain((L, MLA_KV_RANK)),
        "w_ukv": w((L, MLA_KV_RANK, MLA_HEADS * (MLA_NOPE_DIM + MLA_V_DIM)), MLA_KV_RANK),
        "w_br_diff": w((L, DIFF_WIDTH, D_MODEL), DIFF_WIDTH),
        "w_br_mla": w((L, MLA_WIDTH, D_MODEL), MLA_WIDTH),
        "w_mix_out": w((L, D_MODEL, D_MODEL), D_MODEL),
        "g_post_mix": gain((L, D_MODEL)),
        "g_pre_x": gain((L, D_MODEL)),
        "g_mem": gain((L, D_MODEL)),
        "w_q_x": w((L, D_MODEL, D_MODEL), D_MODEL),
        "w_kv_x": w((L, D_MODEL, 2 * D_MODEL), D_MODEL),
        "w_o_x": w((L, D_MODEL, D_MODEL), D_MODEL),
        "g_post_x": gain((L, D_MODEL)),
        "g_pre_ffn": gain((L, D_MODEL)),
        "w_up": w((L, D_MODEL, 2 * D_FF), D_MODEL),
        "conv_w": w((L, CONV_WIDTH, 2 * D_FF), CONV_WIDTH),
        "conv_b": 0.02 * jax.random.normal(next(ks), (L, 2 * D_FF), jnp.float32),
        "w_down": w((L, D_FF, D_MODEL), D_FF),
        "g_post_ffn": gain((L, D_MODEL)),
    }


def reference(x, mem, positions, g_pre_mix, w_in, b_gate, lam_q1, lam_k1, lam_q2, lam_k2,
              g_diff_sub, g_cq, w_uq, g_ckv, w_ukv, w_br_diff, w_br_mla, w_mix_out,
              g_post_mix, g_pre_x, g_mem, w_q_x, w_kv_x, w_o_x, g_post_x, g_pre_ffn,
              w_up, conv_w, conv_b, w_down, g_post_ffn):
    split_at = [int(i) for i in np.cumsum(IN_WIDTHS)[:-1]]
    for l in range(DEPTH):
        h = _rms(x, g_pre_mix[l])
        proj = h @ w_in[l]
        dq, dk, dv, c_q, c_kv, k_rope, gt = jnp.split(proj, split_at, axis=-1)

        lam_init = 0.8 - 0.6 * math.exp(-0.3 * l)
        lam = (jnp.exp(jnp.sum(lam_q1[l].astype(jnp.float32) * lam_k1[l].astype(jnp.float32)))
               - jnp.exp(jnp.sum(lam_q2[l].astype(jnp.float32) * lam_k2[l].astype(jnp.float32)))
               + lam_init)
        o_diff = _diff_attention(dq, dk, dv, lam, lam_init, g_diff_sub[l], positions)
        o_mla = _mla(c_q, c_kv, k_rope, g_cq[l], w_uq[l], g_ckv[l], w_ukv[l], positions)

        gates = jax.nn.sigmoid((gt + b_gate[l]).astype(jnp.float32)).astype(x.dtype)
        g_a, g_b = gates[..., :D_MODEL], gates[..., D_MODEL:]
        merged = g_a * (o_diff @ w_br_diff[l]) + g_b * (o_mla @ w_br_mla[l])
        x = x + _rms(merged @ w_mix_out[l], g_post_mix[l])

        h = _rms(x, g_pre_x[l])
        m = _rms(mem, g_mem[l])
        x = x + _rms(_memory_attention(h, m, w_q_x[l], w_kv_x[l], w_o_x[l]), g_post_x[l])

        h = _rms(x, g_pre_ffn[l])
        x = x + _rms(_conv_glu(h, w_up[l], conv_w[l], conv_b[l], w_down[l]), g_post_ffn[l])
    return x
```

```python
import functools
import math

import jax
import jax.numpy as jnp
from jax import lax
from jax.experimental import pallas as pl
from jax.experimental.pallas import tpu as pltpu

CHUNK = 64
ROPE_THETA = 10000.0
EPS = 1e-6
DIFF_HEADS = 8
DIFF_QK_DIM = 64
DIFF_V_DIM = 128
MLA_HEADS = 8
MLA_Q_RANK = 512
MLA_KV_RANK = 256
MLA_NOPE_DIM = 128
MLA_ROPE_DIM = 64
MLA_V_DIM = 128
MEM_HEADS = 4
CONV_WIDTH = 3
DIFF_WIDTH = DIFF_HEADS * DIFF_V_DIM
MLA_WIDTH = MLA_HEADS * MLA_V_DIM

LANES = 128
BF16_SUBLANES = 16
VMEM_LIMIT_BYTES = 56 * 2**20

MXU_DTYPE = jnp.bfloat16
F32 = jnp.float32


def _pick(n, pref, mult=LANES):
    if n <= pref:
        return n
    best = None
    for d in range(mult, pref + 1, mult):
        if n % d == 0:
            best = d
    assert best is not None, (n, pref, mult)
    return best


def _cparams(*sem):
    return pltpu.CompilerParams(dimension_semantics=sem, vmem_limit_bytes=VMEM_LIMIT_BYTES)


def _rms(x, g):
    return x * lax.rsqrt(jnp.mean(x * x, axis=-1, keepdims=True) + EPS) * g


def _rope128(t, cos, sin):
    lane = lax.broadcasted_iota(jnp.int32, t.shape, 1)
    swapped = jnp.where((lane & 32) == 0, pltpu.roll(t, 96, 1), pltpu.roll(t, 32, 1))
    return t * cos + swapped * sin


def _tables_kernel(pos_ref, inv_ref, sgn_ref, cos_ref, sin_ref):
    ang = pos_ref[...].astype(F32) * inv_ref[...]
    cos_ref[...] = jnp.cos(ang)
    sin_ref[...] = jnp.sin(ang) * sgn_ref[...]


def _rope_tables(positions):
    m = positions.size
    d = DIFF_QK_DIM
    inv = ROPE_THETA ** (-jnp.arange(0, d, 2, dtype=F32) / d)
    inv128 = jnp.tile(inv, LANES // (d // 2))[None, :]
    half = jnp.concatenate([-jnp.ones((d // 2,), F32), jnp.ones((d // 2,), F32)])
    sgn128 = jnp.tile(half, LANES // d)[None, :]
    bm = _pick(m, 2048, 8)
    row = pl.BlockSpec((bm, LANES), lambda i: (i, 0))
    const = pl.BlockSpec((1, LANES), lambda i: (0, 0))
    return pl.pallas_call(
        _tables_kernel,
        grid=(m // bm,),
        in_specs=[pl.BlockSpec((bm, 1), lambda i: (i, 0)), const, const],
        out_specs=[row, row],
        out_shape=[jax.ShapeDtypeStruct((m, LANES), F32)] * 2,
        compiler_params=_cparams("parallel"),
        name="rope_tables",
    )(positions.reshape(m, 1), inv128, sgn128)


def _rmsnorm_kernel(x_ref, g_ref, o_ref):
    o_ref[...] = _rms(x_ref[...], g_ref[...]).astype(o_ref.dtype)


def _rmsnorm(x, g):
    m, d = x.shape
    bm = _pick(m, 512, 8)
    return pl.pallas_call(
        _rmsnorm_kernel,
        grid=(m // bm,),
        in_specs=[pl.BlockSpec((bm, d), lambda i: (i, 0)), pl.BlockSpec((1, d), lambda i: (0, 0))],
        out_specs=pl.BlockSpec((bm, d), lambda i: (i, 0)),
        out_shape=jax.ShapeDtypeStruct((m, d), MXU_DTYPE),
        compiler_params=_cparams("parallel"),
        name="rmsnorm",
    )(x, g[None, :])


def _proj_kernel(h_ref, w_ref, *rest, epilogue):
    *extra, o_ref = rest
    acc = jnp.dot(h_ref[...], w_ref[...], preferred_element_type=F32)
    o_ref[...] = epilogue(acc, *extra).astype(o_ref.dtype)


def _ep_plain(acc):
    return acc


def _ep_scale(acc, *, scale):
    return acc * scale


def _ep_rope(acc, cos_ref, sin_ref):
    cos, sin = cos_ref[...], sin_ref[...]
    n = acc.shape[1] // LANES
    return jnp.concatenate(
        [_rope128(acc[:, c * LANES:(c + 1) * LANES], cos, sin) for c in range(n)], axis=1)


def _ep_sigmoid_bias(acc, b_ref):
    return jax.nn.sigmoid(acc + b_ref[...])


def _proj(h, w, epilogue, *, name, extras=(), extra_specs=None, bm_pref=1024, bn_pref=512,
          out_dtype=None):
    m, k = h.shape
    n = w.shape[1]
    bm = _pick(m, bm_pref, BF16_SUBLANES)
    bn = _pick(n, bn_pref)
    specs = [pl.BlockSpec((bm, k), lambda i, j: (i, 0)), pl.BlockSpec((k, bn), lambda i, j: (0, j))]
    specs += list(extra_specs(bm, bn)) if extra_specs else []
    return pl.pallas_call(
        functools.partial(_proj_kernel, epilogue=epilogue),
        grid=(m // bm, n // bn),
        in_specs=specs,
        out_specs=pl.BlockSpec((bm, bn), lambda i, j: (i, j)),
        out_shape=jax.ShapeDtypeStruct((m, n), out_dtype or MXU_DTYPE),
        compiler_params=_cparams("parallel", "arbitrary"),
        name=name,
    )(h, w, *extras)


def _rope_specs(bm, bn):
    tab = pl.BlockSpec((bm, LANES), lambda i, j: (i, 0))
    return [tab, tab]


def _bias_specs(bm, bn):
    return [pl.BlockSpec((1, bn), lambda i, j: (0, j))]


def _latent_kernel(c_ref, g_ref, w_ref, *rest, epilogue):
    *extra, o_ref = rest
    cn = _rms(c_ref[...].astype(F32), g_ref[...]).astype(MXU_DTYPE)
    acc = jnp.dot(cn, w_ref[...], preferred_element_type=F32)
    o_ref[...] = epilogue(acc, *extra).astype(o_ref.dtype)


def _ep_rope_odd_scale(acc, cos_ref, sin_ref, *, scale):
    cos, sin = cos_ref[...], sin_ref[...]
    n = acc.shape[1] // LANES
    cols = []
    for c in range(n):
        t = acc[:, c * LANES:(c + 1) * LANES]
        cols.append(_rope128(t, cos, sin) if c % 2 else t)
    return jnp.concatenate(cols, axis=1) * scale


def _latent_proj(src, col_block, rank, g, w, epilogue, *, name, extras=(), with_tables=False):
    m = src.shape[0]
    n = w.shape[1]
    bm = _pick(m, 512, BF16_SUBLANES)
    specs = [pl.BlockSpec((bm, rank), lambda i: (i, col_block)),
             pl.BlockSpec((1, rank), lambda i: (0, 0)),
             pl.BlockSpec((rank, n), lambda i: (0, 0))]
    if with_tables:
        specs += [pl.BlockSpec((bm, LANES), lambda i: (i, 0))] * 2
    return pl.pallas_call(
        functools.partial(_latent_kernel, epilogue=epilogue),
        grid=(m // bm,),
        in_specs=specs,
        out_specs=pl.BlockSpec((bm, n), lambda i: (i, 0)),
        out_shape=jax.ShapeDtypeStruct((m, n), MXU_DTYPE),
        compiler_params=_cparams("parallel"),
        name=name,
    )(src, g[None, :], w, *extras)


def _flash(q_rows, load_k, load_v, qi, *, bq, bk, dv):
    rows = q_rows.shape[0]

    def step(j, carry, masked):
        m, l, acc = carry
        s = lax.dot_general(q_rows, load_k(j), (((1,), (1,)), ((), ())),
                            preferred_element_type=F32)
        if masked:
            q_chunk = (lax.broadcasted_iota(jnp.int32, s.shape, 0) % bq) // CHUNK + qi * (bq // CHUNK)
            k_chunk = lax.broadcasted_iota(jnp.int32, s.shape, 1) // CHUNK + j * (bk // CHUNK)
            s = jnp.where(k_chunk <= q_chunk, s, -jnp.inf)
        m_new = jnp.maximum(m, s.max(axis=1, keepdims=True))
        alpha = jnp.exp(m - m_new)
        p = jnp.exp(s - m_new)
        l = alpha * l + p.sum(axis=1, keepdims=True)
        acc = alpha * acc + jnp.dot(p.astype(MXU_DTYPE), load_v(j), preferred_element_type=F32)
        return m_new, l, acc

    carry = (jnp.full((rows, 1), -jnp.inf, F32), jnp.zeros((rows, 1), F32), jnp.zeros((rows, dv), F32))
    n_full = qi * (bq // bk)
    carry = lax.fori_loop(0, n_full, lambda j, c: step(j, c, False), carry)
    for d in range(bq // bk):
        carry = step(n_full + d, carry, True)
    _, l, acc = carry
    return acc, l


def _diff_attn_kernel(lam_ref, q_ref, k_ref, v_ref, g_ref, o_ref, *, bq, bk, lam_init):
    qi = pl.program_id(2)
    lam_v = lam_ref[...]
    lam = (jnp.exp(jnp.sum(lam_v[0:1] * lam_v[1:2], axis=1, keepdims=True))
           - jnp.exp(jnp.sum(lam_v[2:3] * lam_v[3:4], axis=1, keepdims=True)) + lam_init)
    q = q_ref[...]
    lane = lax.broadcasted_iota(jnp.int32, q.shape, 1)
    zero = jnp.zeros_like(q)
    q_rows = jnp.concatenate([jnp.where(lane < DIFF_QK_DIM, q, zero),
                              jnp.where(lane < DIFF_QK_DIM, zero, q)], axis=0)

    def load_k(j):
        return k_ref[pl.ds(pl.multiple_of(j * bk, bk), bk), :]

    def load_v(j):
        return v_ref[pl.ds(pl.multiple_of(j * bk, bk), bk), :]

    acc, l = _flash(q_rows, load_k, load_v, qi, bq=bq, bk=bk, dv=DIFF_V_DIM)
    o = acc / l
    o = o[:bq] - lam * o[bq:]
    o_ref[...] = (_rms(o, g_ref[...]) * (1.0 - lam_init)).astype(o_ref.dtype)


def _diff_attention(qk, vplain, lam_vecs, g_sub, lam_init, batch, seq):
    m = qk.shape[0]
    bq = _pick(seq, 256, CHUNK)
    bk = bq
    nq = seq // bq
    return pl.pallas_call(
        functools.partial(_diff_attn_kernel, bq=bq, bk=bk, lam_init=lam_init),
        grid=(batch, DIFF_HEADS, nq),
        in_specs=[
            pl.BlockSpec((4, DIFF_QK_DIM), lambda b, h, i: (0, 0)),
            pl.BlockSpec((bq, LANES), lambda b, h, i: (b * nq + i, h)),
            pl.BlockSpec((seq, LANES), lambda b, h, i: (b, DIFF_HEADS + h)),
            pl.BlockSpec((seq, LANES), lambda b, h, i: (b, h)),
            pl.BlockSpec((1, DIFF_V_DIM), lambda b, h, i: (0, 0)),
        ],
        out_specs=pl.BlockSpec((bq, DIFF_V_DIM), lambda b, h, i: (b * nq + i, h)),
        out_shape=jax.ShapeDtypeStruct((m, DIFF_WIDTH), MXU_DTYPE),
        compiler_params=_cparams("parallel", "parallel", "arbitrary"),
        name="diff_attention",
    )(lam_vecs, qk, qk, vplain, g_sub[None, :])


def _mla_attn_kernel(q_ref, kv_ref, kpe_ref, o_ref, kcat_ref, *, bq, bk):
    qi = pl.program_id(2)

    @pl.when(qi == 0)
    def _():
        kcat_ref[:, :MLA_NOPE_DIM] = kv_ref[:, :MLA_NOPE_DIM]
        kcat_ref[:, MLA_NOPE_DIM:] = kpe_ref[...]

    def load_k(j):
        return kcat_ref[pl.ds(pl.multiple_of(j * bk, bk), bk), :]

    def load_v(j):
        return kv_ref[pl.ds(pl.multiple_of(j * bk, bk), bk), MLA_NOPE_DIM:]

    acc, l = _flash(q_ref[...], load_k, load_v, qi, bq=bq, bk=bk, dv=MLA_V_DIM)
    o_ref[...] = (acc / l).astype(o_ref.dtype)


def _mla_attention(q, kv, kpe, batch, seq):
    m = q.shape[0]
    bq = _pick(seq, 512, CHUNK)
    bk = _pick(bq, 256, CHUNK)
    nq = seq // bq
    hw = MLA_NOPE_DIM + MLA_V_DIM
    return pl.pallas_call(
        functools.partial(_mla_attn_kernel, bq=bq, bk=bk),
        grid=(batch, MLA_HEADS, nq),
        in_specs=[
            pl.BlockSpec((bq, hw), lambda b, h, i: (b * nq + i, h)),
            pl.BlockSpec((seq, hw), lambda b, h, i: (b, h)),
            pl.BlockSpec((seq, LANES), lambda b, h, i: (b, 0)),
        ],
        out_specs=pl.BlockSpec((bq, MLA_V_DIM), lambda b, h, i: (b * nq + i, h)),
        out_shape=jax.ShapeDtypeStruct((m, MLA_WIDTH), MXU_DTYPE),
        scratch_shapes=[pltpu.VMEM((seq, hw), MXU_DTYPE)],
        compiler_params=_cparams("parallel", "parallel", "arbitrary"),
        name="mla_attention",
    )(q, kv, kpe)


def _merge_kernel(od_ref, om_ref, wd_ref, wm_ref, ga_ref, gb_ref, o_ref):
    a = jnp.dot(od_ref[...], wd_ref[...], preferred_element_type=F32)
    b = jnp.dot(om_ref[...], wm_ref[...], preferred_element_type=F32)
    o_ref[...] = (ga_ref[...].astype(F32) * a + gb_ref[...].astype(F32) * b).astype(o_ref.dtype)


def _merge(o_diff, o_mla, w_d, w_m, gates):
    m = o_diff.shape[0]
    n = w_d.shape[1]
    bm = _pick(m, 1024, BF16_SUBLANES)
    bn = _pick(n, 512)
    nb = n // bn
    return pl.pallas_call(
        _merge_kernel,
        grid=(m // bm, nb),
        in_specs=[
            pl.BlockSpec((bm, DIFF_WIDTH), lambda i, j: (i, 0)),
            pl.BlockSpec((bm, MLA_WIDTH), lambda i, j: (i, 0)),
            pl.BlockSpec((DIFF_WIDTH, bn), lambda i, j: (0, j)),
            pl.BlockSpec((MLA_WIDTH, bn), lambda i, j: (0, j)),
            pl.BlockSpec((bm, bn), lambda i, j: (i, j)),
            pl.BlockSpec((bm, bn), lambda i, j: (i, j + nb)),
        ],
        out_specs=pl.BlockSpec((bm, bn), lambda i, j: (i, j)),
        out_shape=jax.ShapeDtypeStruct((m, n), MXU_DTYPE),
        compiler_params=_cparams("parallel", "arbitrary"),
        name="branch_merge",
    )(o_diff, o_mla, w_d, w_m, gates, gates)


def _out_proj_kernel(lhs_ref, w_ref, x_ref, gp_ref, gn_ref, xo_ref, ho_ref, acc_ref, *, nk):
    k = pl.program_id(1)

    @pl.when(k == 0)
    def _():
        acc_ref[...] = jnp.zeros_like(acc_ref)

    acc_ref[...] += jnp.dot(lhs_ref[...], w_ref[...], preferred_element_type=F32)

    @pl.when(k == nk - 1)
    def _():
        xn = x_ref[...] + _rms(acc_ref[...], gp_ref[...])
        xo_ref[...] = xn
        ho_ref[...] = _rms(xn, gn_ref[...]).astype(ho_ref.dtype)


def _out_proj(lhs, w, x, g_post, g_next):
    m, kdim = lhs.shape
    n = w.shape[1]
    bm = _pick(m, 512, BF16_SUBLANES)
    bk = _pick(kdim, 1408)
    nk = kdim // bk
    row = lambda i, k: (i, 0)
    const = lambda i, k: (0, 0)
    return pl.pallas_call(
        functools.partial(_out_proj_kernel, nk=nk),
        grid=(m // bm, nk),
        in_specs=[
            pl.BlockSpec((bm, bk), lambda i, k: (i, k)),
            pl.BlockSpec((bk, n), lambda i, k: (k, 0)),
            pl.BlockSpec((bm, n), row),
            pl.BlockSpec((1, n), const),
            pl.BlockSpec((1, n), const),
        ],
        out_specs=[pl.BlockSpec((bm, n), row), pl.BlockSpec((bm, n), row)],
        out_shape=[jax.ShapeDtypeStruct((m, n), F32), jax.ShapeDtypeStruct((m, n), MXU_DTYPE)],
        scratch_shapes=[pltpu.VMEM((bm, n), F32)],
        compiler_params=_cparams("parallel", "arbitrary"),
        name="out_proj",
    )(lhs, w, x, g_post[None, :], g_next[None, :])


def _mem_attn_kernel(q_ref, kv_ref, o_ref, *, dh):
    width = MEM_HEADS * dh
    outs = []
    for h in range(MEM_HEADS):
        q = q_ref[:, h * dh:(h + 1) * dh]
        k = kv_ref[:, h * dh:(h + 1) * dh]
        v = kv_ref[:, width + h * dh:width + (h + 1) * dh]
        s = lax.dot_general(q, k, (((1,), (1,)), ((), ())), preferred_element_type=F32)
        p = jnp.exp(s - s.max(axis=1, keepdims=True))
        l = p.sum(axis=1, keepdims=True)
        outs.append(jnp.dot(p.astype(MXU_DTYPE), v, preferred_element_type=F32) / l)
    o_ref[...] = jnp.concatenate(outs, axis=1).astype(o_ref.dtype)


def _mem_attention(q, kv, seq, mem_tokens):
    m, d = q.shape
    bq = _pick(seq, 512, BF16_SUBLANES)
    per_batch = seq // bq
    return pl.pallas_call(
        functools.partial(_mem_attn_kernel, dh=d // MEM_HEADS),
        grid=(m // bq,),
        in_specs=[pl.BlockSpec((bq, d), lambda i: (i, 0)),
                  pl.BlockSpec((mem_tokens, 2 * d), lambda i: (i // per_batch, 0))],
        out_specs=pl.BlockSpec((bq, d), lambda i: (i, 0)),
        out_shape=jax.ShapeDtypeStruct((m, d), MXU_DTYPE),
        compiler_params=_cparams("parallel"),
        name="mem_attention",
    )(q, kv)


HALO = BF16_SUBLANES


def _ffn_up_kernel(h_ref, halo_ref, wa_ref, wg_ref, cwa_ref, cwg_ref, cba_ref, cbg_ref, o_ref, hs_ref,
                   *, bm, blocks_per_seq):
    i = pl.program_id(0)

    @pl.when(pl.program_id(1) == 0)
    def _():
        halo = halo_ref[...]
        first = (i % blocks_per_seq) == 0
        hs_ref[:HALO, :] = jnp.where(first, jnp.zeros_like(halo), halo)
        hs_ref[HALO:, :] = h_ref[...]

    hs = hs_ref[...]

    def conv(w_ref, cw_ref, cb_ref):
        u = jnp.dot(hs, w_ref[...], preferred_element_type=F32)
        cw = cw_ref[...]
        c = cb_ref[...]
        for tap in range(CONV_WIDTH):
            off = HALO - (CONV_WIDTH - 1) + tap
            c = c + u[off:off + bm] * cw[tap:tap + 1]
        return c

    a = conv(wa_ref, cwa_ref, cba_ref)
    g = conv(wg_ref, cwg_ref, cbg_ref)
    o_ref[...] = (a * jax.nn.sigmoid(a) * g).astype(o_ref.dtype)


def _ffn_up(h, w_up, conv_w, conv_b, seq):
    m, d = h.shape
    ff = w_up.shape[1] // 2
    bm = _pick(seq, 512, HALO)
    bn = _pick(ff, 512)
    nb = ff // bn
    halo_blocks = bm // HALO
    return pl.pallas_call(
        functools.partial(_ffn_up_kernel, bm=bm, blocks_per_seq=seq // bm),
        grid=(m // bm, nb),
        in_specs=[
            pl.BlockSpec((bm, d), lambda i, j: (i, 0)),
            pl.BlockSpec((HALO, d), lambda i, j: (jnp.maximum(i * halo_blocks - 1, 0), 0)),
            pl.BlockSpec((d, bn), lambda i, j: (0, j)),
            pl.BlockSpec((d, bn), lambda i, j: (0, j + nb)),
            pl.BlockSpec((CONV_WIDTH, bn), lambda i, j: (0, j)),
            pl.BlockSpec((CONV_WIDTH, bn), lambda i, j: (0, j + nb)),
            pl.BlockSpec((1, bn), lambda i, j: (0, j)),
            pl.BlockSpec((1, bn), lambda i, j: (0, j + nb)),
        ],
        out_specs=pl.BlockSpec((bm, bn), lambda i, j: (i, j)),
        out_shape=jax.ShapeDtypeStruct((m, ff), MXU_DTYPE),
        scratch_shapes=[pltpu.VMEM((HALO + bm, d), MXU_DTYPE)],
        compiler_params=_cparams("parallel", "arbitrary"),
        name="ffn_up",
    )(h, h, w_up, w_up, conv_w, conv_w, conv_b[None, :], conv_b[None, :])


def kernel(x, mem, positions, g_pre_mix, w_in, b_gate, lam_q1, lam_k1, lam_q2, lam_k2, g_diff_sub, g_cq,
           w_uq, g_ckv, w_ukv, w_br_diff, w_br_mla, w_mix_out, g_post_mix, g_pre_x, g_mem, w_q_x, w_kv_x,
           w_o_x, g_post_x, g_pre_ffn, w_up, conv_w, conv_b, w_down, g_post_ffn):
    batch, seq, d = x.shape
    mem_tokens = mem.shape[1]
    depth = w_in.shape[0]
    m = batch * seq
    cast = lambda a: a.astype(MXU_DTYPE)

    qw = DIFF_HEADS * 2 * DIFF_QK_DIM
    o_v = 2 * qw
    o_cq = o_v + DIFF_WIDTH
    o_ckv = o_cq + MLA_Q_RANK
    o_kr = o_ckv + MLA_KV_RANK
    o_gt = o_kr + MLA_ROPE_DIM
    assert w_in.shape[2] == o_gt + 2 * d
    assert (o_cq - o_v) % MLA_Q_RANK == 0 and (o_ckv - o_v) % MLA_KV_RANK == 0
    qhead = MLA_NOPE_DIM + MLA_ROPE_DIM

    cos, sin = _rope_tables(positions)
    xf = x.reshape(m, d)
    memf = mem.reshape(batch * mem_tokens, d)
    h = _rmsnorm(xf, g_pre_mix[0])

    for l in range(depth):
        lam_init = 0.8 - 0.6 * math.exp(-0.3 * l)
        wl = w_in[l]
        w_qk = cast(jnp.concatenate([wl[:, :qw] * (DIFF_QK_DIM ** -0.5), wl[:, qw:o_v]], axis=1))
        w_plain = cast(wl[:, o_v:o_kr])
        w_kpe = cast(jnp.pad(wl[:, o_kr:o_gt], ((0, 0), (0, LANES - MLA_ROPE_DIM))))
        w_gate = cast(wl[:, o_gt:])
        w_uq_p = cast(jnp.pad(w_uq[l].reshape(MLA_Q_RANK, MLA_HEADS, qhead),
                              ((0, 0), (0, 0), (0, 2 * LANES - qhead))).reshape(MLA_Q_RANK, -1))

        qk = _proj(h, w_qk, _ep_rope, name="proj_qk", extras=(cos, sin), extra_specs=_rope_specs)
        plain = _proj(h, w_plain, _ep_plain, name="proj_plain", bn_pref=896)
        kpe = _proj(h, w_kpe, _ep_rope, name="proj_kpe", extras=(cos, sin), extra_specs=_rope_specs)
        gates = _proj(h, w_gate, _ep_sigmoid_bias, name="proj_gates", extras=(b_gate[l][None, :],),
                      extra_specs=_bias_specs)
        q_mla = _latent_proj(plain, (o_cq - o_v) // MLA_Q_RANK, MLA_Q_RANK, g_cq[l], w_uq_p,
                             functools.partial(_ep_rope_odd_scale, scale=qhead ** -0.5),
                             name="mla_q", extras=(cos, sin), with_tables=True)
        kv_mla = _latent_proj(plain, (o_ckv - o_v) // MLA_KV_RANK, MLA_KV_RANK, g_ckv[l], cast(w_ukv[l]),
                              _ep_plain, name="mla_kv")
        lam_vecs = jnp.stack([lam_q1[l], lam_k1[l], lam_q2[l], lam_k2[l]])
        o_diff = _diff_attention(qk, plain, lam_vecs, g_diff_sub[l], lam_init, batch, seq)
        o_mla = _mla_attention(q_mla, kv_mla, kpe, batch, seq)
        merged = _merge(o_diff, o_mla, cast(w_br_diff[l]), cast(w_br_mla[l]), gates)
        xf, h = _out_proj(merged, cast(w_mix_out[l]), xf, g_post_mix[l], g_pre_x[l])

        q_x = _proj(h, cast(w_q_x[l]), functools.partial(_ep_scale, scale=(d // MEM_HEADS) ** -0.5),
                    name="proj_qx")
        kv_x = _proj(_rmsnorm(memf, g_mem[l]), cast(w_kv_x[l]), _ep_plain, name="proj_kvx")
        o_x = _mem_attention(q_x, kv_x, seq, mem_tokens)
        xf, h = _out_proj(o_x, cast(w_o_x[l]), xf, g_post_x[l], g_pre_ffn[l])

        act = _ffn_up(h, cast(w_up[l]), conv_w[l], conv_b[l], seq)
        g_next = g_pre_mix[l + 1] if l + 1 < depth else g_pre_mix[0]
        xf, h = _out_proj(act, cast(w_down[l]), xf, g_post_ffn[l], g_next)

    return xf.reshape(batch, seq, d)
```

```python
import functools
import math

import jax
import jax.numpy as jnp
from jax import lax
from jax.experimental import pallas as pl
from jax.experimental.pallas import tpu as pltpu

CHUNK = 64
ROPE_THETA = 10000.0
EPS = 1e-6
DIFF_HEADS = 8
DIFF_QK_DIM = 64
DIFF_V_DIM = 128
MLA_HEADS = 8
MLA_Q_RANK = 512
MLA_KV_RANK = 256
MLA_NOPE_DIM = 128
MLA_ROPE_DIM = 64
MLA_V_DIM = 128
MEM_HEADS = 4
CONV_WIDTH = 3
DIFF_WIDTH = DIFF_HEADS * DIFF_V_DIM
MLA_WIDTH = MLA_HEADS * MLA_V_DIM

LANES = 128
BF16_SUBLANES = 16
VMEM_LIMIT_BYTES = 56 * 2**20

MXU_DTYPE = jnp.bfloat16
F32 = jnp.float32
LOG2E = math.log2(math.e)


def _pick(n, pref, mult=LANES):
    if n <= pref:
        return n
    best = None
    for d in range(mult, pref + 1, mult):
        if n % d == 0:
            best = d
    assert best is not None, (n, pref, mult)
    return best


def _cparams(*sem):
    return pltpu.CompilerParams(dimension_semantics=sem, vmem_limit_bytes=VMEM_LIMIT_BYTES)


def _rms(x, g):
    return x * lax.rsqrt(jnp.mean(x * x, axis=-1, keepdims=True) + EPS) * g


def _rope128(t, cos, sin):
    lane = lax.broadcasted_iota(jnp.int32, t.shape, 1)
    swapped = jnp.where((lane & 32) == 0, pltpu.roll(t, 96, 1), pltpu.roll(t, 32, 1))
    return t * cos + swapped * sin


_NT = (((1,), (1,)), ((), ()))


def _tables_kernel(pos_ref, inv_ref, sgn_ref, cos_ref, sin_ref):
    ang = pos_ref[...].astype(F32) * inv_ref[...]
    cos_ref[...] = jnp.cos(ang)
    sin_ref[...] = jnp.sin(ang) * sgn_ref[...]


def _rope_tables(positions):
    m = positions.size
    d = DIFF_QK_DIM
    inv = ROPE_THETA ** (-jnp.arange(0, d, 2, dtype=F32) / d)
    inv128 = jnp.tile(inv, LANES // (d // 2))[None, :]
    half = jnp.concatenate([-jnp.ones((d // 2,), F32), jnp.ones((d // 2,), F32)])
    sgn128 = jnp.tile(half, LANES // d)[None, :]
    bm = _pick(m, 2048, 8)
    row = pl.BlockSpec((bm, LANES), lambda i: (i, 0))
    const = pl.BlockSpec((1, LANES), lambda i: (0, 0))
    return pl.pallas_call(
        _tables_kernel,
        grid=(m // bm,),
        in_specs=[pl.BlockSpec((bm, 1), lambda i: (i, 0)), const, const],
        out_specs=[row, row],
        out_shape=[jax.ShapeDtypeStruct((m, LANES), F32)] * 2,
        compiler_params=_cparams("parallel"),
        name="rope_tables",
    )(positions.reshape(m, 1), inv128, sgn128)


def _rmsnorm_kernel(x_ref, g_ref, o_ref):
    o_ref[...] = _rms(x_ref[...], g_ref[...]).astype(o_ref.dtype)


def _rmsnorm(x, g):
    m, d = x.shape
    bm = _pick(m, 512, 8)
    return pl.pallas_call(
        _rmsnorm_kernel,
        grid=(m // bm,),
        in_specs=[pl.BlockSpec((bm, d), lambda i: (i, 0)), pl.BlockSpec((1, d), lambda i: (0, 0))],
        out_specs=pl.BlockSpec((bm, d), lambda i: (i, 0)),
        out_shape=jax.ShapeDtypeStruct((m, d), MXU_DTYPE),
        compiler_params=_cparams("parallel"),
        name="rmsnorm",
    )(x, g[None, :])


def _proj_kernel(h_ref, w_ref, *rest, epilogue):
    *extra, o_ref = rest
    acc = jnp.dot(h_ref[...], w_ref[...], preferred_element_type=F32)
    o_ref[...] = epilogue(acc, *extra).astype(o_ref.dtype)


def _ep_plain(acc):
    return acc


def _ep_scale(acc, *, scale):
    return acc * scale


def _ep_rope(acc, cos_ref, sin_ref, *, q_tiles=0):
    cos, sin = cos_ref[...], sin_ref[...]
    n = acc.shape[1] // LANES
    out = jnp.concatenate(
        [_rope128(acc[:, c * LANES:(c + 1) * LANES], cos, sin) for c in range(n)], axis=1)
    if q_tiles:
        out = out * jnp.where(pl.program_id(1) < q_tiles, LOG2E, 1.0)
    return out


def _ep_sigmoid_bias(acc, b_ref):
    return jax.nn.sigmoid(acc + b_ref[...])


def _proj(h, w, epilogue, *, name, extras=(), extra_specs=None, bm_pref=1024, bn_pref=512):
    m, k = h.shape
    n = w.shape[1]
    bm = _pick(m, bm_pref, BF16_SUBLANES)
    bn = _pick(n, bn_pref)
    specs = [pl.BlockSpec((bm, k), lambda i, j: (i, 0)), pl.BlockSpec((k, bn), lambda i, j: (0, j))]
    specs += list(extra_specs(bm, bn)) if extra_specs else []
    return pl.pallas_call(
        functools.partial(_proj_kernel, epilogue=epilogue),
        grid=(m // bm, n // bn),
        in_specs=specs,
        out_specs=pl.BlockSpec((bm, bn), lambda i, j: (i, j)),
        out_shape=jax.ShapeDtypeStruct((m, n), MXU_DTYPE),
        compiler_params=_cparams("parallel", "arbitrary"),
        name=name,
    )(h, w, *extras)


def _rope_specs(bm, bn):
    tab = pl.BlockSpec((bm, LANES), lambda i, j: (i, 0))
    return [tab, tab]


def _bias_specs(bm, bn):
    return [pl.BlockSpec((1, bn), lambda i, j: (0, j))]


def _proj_t_kernel(h_ref, wt_ref, o_ref):
    o_ref[...] = lax.dot_general(wt_ref[...], h_ref[...], _NT,
                                 preferred_element_type=F32).astype(o_ref.dtype)


def _proj_t(h, wt, *, name):
    m, k = h.shape
    n = wt.shape[0]
    bm = _pick(m, 1024)
    bn = _pick(n, 512, BF16_SUBLANES)
    return pl.pallas_call(
        _proj_t_kernel,
        grid=(m // bm, n // bn),
        in_specs=[pl.BlockSpec((bm, k), lambda i, j: (i, 0)), pl.BlockSpec((bn, k), lambda i, j: (j, 0))],
        out_specs=pl.BlockSpec((bn, bm), lambda i, j: (j, i)),
        out_shape=jax.ShapeDtypeStruct((n, m), MXU_DTYPE),
        compiler_params=_cparams("parallel", "arbitrary"),
        name=name,
    )(h, wt)


def _latent_q_kernel(c_ref, g_ref, w_ref, cos_ref, sin_ref, o_ref, *, scale):
    cn = _rms(c_ref[...].astype(F32), g_ref[...]).astype(MXU_DTYPE)
    acc = jnp.dot(cn, w_ref[...], preferred_element_type=F32)
    cos, sin = cos_ref[...], sin_ref[...]
    cols = []
    for c in range(acc.shape[1] // LANES):
        t = acc[:, c * LANES:(c + 1) * LANES]
        cols.append(_rope128(t, cos, sin) if c % 2 else t)
    o_ref[...] = (jnp.concatenate(cols, axis=1) * scale).astype(o_ref.dtype)


def _latent_kv_kernel(c_ref, g_ref, wk_ref, wvt_ref, k_ref, vt_ref):
    cn = _rms(c_ref[...].astype(F32), g_ref[...]).astype(MXU_DTYPE)
    k_ref[...] = jnp.dot(cn, wk_ref[...], preferred_element_type=F32).astype(k_ref.dtype)
    vt_ref[...] = lax.dot_general(wvt_ref[...], cn, _NT, preferred_element_type=F32).astype(vt_ref.dtype)


def _latent_q(src, col_block, g, w, cos, sin, scale):
    m = src.shape[0]
    rank, n = w.shape
    bm = _pick(m, 512, BF16_SUBLANES)
    tab = pl.BlockSpec((bm, LANES), lambda i: (i, 0))
    return pl.pallas_call(
        functools.partial(_latent_q_kernel, scale=scale),
        grid=(m // bm,),
        in_specs=[pl.BlockSpec((bm, rank), lambda i: (i, col_block)),
                  pl.BlockSpec((1, rank), lambda i: (0, 0)),
                  pl.BlockSpec((rank, n), lambda i: (0, 0)), tab, tab],
        out_specs=pl.BlockSpec((bm, n), lambda i: (i, 0)),
        out_shape=jax.ShapeDtypeStruct((m, n), MXU_DTYPE),
        compiler_params=_cparams("parallel"),
        name="mla_q",
    )(src, g[None, :], w, cos, sin)


def _latent_kv(src, col_block, g, wk, wvt):
    m = src.shape[0]
    rank, nk = wk.shape
    nv = wvt.shape[0]
    bm = _pick(m, 512)
    return pl.pallas_call(
        _latent_kv_kernel,
        grid=(m // bm,),
        in_specs=[pl.BlockSpec((bm, rank), lambda i: (i, col_block)),
                  pl.BlockSpec((1, rank), lambda i: (0, 0)),
                  pl.BlockSpec((rank, nk), lambda i: (0, 0)),
                  pl.BlockSpec((nv, rank), lambda i: (0, 0))],
        out_specs=[pl.BlockSpec((bm, nk), lambda i: (i, 0)), pl.BlockSpec((nv, bm), lambda i: (0, i))],
        out_shape=[jax.ShapeDtypeStruct((m, nk), MXU_DTYPE), jax.ShapeDtypeStruct((nv, m), MXU_DTYPE)],
        compiler_params=_cparams("parallel"),
        name="mla_kv",
    )(src, g[None, :], wk, wvt)


def _flash(q_rows, load_k, load_vt, qi, *, bq, bk, dv, unroll):
    rows = q_rows.shape[0]
    diag = bq // bk
    assert unroll % diag == 0
    rel = (lax.broadcasted_iota(jnp.int32, (bk, rows), 0) // CHUNK
           - (lax.broadcasted_iota(jnp.int32, (bk, rows), 1) % bq) // CHUNK)

    def group(jg, carry, count, masked):
        m, l, acc = carry
        blocks = [jg * unroll + u for u in range(count)]
        scores = [lax.dot_general(load_k(j), q_rows, _NT, preferred_element_type=F32) for j in blocks]
        for u, (j, s) in enumerate(zip(blocks, scores)):
            if u >= count - masked:
                s = jnp.where(rel <= qi * (bq // CHUNK) - j * (bk // CHUNK), s, -jnp.inf)
            m_new = jnp.maximum(m, s.max(axis=0, keepdims=True))
            alpha = jnp.exp2(m - m_new)
            p = jnp.exp2(s - m_new)
            l = alpha * l + p.sum(axis=0, keepdims=True)
            acc = alpha * acc + jnp.dot(load_vt(j), p.astype(MXU_DTYPE), preferred_element_type=F32)
            m = m_new
        return m, l, acc

    carry = (jnp.full((1, rows), -jnp.inf, F32), jnp.zeros((1, rows), F32), jnp.zeros((dv, rows), F32))
    n_blocks = (qi + 1) * diag
    n_groups = (n_blocks + unroll - 1) // unroll
    carry = lax.fori_loop(0, n_groups - 1, lambda jg, c: group(jg, c, unroll, 0), carry)
    last = (n_blocks - (n_groups - 1) * unroll) // diag - 1
    tails = [functools.partial(group, count=(t + 1) * diag, masked=diag) for t in range(unroll // diag)]
    carry = lax.switch(last, [lambda c, f=f: f(n_groups - 1, c) for f in tails], carry)
    _, l, acc = carry
    return acc, l


KV_GROUP = 4


def _kv_group(seq, bq, bk):
    group = max(bq // bk, min(KV_GROUP, seq // bk))
    assert (seq // bk) % group == 0 and group % (bq // bk) == 0
    return group


def _diff_attn_kernel(lam_ref, q_ref, k_ref, vt_ref, g_ref, o_ref, *, bq, bk, lam_init, unroll):
    qi = pl.program_id(2)
    lam_v = lam_ref[...]
    lam = (jnp.exp(jnp.sum(lam_v[0:1] * lam_v[1:2], axis=1, keepdims=True))
           - jnp.exp(jnp.sum(lam_v[2:3] * lam_v[3:4], axis=1, keepdims=True)) + lam_init)
    q = q_ref[...]
    lane = lax.broadcasted_iota(jnp.int32, q.shape, 1)
    zero = jnp.zeros_like(q)
    q_rows = jnp.concatenate([jnp.where(lane < DIFF_QK_DIM, q, zero),
                              jnp.where(lane < DIFF_QK_DIM, zero, q)], axis=0)

    def load_k(j):
        return k_ref[pl.ds(pl.multiple_of(j * bk, bk), bk), :]

    def load_vt(j):
        return vt_ref[:, pl.ds(pl.multiple_of(j * bk, bk), bk)]

    acc, l = _flash(q_rows, load_k, load_vt, qi, bq=bq, bk=bk, dv=DIFF_V_DIM, unroll=unroll)
    o = acc * (1.0 / l)
    o = o[:, :bq] - lam * o[:, bq:]
    o = o * lax.rsqrt(jnp.mean(o * o, axis=0, keepdims=True) + EPS) * g_ref[...]
    o_ref[...] = (o * (1.0 - lam_init)).T.astype(o_ref.dtype)


def _diff_attention(qk, vt, lam_vecs, g_sub, lam_init, batch, seq):
    m = qk.shape[0]
    bq = _pick(seq, 512, CHUNK)
    bk = _pick(bq, 256, CHUNK)
    nq = seq // bq
    unroll = _kv_group(seq, bq, bk)
    return pl.pallas_call(
        functools.partial(_diff_attn_kernel, bq=bq, bk=bk, lam_init=lam_init, unroll=unroll),
        grid=(batch, DIFF_HEADS, nq),
        in_specs=[
            pl.BlockSpec((4, DIFF_QK_DIM), lambda b, h, i: (0, 0)),
            pl.BlockSpec((bq, LANES), lambda b, h, i: (b * nq + i, h)),
            pl.BlockSpec((seq, LANES), lambda b, h, i: (b, DIFF_HEADS + h)),
            pl.BlockSpec((DIFF_V_DIM, seq), lambda b, h, i: (h, b)),
            pl.BlockSpec((DIFF_V_DIM, 1), lambda b, h, i: (0, 0)),
        ],
        out_specs=pl.BlockSpec((bq, DIFF_V_DIM), lambda b, h, i: (b * nq + i, h)),
        out_shape=jax.ShapeDtypeStruct((m, DIFF_WIDTH), MXU_DTYPE),
        compiler_params=_cparams("parallel", "parallel", "arbitrary"),
        name="diff_attention",
    )(lam_vecs, qk, qk, vt, g_sub[:, None])


def _mla_attn_kernel(q_ref, kn_ref, kpe_ref, vt_ref, o_ref, kcat_ref, *, bq, bk, unroll):
    qi = pl.program_id(2)

    @pl.when(qi == 0)
    def _():
        kcat_ref[:, :MLA_NOPE_DIM] = kn_ref[...]
        kcat_ref[:, MLA_NOPE_DIM:] = kpe_ref[...]

    def load_k(j):
        return kcat_ref[pl.ds(pl.multiple_of(j * bk, bk), bk), :]

    def load_vt(j):
        return vt_ref[:, pl.ds(pl.multiple_of(j * bk, bk), bk)]

    acc, l = _flash(q_ref[...], load_k, load_vt, qi, bq=bq, bk=bk, dv=MLA_V_DIM, unroll=unroll)
    o_ref[...] = (acc * (1.0 / l)).T.astype(o_ref.dtype)


def _mla_attention(q, k_nope, kpe, vt, batch, seq):
    m = q.shape[0]
    bq = _pick(seq, 512, CHUNK)
    bk = _pick(bq, 256, CHUNK)
    nq = seq // bq
    qw = 2 * LANES
    unroll = _kv_group(seq, bq, bk)
    return pl.pallas_call(
        functools.partial(_mla_attn_kernel, bq=bq, bk=bk, unroll=unroll),
        grid=(batch, MLA_HEADS, nq),
        in_specs=[
            pl.BlockSpec((bq, qw), lambda b, h, i: (b * nq + i, h)),
            pl.BlockSpec((seq, MLA_NOPE_DIM), lambda b, h, i: (b, h)),
            pl.BlockSpec((seq, LANES), lambda b, h, i: (b, 0)),
            pl.BlockSpec((MLA_V_DIM, seq), lambda b, h, i: (h, b)),
        ],
        out_specs=pl.BlockSpec((bq, MLA_V_DIM), lambda b, h, i: (b * nq + i, h)),
        out_shape=jax.ShapeDtypeStruct((m, MLA_WIDTH), MXU_DTYPE),
        scratch_shapes=[pltpu.VMEM((seq, qw), MXU_DTYPE)],
        compiler_params=_cparams("parallel", "parallel", "arbitrary"),
        name="mla_attention",
    )(q, k_nope, kpe, vt)


def _merge_kernel(od_ref, om_ref, wd_ref, wm_ref, ga_ref, gb_ref, o_ref):
    a = jnp.dot(od_ref[...], wd_ref[...], preferred_element_type=F32)
    b = jnp.dot(om_ref[...], wm_ref[...], preferred_element_type=F32)
    o_ref[...] = (ga_ref[...].astype(F32) * a + gb_ref[...].astype(F32) * b).astype(o_ref.dtype)


def _merge(o_diff, o_mla, w_d, w_m, gates):
    m = o_diff.shape[0]
    n = w_d.shape[1]
    bm = _pick(m, 1024, BF16_SUBLANES)
    bn = _pick(n, 512)
    nb = n // bn
    return pl.pallas_call(
        _merge_kernel,
        grid=(m // bm, nb),
        in_specs=[
            pl.BlockSpec((bm, DIFF_WIDTH), lambda i, j: (i, 0)),
            pl.BlockSpec((bm, MLA_WIDTH), lambda i, j: (i, 0)),
            pl.BlockSpec((DIFF_WIDTH, bn), lambda i, j: (0, j)),
            pl.BlockSpec((MLA_WIDTH, bn), lambda i, j: (0, j)),
            pl.BlockSpec((bm, bn), lambda i, j: (i, j)),
            pl.BlockSpec((bm, bn), lambda i, j: (i, j + nb)),
        ],
        out_specs=pl.BlockSpec((bm, bn), lambda i, j: (i, j)),
        out_shape=jax.ShapeDtypeStruct((m, n), MXU_DTYPE),
        compiler_params=_cparams("parallel", "arbitrary"),
        name="branch_merge",
    )(o_diff, o_mla, w_d, w_m, gates, gates)


def _out_proj_kernel(lhs_ref, w_ref, x_ref, gp_ref, gn_ref, xo_ref, ho_ref, acc_ref, *, nk):
    k = pl.program_id(1)

    @pl.when(k == 0)
    def _():
        acc_ref[...] = jnp.zeros_like(acc_ref)

    acc_ref[...] += jnp.dot(lhs_ref[...], w_ref[...], preferred_element_type=F32)

    @pl.when(k == nk - 1)
    def _():
        xn = x_ref[...] + _rms(acc_ref[...], gp_ref[...])
        xo_ref[...] = xn
        ho_ref[...] = _rms(xn, gn_ref[...]).astype(ho_ref.dtype)


def _out_proj(lhs, w, x, g_post, g_next):
    m, kdim = lhs.shape
    n = w.shape[1]
    bm = _pick(m, 512, BF16_SUBLANES)
    bk = _pick(kdim, 1408)
    nk = kdim // bk
    row = lambda i, k: (i, 0)
    const = lambda i, k: (0, 0)
    return pl.pallas_call(
        functools.partial(_out_proj_kernel, nk=nk),
        grid=(m // bm, nk),
        in_specs=[
            pl.BlockSpec((bm, bk), lambda i, k: (i, k)),
            pl.BlockSpec((bk, n), lambda i, k: (k, 0)),
            pl.BlockSpec((bm, n), row),
            pl.BlockSpec((1, n), const),
            pl.BlockSpec((1, n), const),
        ],
        out_specs=[pl.BlockSpec((bm, n), row), pl.BlockSpec((bm, n), row)],
        out_shape=[jax.ShapeDtypeStruct((m, n), F32), jax.ShapeDtypeStruct((m, n), MXU_DTYPE)],
        scratch_shapes=[pltpu.VMEM((bm, n), F32)],
        compiler_params=_cparams("parallel", "arbitrary"),
        name="out_proj",
    )(lhs, w, x, g_post[None, :], g_next[None, :])


def _mem_attn_kernel(q_ref, kv_ref, o_ref, *, dh):
    width = MEM_HEADS * dh
    outs = []
    for h in range(MEM_HEADS):
        q = q_ref[:, h * dh:(h + 1) * dh]
        k = kv_ref[:, h * dh:(h + 1) * dh]
        v = kv_ref[:, width + h * dh:width + (h + 1) * dh]
        s = lax.dot_general(q, k, _NT, preferred_element_type=F32)
        p = jnp.exp(s - s.max(axis=1, keepdims=True))
        l = p.sum(axis=1, keepdims=True)
        outs.append(jnp.dot(p.astype(MXU_DTYPE), v, preferred_element_type=F32) / l)
    o_ref[...] = jnp.concatenate(outs, axis=1).astype(o_ref.dtype)


def _mem_attention(q, kv, seq, mem_tokens):
    m, d = q.shape
    bq = _pick(seq, 512, BF16_SUBLANES)
    per_batch = seq // bq
    return pl.pallas_call(
        functools.partial(_mem_attn_kernel, dh=d // MEM_HEADS),
        grid=(m // bq,),
        in_specs=[pl.BlockSpec((bq, d), lambda i: (i, 0)),
                  pl.BlockSpec((mem_tokens, 2 * d), lambda i: (i // per_batch, 0))],
        out_specs=pl.BlockSpec((bq, d), lambda i: (i, 0)),
        out_shape=jax.ShapeDtypeStruct((m, d), MXU_DTYPE),
        compiler_params=_cparams("parallel"),
        name="mem_attention",
    )(q, kv)


HALO = BF16_SUBLANES


def _ffn_up_kernel(h_ref, halo_ref, wa_ref, wg_ref, cwa_ref, cwg_ref, cba_ref, cbg_ref, o_ref, hs_ref,
                   *, bm, blocks_per_seq):
    i = pl.program_id(0)

    @pl.when(pl.program_id(1) == 0)
    def _():
        halo = halo_ref[...]
        first = (i % blocks_per_seq) == 0
        hs_ref[:HALO, :] = jnp.where(first, jnp.zeros_like(halo), halo)
        hs_ref[HALO:, :] = h_ref[...]

    hs = hs_ref[...]

    def conv(w_ref, cw_ref, cb_ref):
        u = jnp.dot(hs, w_ref[...], preferred_element_type=F32)
        cw = cw_ref[...]
        c = cb_ref[...]
        for tap in range(CONV_WIDTH):
            off = HALO - (CONV_WIDTH - 1) + tap
            c = c + u[off:off + bm] * cw[tap:tap + 1]
        return c

    a = conv(wa_ref, cwa_ref, cba_ref)
    g = conv(wg_ref, cwg_ref, cbg_ref)
    o_ref[...] = (a * jax.nn.sigmoid(a) * g).astype(o_ref.dtype)


def _ffn_up(h, w_up, conv_w, conv_b, seq):
    m, d = h.shape
    ff = w_up.shape[1] // 2
    bm = _pick(seq, 512, HALO)
    bn = _pick(ff, 512)
    nb = ff // bn
    halo_blocks = bm // HALO
    return pl.pallas_call(
        functools.partial(_ffn_up_kernel, bm=bm, blocks_per_seq=seq // bm),
        grid=(m // bm, nb),
        in_specs=[
            pl.BlockSpec((bm, d), lambda i, j: (i, 0)),
            pl.BlockSpec((HALO, d), lambda i, j: (jnp.maximum(i * halo_blocks - 1, 0), 0)),
            pl.BlockSpec((d, bn), lambda i, j: (0, j)),
            pl.BlockSpec((d, bn), lambda i, j: (0, j + nb)),
            pl.BlockSpec((CONV_WIDTH, bn), lambda i, j: (0, j)),
            pl.BlockSpec((CONV_WIDTH, bn), lambda i, j: (0, j + nb)),
            pl.BlockSpec((1, bn), lambda i, j: (0, j)),
            pl.BlockSpec((1, bn), lambda i, j: (0, j + nb)),
        ],
        out_specs=pl.BlockSpec((bm, bn), lambda i, j: (i, j)),
        out_shape=jax.ShapeDtypeStruct((m, ff), MXU_DTYPE),
        scratch_shapes=[pltpu.VMEM((HALO + bm, d), MXU_DTYPE)],
        compiler_params=_cparams("parallel", "arbitrary"),
        name="ffn_up",
    )(h, h, w_up, w_up, conv_w, conv_w, conv_b[None, :], conv_b[None, :])


def kernel(x, mem, positions, g_pre_mix, w_in, b_gate, lam_q1, lam_k1, lam_q2, lam_k2, g_diff_sub, g_cq,
           w_uq, g_ckv, w_ukv, w_br_diff, w_br_mla, w_mix_out, g_post_mix, g_pre_x, g_mem, w_q_x, w_kv_x,
           w_o_x, g_post_x, g_pre_ffn, w_up, conv_w, conv_b, w_down, g_post_ffn):
    batch, seq, d = x.shape
    mem_tokens = mem.shape[1]
    depth = w_in.shape[0]
    m = batch * seq
    cast = lambda a: a.astype(MXU_DTYPE)

    qw = DIFF_HEADS * 2 * DIFF_QK_DIM
    o_v = 2 * qw
    o_cq = o_v + DIFF_WIDTH
    o_ckv = o_cq + MLA_Q_RANK
    o_kr = o_ckv + MLA_KV_RANK
    o_gt = o_kr + MLA_ROPE_DIM
    assert w_in.shape[2] == o_gt + 2 * d
    assert MLA_Q_RANK % MLA_KV_RANK == 0
    qhead = MLA_NOPE_DIM + MLA_ROPE_DIM
    kvhead = MLA_NOPE_DIM + MLA_V_DIM

    cos, sin = _rope_tables(positions)
    xf = x.reshape(m, d)
    memf = mem.reshape(batch * mem_tokens, d)
    h = _rmsnorm(xf, g_pre_mix[0])

    for l in range(depth):
        lam_init = 0.8 - 0.6 * math.exp(-0.3 * l)
        wl = w_in[l]
        w_qk = cast(jnp.concatenate([wl[:, :qw] * (DIFF_QK_DIM ** -0.5), wl[:, qw:o_v]], axis=1))
        w_vt = cast(wl[:, o_v:o_cq].T)
        w_lat = cast(wl[:, o_cq:o_kr])
        w_kpe = cast(jnp.pad(wl[:, o_kr:o_gt], ((0, 0), (0, LANES - MLA_ROPE_DIM))))
        w_gate = cast(wl[:, o_gt:])
        w_uq_p = cast(jnp.pad(w_uq[l].reshape(MLA_Q_RANK, MLA_HEADS, qhead),
                              ((0, 0), (0, 0), (0, 2 * LANES - qhead))).reshape(MLA_Q_RANK, -1))
        w_ukv_h = w_ukv[l].reshape(MLA_KV_RANK, MLA_HEADS, kvhead)
        w_uk = cast(w_ukv_h[:, :, :MLA_NOPE_DIM].reshape(MLA_KV_RANK, -1))
        w_uvt = cast(w_ukv_h[:, :, MLA_NOPE_DIM:].reshape(MLA_KV_RANK, -1).T)

        bn_qk = _pick(2 * qw, 512)
        qk = _proj(h, w_qk, functools.partial(_ep_rope, q_tiles=qw // bn_qk), name="proj_qk",
                   extras=(cos, sin), extra_specs=_rope_specs)
        vt = _proj_t(h, w_vt, name="proj_vt")
        lat = _proj(h, w_lat, _ep_plain, name="proj_latent", bn_pref=MLA_Q_RANK + MLA_KV_RANK)
        kpe = _proj(h, w_kpe, _ep_rope, name="proj_kpe", extras=(cos, sin), extra_specs=_rope_specs)
        gates = _proj(h, w_gate, _ep_sigmoid_bias, name="proj_gates", extras=(b_gate[l][None, :],),
                      extra_specs=_bias_specs)
        q_mla = _latent_q(lat, 0, g_cq[l], w_uq_p, cos, sin, qhead ** -0.5 * LOG2E)
        k_mla, vt_mla = _latent_kv(lat, MLA_Q_RANK // MLA_KV_RANK, g_ckv[l], w_uk, w_uvt)
        lam_vecs = jnp.stack([lam_q1[l], lam_k1[l], lam_q2[l], lam_k2[l]])
        o_diff = _diff_attention(qk, vt, lam_vecs, g_diff_sub[l], lam_init, batch, seq)
        o_mla = _mla_attention(q_mla, k_mla, kpe, vt_mla, batch, seq)
        merged = _merge(o_diff, o_mla, cast(w_br_diff[l]), cast(w_br_mla[l]), gates)
        xf, h = _out_proj(merged, cast(w_mix_out[l]), xf, g_post_mix[l], g_pre_x[l])

        q_x = _proj(h, cast(w_q_x[l]), functools.partial(_ep_scale, scale=(d // MEM_HEADS) ** -0.5),
                    name="proj_qx")
        kv_x = _proj(_rmsnorm(memf, g_mem[l]), cast(w_kv_x[l]), _ep_plain, name="proj_kvx")
        o_x = _mem_attention(q_x, kv_x, seq, mem_tokens)
        xf, h = _out_proj(o_x, cast(w_o_x[l]), xf, g_post_x[l], g_pre_ffn[l])

        act = _ffn_up(h, cast(w_up[l]), conv_w[l], conv_b[l], seq)
        g_next = g_pre_mix[l + 1] if l + 1 < depth else g_pre_mix[0]
        xf, h = _out_proj(act, cast(w_down[l]), xf, g_post_ffn[l], g_next)

    return xf.reshape(batch, seq, d)
```

```python
import functools
import math

import jax
import jax.numpy as jnp
from jax import lax
from jax.experimental import pallas as pl
from jax.experimental.pallas import tpu as pltpu

CHUNK = 64
ROPE_THETA = 10000.0
EPS = 1e-6
DIFF_HEADS = 8
DIFF_QK_DIM = 64
DIFF_V_DIM = 128
MLA_HEADS = 8
MLA_Q_RANK = 512
MLA_KV_RANK = 256
MLA_NOPE_DIM = 128
MLA_ROPE_DIM = 64
MLA_V_DIM = 128
MEM_HEADS = 4
CONV_WIDTH = 3
DIFF_WIDTH = DIFF_HEADS * DIFF_V_DIM
MLA_WIDTH = MLA_HEADS * MLA_V_DIM

LANES = 128
BF16_SUBLANES = 16
MXU_COLS = 256
VMEM_LIMIT_BYTES = 56 * 2**20

MXU_DTYPE = jnp.bfloat16
F32 = jnp.float32
LOG2E = math.log2(math.e)


def _pick(n, pref, mult=LANES):
    if n <= pref:
        return n
    best = None
    for d in range(mult, pref + 1, mult):
        if n % d == 0:
            best = d
    assert best is not None, (n, pref, mult)
    return best


def _cparams(*sem, flags=None):
    return pltpu.CompilerParams(dimension_semantics=sem, vmem_limit_bytes=VMEM_LIMIT_BYTES, flags=flags)


def _rms(x, g):
    return x * lax.rsqrt(jnp.mean(x * x, axis=-1, keepdims=True) + EPS) * g


def _rope128(t, cos, sin):
    lane = lax.broadcasted_iota(jnp.int32, t.shape, 1)
    swapped = jnp.where((lane & 32) == 0, pltpu.roll(t, 96, 1), pltpu.roll(t, 32, 1))
    return t * cos + swapped * sin


_NT = (((1,), (1,)), ((), ()))


def _tables_kernel(pos_ref, inv_ref, sgn_ref, cos_ref, sin_ref):
    ang = pos_ref[...].astype(F32) * inv_ref[...]
    cos_ref[...] = jnp.cos(ang)
    sin_ref[...] = jnp.sin(ang) * sgn_ref[...]


def _rope_tables(positions):
    m = positions.size
    d = DIFF_QK_DIM
    inv = ROPE_THETA ** (-jnp.arange(0, d, 2, dtype=F32) / d)
    inv128 = jnp.tile(inv, LANES // (d // 2))[None, :]
    half = jnp.concatenate([-jnp.ones((d // 2,), F32), jnp.ones((d // 2,), F32)])
    sgn128 = jnp.tile(half, LANES // d)[None, :]
    bm = _pick(m, 2048, 8)
    row = pl.BlockSpec((bm, LANES), lambda i: (i, 0))
    const = pl.BlockSpec((1, LANES), lambda i: (0, 0))
    return pl.pallas_call(
        _tables_kernel,
        grid=(m // bm,),
        in_specs=[pl.BlockSpec((bm, 1), lambda i: (i, 0)), const, const],
        out_specs=[row, row],
        out_shape=[jax.ShapeDtypeStruct((m, LANES), F32)] * 2,
        compiler_params=_cparams("parallel"),
        name="rope_tables",
    )(positions.reshape(m, 1), inv128, sgn128)


def _rmsnorm_kernel(x_ref, g_ref, o_ref):
    o_ref[...] = _rms(x_ref[...], g_ref[...]).astype(o_ref.dtype)


def _rmsnorm(x, g):
    m, d = x.shape
    bm = _pick(m, 512, 8)
    return pl.pallas_call(
        _rmsnorm_kernel,
        grid=(m // bm,),
        in_specs=[pl.BlockSpec((bm, d), lambda i: (i, 0)), pl.BlockSpec((1, d), lambda i: (0, 0))],
        out_specs=pl.BlockSpec((bm, d), lambda i: (i, 0)),
        out_shape=jax.ShapeDtypeStruct((m, d), MXU_DTYPE),
        compiler_params=_cparams("parallel"),
        name="rmsnorm",
    )(x, g[None, :])


def _proj_kernel(h_ref, w_ref, *rest, epilogue, sub):
    *extra, o_ref = rest
    h = h_ref[...]
    for t in range(w_ref.shape[1] // sub):
        cols = slice(t * sub, (t + 1) * sub)
        acc = jnp.dot(h, w_ref[:, cols], preferred_element_type=F32)
        o_ref[:, cols] = epilogue(acc, cols, *extra).astype(o_ref.dtype)


def _ep_plain(acc, cols):
    return acc


def _ep_scale(acc, cols, *, scale):
    return acc * scale


def _ep_rope(acc, cols, cos_ref, sin_ref, *, q_tiles=0):
    cos, sin = cos_ref[...], sin_ref[...]
    n = acc.shape[1] // LANES
    out = jnp.concatenate(
        [_rope128(acc[:, c * LANES:(c + 1) * LANES], cos, sin) for c in range(n)], axis=1)
    if q_tiles:
        out = out * jnp.where(pl.program_id(1) < q_tiles, LOG2E, 1.0)
    return out


def _ep_sigmoid_bias(acc, cols, b_ref):
    return jax.nn.sigmoid(acc + b_ref[:, cols])


def _proj(h, w, epilogue, *, name, extras=(), extra_specs=None, bm_pref=1024, bn_pref=512):
    m, k = h.shape
    n = w.shape[1]
    bm = _pick(m, bm_pref, BF16_SUBLANES)
    bn = _pick(n, bn_pref)
    specs = [pl.BlockSpec((bm, k), lambda i, j: (i, 0)), pl.BlockSpec((k, bn), lambda i, j: (0, j))]
    specs += list(extra_specs(bm, bn)) if extra_specs else []
    return pl.pallas_call(
        functools.partial(_proj_kernel, epilogue=epilogue, sub=_pick(bn, MXU_COLS)),
        grid=(m // bm, n // bn),
        in_specs=specs,
        out_specs=pl.BlockSpec((bm, bn), lambda i, j: (i, j)),
        out_shape=jax.ShapeDtypeStruct((m, n), MXU_DTYPE),
        compiler_params=_cparams("parallel", "arbitrary"),
        name=name,
    )(h, w, *extras)


def _rope_specs(bm, bn):
    tab = pl.BlockSpec((bm, LANES), lambda i, j: (i, 0))
    return [tab, tab]


def _bias_specs(bm, bn):
    return [pl.BlockSpec((1, bn), lambda i, j: (0, j))]


def _proj_t_kernel(h_ref, wt_ref, o_ref):
    o_ref[...] = lax.dot_general(wt_ref[...], h_ref[...], _NT,
                                 preferred_element_type=F32).astype(o_ref.dtype)


def _proj_t(h, wt, *, name):
    m, k = h.shape
    n = wt.shape[0]
    bm = _pick(m, 1024)
    bn = _pick(n, 512, BF16_SUBLANES)
    return pl.pallas_call(
        _proj_t_kernel,
        grid=(m // bm, n // bn),
        in_specs=[pl.BlockSpec((bm, k), lambda i, j: (i, 0)), pl.BlockSpec((bn, k), lambda i, j: (j, 0))],
        out_specs=pl.BlockSpec((bn, bm), lambda i, j: (j, i)),
        out_shape=jax.ShapeDtypeStruct((n, m), MXU_DTYPE),
        compiler_params=_cparams("parallel", "arbitrary"),
        name=name,
    )(h, wt)


def _latent_q_kernel(c_ref, g_ref, w_ref, cos_ref, sin_ref, o_ref, *, scale):
    cn = _rms(c_ref[...].astype(F32), g_ref[...]).astype(MXU_DTYPE)
    acc = jnp.dot(cn, w_ref[...], preferred_element_type=F32)
    cos, sin = cos_ref[...], sin_ref[...]
    cols = []
    for c in range(acc.shape[1] // LANES):
        t = acc[:, c * LANES:(c + 1) * LANES]
        cols.append(_rope128(t, cos, sin) if c % 2 else t)
    o_ref[...] = (jnp.concatenate(cols, axis=1) * scale).astype(o_ref.dtype)


def _latent_kv_kernel(c_ref, g_ref, wk_ref, wvt_ref, k_ref, vt_ref):
    cn = _rms(c_ref[...].astype(F32), g_ref[...]).astype(MXU_DTYPE)
    k_ref[...] = jnp.dot(cn, wk_ref[...], preferred_element_type=F32).astype(k_ref.dtype)
    vt_ref[...] = lax.dot_general(wvt_ref[...], cn, _NT, preferred_element_type=F32).astype(vt_ref.dtype)


def _latent_q(src, col_block, g, w, cos, sin, scale):
    m = src.shape[0]
    rank, n = w.shape
    bm = _pick(m, 512, BF16_SUBLANES)
    tab = pl.BlockSpec((bm, LANES), lambda i: (i, 0))
    return pl.pallas_call(
        functools.partial(_latent_q_kernel, scale=scale),
        grid=(m // bm,),
        in_specs=[pl.BlockSpec((bm, rank), lambda i: (i, col_block)),
                  pl.BlockSpec((1, rank), lambda i: (0, 0)),
                  pl.BlockSpec((rank, n), lambda i: (0, 0)), tab, tab],
        out_specs=pl.BlockSpec((bm, n), lambda i: (i, 0)),
        out_shape=jax.ShapeDtypeStruct((m, n), MXU_DTYPE),
        compiler_params=_cparams("parallel"),
        name="mla_q",
    )(src, g[None, :], w, cos, sin)


def _latent_kv(src, col_block, g, wk, wvt):
    m = src.shape[0]
    rank, nk = wk.shape
    nv = wvt.shape[0]
    bm = _pick(m, 512)
    return pl.pallas_call(
        _latent_kv_kernel,
        grid=(m // bm,),
        in_specs=[pl.BlockSpec((bm, rank), lambda i: (i, col_block)),
                  pl.BlockSpec((1, rank), lambda i: (0, 0)),
                  pl.BlockSpec((rank, nk), lambda i: (0, 0)),
                  pl.BlockSpec((nv, rank), lambda i: (0, 0))],
        out_specs=[pl.BlockSpec((bm, nk), lambda i: (i, 0)), pl.BlockSpec((nv, bm), lambda i: (0, i))],
        out_shape=[jax.ShapeDtypeStruct((m, nk), MXU_DTYPE), jax.ShapeDtypeStruct((nv, m), MXU_DTYPE)],
        compiler_params=_cparams("parallel"),
        name="mla_kv",
    )(src, g[None, :], wk, wvt)


def _flash(q_rows, load_k, load_vt, qi, *, bq, bk, dv, unroll):
    rows = q_rows.shape[0]
    diag = bq // bk
    assert unroll % diag == 0
    rel = (lax.broadcasted_iota(jnp.int32, (bk, rows), 0) // CHUNK
           - (lax.broadcasted_iota(jnp.int32, (bk, rows), 1) % bq) // CHUNK)

    def group(jg, carry, count, masked):
        m, l, acc = carry
        blocks = [jg * unroll + u for u in range(count)]
        scores = [lax.dot_general(load_k(j), q_rows, _NT, preferred_element_type=F32) for j in blocks]
        for u, (j, s) in enumerate(zip(blocks, scores)):
            if u >= count - masked:
                s = jnp.where(rel <= qi * (bq // CHUNK) - j * (bk // CHUNK), s, -jnp.inf)
            m_new = jnp.maximum(m, s.max(axis=0, keepdims=True))
            alpha = jnp.exp2(m - m_new)
            p = jnp.exp2(s - m_new)
            l = alpha * l + p.sum(axis=0, keepdims=True)
            acc = alpha * acc + jnp.dot(load_vt(j), p.astype(MXU_DTYPE), preferred_element_type=F32)
            m = m_new
        return m, l, acc

    carry = (jnp.full((1, rows), -jnp.inf, F32), jnp.zeros((1, rows), F32), jnp.zeros((dv, rows), F32))
    n_blocks = (qi + 1) * diag
    n_groups = (n_blocks + unroll - 1) // unroll
    carry = lax.fori_loop(0, n_groups - 1, lambda jg, c: group(jg, c, unroll, 0), carry)
    last = (n_blocks - (n_groups - 1) * unroll) // diag - 1
    tails = [functools.partial(group, count=(t + 1) * diag, masked=diag) for t in range(unroll // diag)]
    carry = lax.switch(last, [lambda c, f=f: f(n_groups - 1, c) for f in tails], carry)
    _, l, acc = carry
    return acc, l


KV_GROUP = 4


def _kv_group(seq, bq, bk):
    group = max(bq // bk, min(KV_GROUP, seq // bk))
    assert (seq // bk) % group == 0 and group % (bq // bk) == 0
    return group


def _diff_attn_kernel(lam_ref, q_ref, k_ref, vt_ref, g_ref, o_ref, *, bq, bk, lam_init, unroll):
    qi = pl.program_id(2)
    lam_v = lam_ref[...]
    lam = (jnp.exp(jnp.sum(lam_v[0:1] * lam_v[1:2], axis=1, keepdims=True))
           - jnp.exp(jnp.sum(lam_v[2:3] * lam_v[3:4], axis=1, keepdims=True)) + lam_init)
    q = q_ref[...]
    lane = lax.broadcasted_iota(jnp.int32, q.shape, 1)
    zero = jnp.zeros_like(q)
    q_rows = jnp.concatenate([jnp.where(lane < DIFF_QK_DIM, q, zero),
                              jnp.where(lane < DIFF_QK_DIM, zero, q)], axis=0)

    def load_k(j):
        return k_ref[pl.ds(pl.multiple_of(j * bk, bk), bk), :]

    def load_vt(j):
        return vt_ref[:, pl.ds(pl.multiple_of(j * bk, bk), bk)]

    acc, l = _flash(q_rows, load_k, load_vt, qi, bq=bq, bk=bk, dv=DIFF_V_DIM, unroll=unroll)
    o = acc * (1.0 / l)
    o = o[:, :bq] - lam * o[:, bq:]
    o = o * lax.rsqrt(jnp.mean(o * o, axis=0, keepdims=True) + EPS) * g_ref[...]
    o_ref[...] = (o * (1.0 - lam_init)).T.astype(o_ref.dtype)


def _diff_attention(qk, vt, lam_vecs, g_sub, lam_init, batch, seq):
    m = qk.shape[0]
    bq = _pick(seq, 512, CHUNK)
    bk = _pick(bq, 256, CHUNK)
    nq = seq // bq
    unroll = _kv_group(seq, bq, bk)
    return pl.pallas_call(
        functools.partial(_diff_attn_kernel, bq=bq, bk=bk, lam_init=lam_init, unroll=unroll),
        grid=(batch, DIFF_HEADS, nq),
        in_specs=[
            pl.BlockSpec((4, DIFF_QK_DIM), lambda b, h, i: (0, 0)),
            pl.BlockSpec((bq, LANES), lambda b, h, i: (b * nq + i, h)),
            pl.BlockSpec((seq, LANES), lambda b, h, i: (b, DIFF_HEADS + h)),
            pl.BlockSpec((DIFF_V_DIM, seq), lambda b, h, i: (h, b)),
            pl.BlockSpec((DIFF_V_DIM, 1), lambda b, h, i: (0, 0)),
        ],
        out_specs=pl.BlockSpec((bq, DIFF_V_DIM), lambda b, h, i: (b * nq + i, h)),
        out_shape=jax.ShapeDtypeStruct((m, DIFF_WIDTH), MXU_DTYPE),
        compiler_params=_cparams("parallel", "parallel", "arbitrary"),
        name="diff_attention",
    )(lam_vecs, qk, qk, vt, g_sub[:, None])


def _mla_attn_kernel(q_ref, kn_ref, kpe_ref, vt_ref, o_ref, kcat_ref, *, bq, bk, unroll):
    qi = pl.program_id(2)

    @pl.when(qi == 0)
    def _():
        kcat_ref[:, :MLA_NOPE_DIM] = kn_ref[...]
        kcat_ref[:, MLA_NOPE_DIM:] = kpe_ref[...]

    def load_k(j):
        return kcat_ref[pl.ds(pl.multiple_of(j * bk, bk), bk), :]

    def load_vt(j):
        return vt_ref[:, pl.ds(pl.multiple_of(j * bk, bk), bk)]

    acc, l = _flash(q_ref[...], load_k, load_vt, qi, bq=bq, bk=bk, dv=MLA_V_DIM, unroll=unroll)
    o_ref[...] = (acc * (1.0 / l)).T.astype(o_ref.dtype)


def _mla_attention(q, k_nope, kpe, vt, batch, seq):
    m = q.shape[0]
    bq = _pick(seq, 512, CHUNK)
    bk = _pick(bq, 256, CHUNK)
    nq = seq // bq
    qw = 2 * LANES
    unroll = _kv_group(seq, bq, bk)
    return pl.pallas_call(
        functools.partial(_mla_attn_kernel, bq=bq, bk=bk, unroll=unroll),
        grid=(batch, MLA_HEADS, nq),
        in_specs=[
            pl.BlockSpec((bq, qw), lambda b, h, i: (b * nq + i, h)),
            pl.BlockSpec((seq, MLA_NOPE_DIM), lambda b, h, i: (b, h)),
            pl.BlockSpec((seq, LANES), lambda b, h, i: (b, 0)),
            pl.BlockSpec((MLA_V_DIM, seq), lambda b, h, i: (h, b)),
        ],
        out_specs=pl.BlockSpec((bq, MLA_V_DIM), lambda b, h, i: (b * nq + i, h)),
        out_shape=jax.ShapeDtypeStruct((m, MLA_WIDTH), MXU_DTYPE),
        scratch_shapes=[pltpu.VMEM((seq, qw), MXU_DTYPE)],
        compiler_params=_cparams("parallel", "parallel", "arbitrary"),
        name="mla_attention",
    )(q, k_nope, kpe, vt)


def _merge_kernel(od_ref, om_ref, wd_ref, wm_ref, ga_ref, gb_ref, o_ref):
    a = jnp.dot(od_ref[...], wd_ref[...], preferred_element_type=F32)
    b = jnp.dot(om_ref[...], wm_ref[...], preferred_element_type=F32)
    o_ref[...] = (ga_ref[...].astype(F32) * a + gb_ref[...].astype(F32) * b).astype(o_ref.dtype)


def _merge(o_diff, o_mla, w_d, w_m, gates):
    m = o_diff.shape[0]
    n = w_d.shape[1]
    bm = _pick(m, 1024, BF16_SUBLANES)
    bn = _pick(n, 512)
    nb = n // bn
    return pl.pallas_call(
        _merge_kernel,
        grid=(m // bm, nb),
        in_specs=[
            pl.BlockSpec((bm, DIFF_WIDTH), lambda i, j: (i, 0)),
            pl.BlockSpec((bm, MLA_WIDTH), lambda i, j: (i, 0)),
            pl.BlockSpec((DIFF_WIDTH, bn), lambda i, j: (0, j)),
            pl.BlockSpec((MLA_WIDTH, bn), lambda i, j: (0, j)),
            pl.BlockSpec((bm, bn), lambda i, j: (i, j)),
            pl.BlockSpec((bm, bn), lambda i, j: (i, j + nb)),
        ],
        out_specs=pl.BlockSpec((bm, bn), lambda i, j: (i, j)),
        out_shape=jax.ShapeDtypeStruct((m, n), MXU_DTYPE),
        compiler_params=_cparams("parallel", "arbitrary"),
        name="branch_merge",
    )(o_diff, o_mla, w_d, w_m, gates, gates)


def _residual_norm(y, x_ref, gp_ref, gn_ref, xo_ref, ho_ref):
    xn = x_ref[...] + _rms(y, gp_ref[...])
    xo_ref[...] = xn
    ho_ref[...] = _rms(xn, gn_ref[...]).astype(ho_ref.dtype)


def _out_proj_kernel(lhs_ref, w_ref, x_ref, gp_ref, gn_ref, xo_ref, ho_ref):
    y = jnp.dot(lhs_ref[...], w_ref[...], preferred_element_type=F32)
    _residual_norm(y, x_ref, gp_ref, gn_ref, xo_ref, ho_ref)


def _out_proj(lhs, w, x, g_post, g_next):
    m, kdim = lhs.shape
    n = w.shape[1]
    bm = _pick(m, 512, BF16_SUBLANES)
    row = lambda i: (i, 0)
    const = lambda i: (0, 0)
    return pl.pallas_call(
        _out_proj_kernel,
        grid=(m // bm,),
        in_specs=[
            pl.BlockSpec((bm, kdim), row),
            pl.BlockSpec((kdim, n), const),
            pl.BlockSpec((bm, n), row),
            pl.BlockSpec((1, n), const),
            pl.BlockSpec((1, n), const),
        ],
        out_specs=[pl.BlockSpec((bm, n), row), pl.BlockSpec((bm, n), row)],
        out_shape=[jax.ShapeDtypeStruct((m, n), F32), jax.ShapeDtypeStruct((m, n), MXU_DTYPE)],
        compiler_params=_cparams("parallel"),
        name="out_proj",
    )(lhs, w, x, g_post[None, :], g_next[None, :])


def _mem_attn_kernel(q_ref, kv_ref, o_ref, *, dh):
    width = MEM_HEADS * dh
    outs = []
    for h in range(MEM_HEADS):
        q = q_ref[:, h * dh:(h + 1) * dh]
        k = kv_ref[:, h * dh:(h + 1) * dh]
        v = kv_ref[:, width + h * dh:width + (h + 1) * dh]
        s = lax.dot_general(q, k, _NT, preferred_element_type=F32)
        p = jnp.exp(s - s.max(axis=1, keepdims=True))
        l = p.sum(axis=1, keepdims=True)
        outs.append(jnp.dot(p.astype(MXU_DTYPE), v, preferred_element_type=F32) / l)
    o_ref[...] = jnp.concatenate(outs, axis=1).astype(o_ref.dtype)


def _mem_attention(q, kv, seq, mem_tokens):
    m, d = q.shape
    bq = _pick(seq, 512, BF16_SUBLANES)
    per_batch = seq // bq
    return pl.pallas_call(
        functools.partial(_mem_attn_kernel, dh=d // MEM_HEADS),
        grid=(m // bq,),
        in_specs=[pl.BlockSpec((bq, d), lambda i: (i, 0)),
                  pl.BlockSpec((mem_tokens, 2 * d), lambda i: (i // per_batch, 0))],
        out_specs=pl.BlockSpec((bq, d), lambda i: (i, 0)),
        out_shape=jax.ShapeDtypeStruct((m, d), MXU_DTYPE),
        compiler_params=_cparams("parallel"),
        name="mem_attention",
    )(q, kv)


HALO = BF16_SUBLANES


def _ffn_kernel(h_ref, halo_ref, wa_ref, wg_ref, cwa_ref, cwg_ref, cba_ref, cbg_ref, wd_ref, x_ref, gp_ref,
                gn_ref, xo_ref, ho_ref, hs_ref, u_ref, acc_ref, *, bm, blocks_per_seq, nb, sub):
    i = pl.program_id(0)
    j = pl.program_id(1)

    @pl.when(j == 0)
    def _():
        halo = halo_ref[...]
        first = (i % blocks_per_seq) == 0
        hs_ref[:HALO, :] = jnp.where(first, jnp.zeros_like(halo), halo)
        hs_ref[HALO:, :] = h_ref[...]
        acc_ref[...] = jnp.zeros_like(acc_ref)

    hs = hs_ref[...]

    def conv(slot, cols, w_ref, cw_ref, cb_ref):
        u_ref[slot] = jnp.dot(hs, w_ref[:, cols], preferred_element_type=F32)
        cw = cw_ref[:, cols]
        c = cb_ref[:, cols]
        for tap in range(CONV_WIDTH):
            c = c + u_ref[slot, pl.ds(HALO - (CONV_WIDTH - 1) + tap, bm), :] * cw[tap:tap + 1]
        return c

    acts = []
    for t in range(wa_ref.shape[1] // sub):
        cols = slice(t * sub, (t + 1) * sub)
        a = conv(2 * t, cols, wa_ref, cwa_ref, cba_ref)
        g = conv(2 * t + 1, cols, wg_ref, cwg_ref, cbg_ref)
        acts.append((a * jax.nn.sigmoid(a) * g).astype(MXU_DTYPE))
    acc_ref[...] += jnp.dot(jnp.concatenate(acts, axis=1), wd_ref[...], preferred_element_type=F32)

    @pl.when(j == nb - 1)
    def _():
        _residual_norm(acc_ref[...], x_ref, gp_ref, gn_ref, xo_ref, ho_ref)


def _ffn(h, w_up, conv_w, conv_b, w_down, x, g_post, g_next, seq):
    m, d = h.shape
    ff = w_up.shape[1] // 2
    bm = _pick(seq, 512, HALO)
    bn = _pick(ff, 512)
    sub = _pick(bn, MXU_COLS)
    nb = ff // bn
    halo_blocks = bm // HALO
    row = lambda i, j: (i, 0)
    const = lambda i, j: (0, 0)
    up_a = lambda i, j: (0, j)
    up_g = lambda i, j: (0, j + nb)
    return pl.pallas_call(
        functools.partial(_ffn_kernel, bm=bm, blocks_per_seq=seq // bm, nb=nb, sub=sub),
        grid=(m // bm, nb),
        in_specs=[
            pl.BlockSpec((bm, d), row),
            pl.BlockSpec((HALO, d), lambda i, j: (jnp.maximum(i * halo_blocks - 1, 0), 0)),
            pl.BlockSpec((d, bn), up_a),
            pl.BlockSpec((d, bn), up_g),
            pl.BlockSpec((CONV_WIDTH, bn), up_a),
            pl.BlockSpec((CONV_WIDTH, bn), up_g),
            pl.BlockSpec((1, bn), up_a),
            pl.BlockSpec((1, bn), up_g),
            pl.BlockSpec((bn, d), lambda i, j: (j, 0)),
            pl.BlockSpec((bm, d), row),
            pl.BlockSpec((1, d), const),
            pl.BlockSpec((1, d), const),
        ],
        out_specs=[pl.BlockSpec((bm, d), row), pl.BlockSpec((bm, d), row)],
        out_shape=[jax.ShapeDtypeStruct((m, d), F32), jax.ShapeDtypeStruct((m, d), MXU_DTYPE)],
        scratch_shapes=[pltpu.VMEM((HALO + bm, d), MXU_DTYPE),
                        pltpu.VMEM((2 * (bn // sub), HALO + bm, sub), F32),
                        pltpu.VMEM((bm, d), F32)],
        compiler_params=_cparams("parallel", "arbitrary"),
        name="ffn",
    )(h, h, w_up, w_up, conv_w, conv_w, conv_b[None, :], conv_b[None, :], w_down, x,
      g_post[None, :], g_next[None, :])


def kernel(x, mem, positions, g_pre_mix, w_in, b_gate, lam_q1, lam_k1, lam_q2, lam_k2, g_diff_sub, g_cq,
           w_uq, g_ckv, w_ukv, w_br_diff, w_br_mla, w_mix_out, g_post_mix, g_pre_x, g_mem, w_q_x, w_kv_x,
           w_o_x, g_post_x, g_pre_ffn, w_up, conv_w, conv_b, w_down, g_post_ffn):
    batch, seq, d = x.shape
    mem_tokens = mem.shape[1]
    depth = w_in.shape[0]
    m = batch * seq
    cast = lambda a: a.astype(MXU_DTYPE)

    qw = DIFF_HEADS * 2 * DIFF_QK_DIM
    o_v = 2 * qw
    o_cq = o_v + DIFF_WIDTH
    o_ckv = o_cq + MLA_Q_RANK
    o_kr = o_ckv + MLA_KV_RANK
    o_gt = o_kr + MLA_ROPE_DIM
    assert w_in.shape[2] == o_gt + 2 * d
    assert MLA_Q_RANK % MLA_KV_RANK == 0
    qhead = MLA_NOPE_DIM + MLA_ROPE_DIM
    kvhead = MLA_NOPE_DIM + MLA_V_DIM

    cos, sin = _rope_tables(positions)
    xf = x.reshape(m, d)
    memf = mem.reshape(batch * mem_tokens, d)
    h = _rmsnorm(xf, g_pre_mix[0])

    for l in range(depth):
        lam_init = 0.8 - 0.6 * math.exp(-0.3 * l)
        wl = w_in[l]
        w_qk = cast(jnp.concatenate([wl[:, :qw] * (DIFF_QK_DIM ** -0.5), wl[:, qw:o_v]], axis=1))
        w_vt = cast(wl[:, o_v:o_cq].T)
        w_lat = cast(wl[:, o_cq:o_kr])
        w_kpe = cast(jnp.pad(wl[:, o_kr:o_gt], ((0, 0), (0, LANES - MLA_ROPE_DIM))))
        w_gate = cast(wl[:, o_gt:])
        w_uq_p = cast(jnp.pad(w_uq[l].reshape(MLA_Q_RANK, MLA_HEADS, qhead),
                              ((0, 0), (0, 0), (0, 2 * LANES - qhead))).reshape(MLA_Q_RANK, -1))
        w_ukv_h = w_ukv[l].reshape(MLA_KV_RANK, MLA_HEADS, kvhead)
        w_uk = cast(w_ukv_h[:, :, :MLA_NOPE_DIM].reshape(MLA_KV_RANK, -1))
        w_uvt = cast(w_ukv_h[:, :, MLA_NOPE_DIM:].reshape(MLA_KV_RANK, -1).T)

        bn_qk = _pick(2 * qw, 512)
        qk = _proj(h, w_qk, functools.partial(_ep_rope, q_tiles=qw // bn_qk), name="proj_qk",
                   extras=(cos, sin), extra_specs=_rope_specs)
        vt = _proj_t(h, w_vt, name="proj_vt")
        lat = _proj(h, w_lat, _ep_plain, name="proj_latent", bn_pref=MLA_Q_RANK + MLA_KV_RANK)
        kpe = _proj(h, w_kpe, _ep_rope, name="proj_kpe", extras=(cos, sin), extra_specs=_rope_specs)
        gates = _proj(h, w_gate, _ep_sigmoid_bias, name="proj_gates", extras=(b_gate[l][None, :],),
                      extra_specs=_bias_specs)
        q_mla = _latent_q(lat, 0, g_cq[l], w_uq_p, cos, sin, qhead ** -0.5 * LOG2E)
        k_mla, vt_mla = _latent_kv(lat, MLA_Q_RANK // MLA_KV_RANK, g_ckv[l], w_uk, w_uvt)
        lam_vecs = jnp.stack([lam_q1[l], lam_k1[l], lam_q2[l], lam_k2[l]])
        o_diff = _diff_attention(qk, vt, lam_vecs, g_diff_sub[l], lam_init, batch, seq)
        o_mla = _mla_attention(q_mla, k_mla, kpe, vt_mla, batch, seq)
        merged = _merge(o_diff, o_mla, cast(w_br_diff[l]), cast(w_br_mla[l]), gates)
        xf, h = _out_proj(merged, cast(w_mix_out[l]), xf, g_post_mix[l], g_pre_x[l])

        q_x = _proj(h, cast(w_q_x[l]), functools.partial(_ep_scale, scale=(d // MEM_HEADS) ** -0.5),
                    name="proj_qx")
        kv_x = _proj(_rmsnorm(memf, g_mem[l]), cast(w_kv_x[l]), _ep_plain, name="proj_kvx")
        o_x = _mem_attention(q_x, kv_x, seq, mem_tokens)
        xf, h = _out_proj(o_x, cast(w_o_x[l]), xf, g_post_x[l], g_pre_ffn[l])

        g_next = g_pre_mix[l + 1] if l + 1 < depth else g_pre_mix[0]
        xf, h = _ffn(h, cast(w_up[l]), conv_w[l], conv_b[l], cast(w_down[l]), xf, g_post_ffn[l], g_next, seq)

    return xf.reshape(batch, seq, d)
```

```python
import functools
import math

import jax
import jax.numpy as jnp
from jax import lax
from jax.experimental import pallas as pl
from jax.experimental.pallas import tpu as pltpu

CHUNK = 64
ROPE_THETA = 10000.0
EPS = 1e-6
DIFF_HEADS = 8
DIFF_QK_DIM = 64
DIFF_V_DIM = 128
MLA_HEADS = 8
MLA_Q_RANK = 512
MLA_KV_RANK = 256
MLA_NOPE_DIM = 128
MLA_ROPE_DIM = 64
MLA_V_DIM = 128
MEM_HEADS = 4
CONV_WIDTH = 3
DIFF_WIDTH = DIFF_HEADS * DIFF_V_DIM
MLA_WIDTH = MLA_HEADS * MLA_V_DIM

LANES = 128
BF16_SUBLANES = 16
MXU_COLS = 256
VMEM_LIMIT_BYTES = 56 * 2**20

MXU_DTYPE = jnp.bfloat16
F32 = jnp.float32
LOG2E = math.log2(math.e)


def _pick(n, pref, mult=LANES):
    if n <= pref:
        return n
    best = None
    for d in range(mult, pref + 1, mult):
        if n % d == 0:
            best = d
    assert best is not None, (n, pref, mult)
    return best


def _cparams(*sem, flags=None):
    return pltpu.CompilerParams(dimension_semantics=sem, vmem_limit_bytes=VMEM_LIMIT_BYTES, flags=flags)


def _rms(x, g):
    return x * lax.rsqrt(jnp.mean(x * x, axis=-1, keepdims=True) + EPS) * g


def _rope128(t, cos, sin):
    return t * cos + pltpu.roll(t, LANES // 2, 1) * sin


def _pair_halves(w):
    q = LANES // 4
    shape = w.shape
    w = w.reshape(*shape[:-1], shape[-1] // LANES, 2, 2, q)
    return jnp.swapaxes(w, -3, -2).reshape(shape)


_NT = (((1,), (1,)), ((), ()))


def _tables_kernel(pos_ref, inv_ref, sgn_ref, cos_ref, sin_ref):
    ang = pos_ref[...].astype(F32) * inv_ref[...]
    cos_ref[...] = jnp.cos(ang)
    sin_ref[...] = jnp.sin(ang) * sgn_ref[...]


def _rope_tables(positions):
    m = positions.size
    d = DIFF_QK_DIM
    inv = ROPE_THETA ** (-jnp.arange(0, d, 2, dtype=F32) / d)
    inv128 = jnp.tile(inv, LANES // (d // 2))[None, :]
    sgn128 = jnp.concatenate([-jnp.ones((LANES // 2,), F32), jnp.ones((LANES // 2,), F32)])[None, :]
    bm = _pick(m, 2048, 8)
    row = pl.BlockSpec((bm, LANES), lambda i: (i, 0))
    const = pl.BlockSpec((1, LANES), lambda i: (0, 0))
    return pl.pallas_call(
        _tables_kernel,
        grid=(m // bm,),
        in_specs=[pl.BlockSpec((bm, 1), lambda i: (i, 0)), const, const],
        out_specs=[row, row],
        out_shape=[jax.ShapeDtypeStruct((m, LANES), F32)] * 2,
        compiler_params=_cparams("parallel"),
        name="rope_tables",
    )(positions.reshape(m, 1), inv128, sgn128)


def _rmsnorm_kernel(x_ref, g_ref, o_ref):
    o_ref[...] = _rms(x_ref[...], g_ref[...]).astype(o_ref.dtype)


def _rmsnorm(x, g):
    m, d = x.shape
    bm = _pick(m, 512, 8)
    return pl.pallas_call(
        _rmsnorm_kernel,
        grid=(m // bm,),
        in_specs=[pl.BlockSpec((bm, d), lambda i: (i, 0)), pl.BlockSpec((1, d), lambda i: (0, 0))],
        out_specs=pl.BlockSpec((bm, d), lambda i: (i, 0)),
        out_shape=jax.ShapeDtypeStruct((m, d), MXU_DTYPE),
        compiler_params=_cparams("parallel"),
        name="rmsnorm",
    )(x, g[None, :])


def _proj_kernel(h_ref, w_ref, *rest, epilogue, sub):
    *extra, o_ref = rest
    h = h_ref[...]
    for t in range(w_ref.shape[1] // sub):
        cols = slice(t * sub, (t + 1) * sub)
        acc = jnp.dot(h, w_ref[:, cols], preferred_element_type=F32)
        o_ref[:, cols] = epilogue(acc, cols, *extra).astype(o_ref.dtype)


def _ep_plain(acc, cols):
    return acc


def _ep_scale(acc, cols, *, scale):
    return acc * scale


def _ep_rope(acc, cols, cos_ref, sin_ref, *, q_tiles=0):
    cos, sin = cos_ref[...], sin_ref[...]
    n = acc.shape[1] // LANES
    out = jnp.concatenate(
        [_rope128(acc[:, c * LANES:(c + 1) * LANES], cos, sin) for c in range(n)], axis=1)
    if q_tiles:
        out = out * jnp.where(pl.program_id(1) < q_tiles, LOG2E, 1.0)
    return out


def _ep_sigmoid_bias(acc, cols, b_ref):
    return jax.nn.sigmoid(acc + b_ref[:, cols])


def _proj(h, w, epilogue, *, name, extras=(), extra_specs=None, bm_pref=1024, bn_pref=512):
    m, k = h.shape
    n = w.shape[1]
    bm = _pick(m, bm_pref, BF16_SUBLANES)
    bn = _pick(n, bn_pref)
    specs = [pl.BlockSpec((bm, k), lambda i, j: (i, 0)), pl.BlockSpec((k, bn), lambda i, j: (0, j))]
    specs += list(extra_specs(bm, bn)) if extra_specs else []
    return pl.pallas_call(
        functools.partial(_proj_kernel, epilogue=epilogue, sub=_pick(bn, MXU_COLS)),
        grid=(m // bm, n // bn),
        in_specs=specs,
        out_specs=pl.BlockSpec((bm, bn), lambda i, j: (i, j)),
        out_shape=jax.ShapeDtypeStruct((m, n), MXU_DTYPE),
        compiler_params=_cparams("parallel", "arbitrary"),
        name=name,
    )(h, w, *extras)


def _rope_specs(bm, bn):
    tab = pl.BlockSpec((bm, LANES), lambda i, j: (i, 0))
    return [tab, tab]


def _bias_specs(bm, bn):
    return [pl.BlockSpec((1, bn), lambda i, j: (0, j))]


def _proj_t_kernel(h_ref, wt_ref, o_ref):
    o_ref[...] = lax.dot_general(wt_ref[...], h_ref[...], _NT,
                                 preferred_element_type=F32).astype(o_ref.dtype)


def _proj_t(h, wt, *, name):
    m, k = h.shape
    n = wt.shape[0]
    bm = _pick(m, 1024)
    bn = _pick(n, 512, BF16_SUBLANES)
    return pl.pallas_call(
        _proj_t_kernel,
        grid=(m // bm, n // bn),
        in_specs=[pl.BlockSpec((bm, k), lambda i, j: (i, 0)), pl.BlockSpec((bn, k), lambda i, j: (j, 0))],
        out_specs=pl.BlockSpec((bn, bm), lambda i, j: (j, i)),
        out_shape=jax.ShapeDtypeStruct((n, m), MXU_DTYPE),
        compiler_params=_cparams("parallel", "arbitrary"),
        name=name,
    )(h, wt)


def _latent_q_kernel(c_ref, g_ref, w_ref, cos_ref, sin_ref, o_ref, *, scale):
    cn = _rms(c_ref[...].astype(F32), g_ref[...]).astype(MXU_DTYPE)
    acc = jnp.dot(cn, w_ref[...], preferred_element_type=F32)
    cos, sin = cos_ref[...], sin_ref[...]
    cols = []
    for c in range(acc.shape[1] // LANES):
        t = acc[:, c * LANES:(c + 1) * LANES]
        cols.append(_rope128(t, cos, sin) if c % 2 else t)
    o_ref[...] = (jnp.concatenate(cols, axis=1) * scale).astype(o_ref.dtype)


def _latent_kv_kernel(c_ref, g_ref, wk_ref, wvt_ref, k_ref, vt_ref):
    cn = _rms(c_ref[...].astype(F32), g_ref[...]).astype(MXU_DTYPE)
    k_ref[...] = jnp.dot(cn, wk_ref[...], preferred_element_type=F32).astype(k_ref.dtype)
    vt_ref[...] = lax.dot_general(wvt_ref[...], cn, _NT, preferred_element_type=F32).astype(vt_ref.dtype)


def _latent_q(src, col_block, g, w, cos, sin, scale):
    m = src.shape[0]
    rank, n = w.shape
    bm = _pick(m, 512, BF16_SUBLANES)
    tab = pl.BlockSpec((bm, LANES), lambda i: (i, 0))
    return pl.pallas_call(
        functools.partial(_latent_q_kernel, scale=scale),
        grid=(m // bm,),
        in_specs=[pl.BlockSpec((bm, rank), lambda i: (i, col_block)),
                  pl.BlockSpec((1, rank), lambda i: (0, 0)),
                  pl.BlockSpec((rank, n), lambda i: (0, 0)), tab, tab],
        out_specs=pl.BlockSpec((bm, n), lambda i: (i, 0)),
        out_shape=jax.ShapeDtypeStruct((m, n), MXU_DTYPE),
        compiler_params=_cparams("parallel"),
        name="mla_q",
    )(src, g[None, :], w, cos, sin)


def _latent_kv(src, col_block, g, wk, wvt):
    m = src.shape[0]
    rank, nk = wk.shape
    nv = wvt.shape[0]
    bm = _pick(m, 512)
    return pl.pallas_call(
        _latent_kv_kernel,
        grid=(m // bm,),
        in_specs=[pl.BlockSpec((bm, rank), lambda i: (i, col_block)),
                  pl.BlockSpec((1, rank), lambda i: (0, 0)),
                  pl.BlockSpec((rank, nk), lambda i: (0, 0)),
                  pl.BlockSpec((nv, rank), lambda i: (0, 0))],
        out_specs=[pl.BlockSpec((bm, nk), lambda i: (i, 0)), pl.BlockSpec((nv, bm), lambda i: (0, i))],
        out_shape=[jax.ShapeDtypeStruct((m, nk), MXU_DTYPE), jax.ShapeDtypeStruct((nv, m), MXU_DTYPE)],
        compiler_params=_cparams("parallel"),
        name="mla_kv",
    )(src, g[None, :], wk, wvt)


def _flash(q_parts, load_k, load_vt, qi, *, bq, bk, dv, unroll):
    diag = bq // bk
    assert unroll % diag == 0
    rel = (lax.broadcasted_iota(jnp.int32, (bk, bq), 0) // CHUNK
           - lax.broadcasted_iota(jnp.int32, (bk, bq), 1) // CHUNK)

    def group(jg, carries, count, masked):
        blocks = [jg * unroll + u for u in range(count)]
        scores = [[lax.dot_general(load_k(j, part), q, _NT, preferred_element_type=F32)
                   for part, q in enumerate(q_parts)] for j in blocks]
        carries = list(carries)
        for u, j in enumerate(blocks):
            d = u - (count - masked)
            for part, s in enumerate(scores[u]):
                m, l, acc = carries[part]
                if d >= 0:
                    s = jnp.where(rel <= -d * (bk // CHUNK), s, -jnp.inf)
                m_new = jnp.maximum(m, s.max(axis=0, keepdims=True))
                alpha = jnp.exp2(m - m_new)
                p = jnp.exp2(s - m_new)
                l = alpha * l + p.sum(axis=0, keepdims=True)
                acc = alpha * acc + jnp.dot(load_vt(j, part), p.astype(MXU_DTYPE),
                                            preferred_element_type=F32)
                carries[part] = (m_new, l, acc)
        return tuple(carries)

    init = (jnp.full((1, bq), -jnp.inf, F32), jnp.zeros((1, bq), F32), jnp.zeros((dv, bq), F32))
    carries = tuple(init for _ in q_parts)
    n_blocks = (qi + 1) * diag
    n_groups = (n_blocks + unroll - 1) // unroll
    carries = lax.fori_loop(0, n_groups - 1, lambda jg, c: group(jg, c, unroll, 0), carries)
    last = (n_blocks - (n_groups - 1) * unroll) // diag - 1
    tails = [functools.partial(group, count=(t + 1) * diag, masked=diag) for t in range(unroll // diag)]
    carries = lax.switch(last, [lambda c, f=f: f(n_groups - 1, c) for f in tails], carries)
    return [(acc, l) for _, l, acc in carries]


KV_GROUP = 4


def _kv_group(seq, bq, bk):
    group = max(bq // bk, min(KV_GROUP, seq // bk))
    assert (seq // bk) % group == 0 and group % (bq // bk) == 0
    return group


def _diff_attn_kernel(lam_ref, q_ref, k_ref, vt_ref, g_ref, o_ref, *, bq, bk, lam_init, unroll):
    qi = pl.program_id(2)
    lam_v = lam_ref[...]
    lam = (jnp.exp(jnp.sum(lam_v[0:1] * lam_v[1:2], axis=1, keepdims=True))
           - jnp.exp(jnp.sum(lam_v[2:3] * lam_v[3:4], axis=1, keepdims=True)) + lam_init)
    q = q_ref[...]
    is_q1 = (lax.broadcasted_iota(jnp.int32, q.shape, 1) & (DIFF_QK_DIM // 2)) == 0
    zero = jnp.zeros_like(q)
    q_parts = [jnp.where(is_q1, q, zero), jnp.where(is_q1, zero, q)]

    def load_k(j, part):
        return k_ref[pl.ds(pl.multiple_of(j * bk, bk), bk), :]

    def load_vt(j, part):
        return vt_ref[:, pl.ds(pl.multiple_of(j * bk, bk), bk)]

    (acc1, l1), (acc2, l2) = _flash(q_parts, load_k, load_vt, qi, bq=bq, bk=bk, dv=DIFF_V_DIM,
                                    unroll=unroll)
    o = acc1 * (1.0 / l1) - lam * (acc2 * (1.0 / l2))
    o = o * lax.rsqrt(jnp.mean(o * o, axis=0, keepdims=True) + EPS) * g_ref[...]
    o_ref[...] = (o * (1.0 - lam_init)).T.astype(o_ref.dtype)


def _diff_attention(qk, vt, lam_vecs, g_sub, lam_init, batch, seq):
    m = qk.shape[0]
    bq = _pick(seq, 512, CHUNK)
    bk = _pick(bq, 256, CHUNK)
    nq = seq // bq
    unroll = _kv_group(seq, bq, bk)
    return pl.pallas_call(
        functools.partial(_diff_attn_kernel, bq=bq, bk=bk, lam_init=lam_init, unroll=unroll),
        grid=(batch, DIFF_HEADS, nq),
        in_specs=[
            pl.BlockSpec((4, DIFF_QK_DIM), lambda b, h, i: (0, 0)),
            pl.BlockSpec((bq, LANES), lambda b, h, i: (b * nq + i, h)),
            pl.BlockSpec((seq, LANES), lambda b, h, i: (b, DIFF_HEADS + h)),
            pl.BlockSpec((DIFF_V_DIM, seq), lambda b, h, i: (h, b)),
            pl.BlockSpec((DIFF_V_DIM, 1), lambda b, h, i: (0, 0)),
        ],
        out_specs=pl.BlockSpec((bq, DIFF_V_DIM), lambda b, h, i: (b * nq + i, h)),
        out_shape=jax.ShapeDtypeStruct((m, DIFF_WIDTH), MXU_DTYPE),
        compiler_params=_cparams("parallel", "parallel", "arbitrary"),
        name="diff_attention",
    )(lam_vecs, qk, qk, vt, g_sub[:, None])


MLA_HEADS_PER_STEP = 2


def _mla_attn_kernel(q_ref, kn_ref, kpe_ref, vt_ref, o_ref, kcat_ref, *, bq, bk, unroll):
    qi = pl.program_id(2)
    qw = 2 * LANES

    @pl.when(qi == 0)
    def _():
        for hh in range(MLA_HEADS_PER_STEP):
            kcat_ref[hh, :, :MLA_NOPE_DIM] = kn_ref[:, hh * MLA_NOPE_DIM:(hh + 1) * MLA_NOPE_DIM]
            kcat_ref[hh, :, MLA_NOPE_DIM:] = kpe_ref[...]

    def load_k(j, part):
        return kcat_ref[part, pl.ds(pl.multiple_of(j * bk, bk), bk), :]

    def load_vt(j, part):
        return vt_ref[part * MLA_V_DIM:(part + 1) * MLA_V_DIM, pl.ds(pl.multiple_of(j * bk, bk), bk)]

    q_parts = [q_ref[:, hh * qw:(hh + 1) * qw] for hh in range(MLA_HEADS_PER_STEP)]
    outs = _flash(q_parts, load_k, load_vt, qi, bq=bq, bk=bk, dv=MLA_V_DIM, unroll=unroll)
    for hh, (acc, l) in enumerate(outs):
        o_ref[:, hh * MLA_V_DIM:(hh + 1) * MLA_V_DIM] = (acc * (1.0 / l)).T.astype(o_ref.dtype)


def _mla_attention(q, k_nope, kpe, vt, batch, seq):
    m = q.shape[0]
    bq = _pick(seq, 512, CHUNK)
    bk = _pick(bq, 256, CHUNK)
    nq = seq // bq
    hs = MLA_HEADS_PER_STEP
    qw = 2 * LANES
    unroll = _kv_group(seq, bq, bk)
    return pl.pallas_call(
        functools.partial(_mla_attn_kernel, bq=bq, bk=bk, unroll=unroll),
        grid=(batch, MLA_HEADS // hs, nq),
        in_specs=[
            pl.BlockSpec((bq, hs * qw), lambda b, h, i: (b * nq + i, h)),
            pl.BlockSpec((seq, hs * MLA_NOPE_DIM), lambda b, h, i: (b, h)),
            pl.BlockSpec((seq, LANES), lambda b, h, i: (b, 0)),
            pl.BlockSpec((hs * MLA_V_DIM, seq), lambda b, h, i: (h, b)),
        ],
        out_specs=pl.BlockSpec((bq, hs * MLA_V_DIM), lambda b, h, i: (b * nq + i, h)),
        out_shape=jax.ShapeDtypeStruct((m, MLA_WIDTH), MXU_DTYPE),
        scratch_shapes=[pltpu.VMEM((hs, seq, qw), MXU_DTYPE)],
        compiler_params=_cparams("parallel", "parallel", "arbitrary"),
        name="mla_attention",
    )(q, k_nope, kpe, vt)


def _merge_kernel(od_ref, om_ref, wd_ref, wm_ref, ga_ref, gb_ref, o_ref):
    a = jnp.dot(od_ref[...], wd_ref[...], preferred_element_type=F32)
    b = jnp.dot(om_ref[...], wm_ref[...], preferred_element_type=F32)
    o_ref[...] = (ga_ref[...].astype(F32) * a + gb_ref[...].astype(F32) * b).astype(o_ref.dtype)


def _merge(o_diff, o_mla, w_d, w_m, gates):
    m = o_diff.shape[0]
    n = w_d.shape[1]
    bm = _pick(m, 1024, BF16_SUBLANES)
    bn = _pick(n, 512)
    nb = n // bn
    return pl.pallas_call(
        _merge_kernel,
        grid=(m // bm, nb),
        in_specs=[
            pl.BlockSpec((bm, DIFF_WIDTH), lambda i, j: (i, 0)),
            pl.BlockSpec((bm, MLA_WIDTH), lambda i, j: (i, 0)),
            pl.BlockSpec((DIFF_WIDTH, bn), lambda i, j: (0, j)),
            pl.BlockSpec((MLA_WIDTH, bn), lambda i, j: (0, j)),
            pl.BlockSpec((bm, bn), lambda i, j: (i, j)),
            pl.BlockSpec((bm, bn), lambda i, j: (i, j + nb)),
        ],
        out_specs=pl.BlockSpec((bm, bn), lambda i, j: (i, j)),
        out_shape=jax.ShapeDtypeStruct((m, n), MXU_DTYPE),
        compiler_params=_cparams("parallel", "arbitrary"),
        name="branch_merge",
    )(o_diff, o_mla, w_d, w_m, gates, gates)


def _residual_norm(y, x_ref, gp_ref, gn_ref, xo_ref, ho_ref):
    xn = x_ref[...] + _rms(y, gp_ref[...])
    xo_ref[...] = xn
    ho_ref[...] = _rms(xn, gn_ref[...]).astype(ho_ref.dtype)


def _out_proj_kernel(lhs_ref, w_ref, x_ref, gp_ref, gn_ref, xo_ref, ho_ref):
    y = jnp.dot(lhs_ref[...], w_ref[...], preferred_element_type=F32)
    _residual_norm(y, x_ref, gp_ref, gn_ref, xo_ref, ho_ref)


def _out_proj(lhs, w, x, g_post, g_next):
    m, kdim = lhs.shape
    n = w.shape[1]
    bm = _pick(m, 512, BF16_SUBLANES)
    row = lambda i: (i, 0)
    const = lambda i: (0, 0)
    return pl.pallas_call(
        _out_proj_kernel,
        grid=(m // bm,),
        in_specs=[
            pl.BlockSpec((bm, kdim), row),
            pl.BlockSpec((kdim, n), const),
            pl.BlockSpec((bm, n), row),
            pl.BlockSpec((1, n), const),
            pl.BlockSpec((1, n), const),
        ],
        out_specs=[pl.BlockSpec((bm, n), row), pl.BlockSpec((bm, n), row)],
        out_shape=[jax.ShapeDtypeStruct((m, n), F32), jax.ShapeDtypeStruct((m, n), MXU_DTYPE)],
        compiler_params=_cparams("parallel"),
        name="out_proj",
    )(lhs, w, x, g_post[None, :], g_next[None, :])


def _mem_attn_kernel(q_ref, kv_ref, o_ref, *, dh):
    width = MEM_HEADS * dh
    outs = []
    for h in range(MEM_HEADS):
        q = q_ref[:, h * dh:(h + 1) * dh]
        k = kv_ref[:, h * dh:(h + 1) * dh]
        v = kv_ref[:, width + h * dh:width + (h + 1) * dh]
        s = lax.dot_general(q, k, _NT, preferred_element_type=F32)
        p = jnp.exp(s - s.max(axis=1, keepdims=True))
        l = p.sum(axis=1, keepdims=True)
        outs.append(jnp.dot(p.astype(MXU_DTYPE), v, preferred_element_type=F32) / l)
    o_ref[...] = jnp.concatenate(outs, axis=1).astype(o_ref.dtype)


def _mem_attention(q, kv, seq, mem_tokens):
    m, d = q.shape
    bq = _pick(seq, 512, BF16_SUBLANES)
    per_batch = seq // bq
    return pl.pallas_call(
        functools.partial(_mem_attn_kernel, dh=d // MEM_HEADS),
        grid=(m // bq,),
        in_specs=[pl.BlockSpec((bq, d), lambda i: (i, 0)),
                  pl.BlockSpec((mem_tokens, 2 * d), lambda i: (i // per_batch, 0))],
        out_specs=pl.BlockSpec((bq, d), lambda i: (i, 0)),
        out_shape=jax.ShapeDtypeStruct((m, d), MXU_DTYPE),
        compiler_params=_cparams("parallel"),
        name="mem_attention",
    )(q, kv)


HALO = BF16_SUBLANES


def _ffn_kernel(h_ref, halo_ref, wa_ref, wg_ref, cwa_ref, cwg_ref, cba_ref, cbg_ref, wd_ref, x_ref, gp_ref,
                gn_ref, xo_ref, ho_ref, hs_ref, u_ref, acc_ref, *, bm, blocks_per_seq, nb, sub):
    i = pl.program_id(0)
    j = pl.program_id(1)

    @pl.when(j == 0)
    def _():
        halo = halo_ref[...]
        first = (i % blocks_per_seq) == 0
        hs_ref[:HALO, :] = jnp.where(first, jnp.zeros_like(halo), halo)
        hs_ref[HALO:, :] = h_ref[...]
        acc_ref[...] = jnp.zeros_like(acc_ref)

    hs = hs_ref[...]

    def conv(slot, cols, w_ref, cw_ref, cb_ref):
        u_ref[slot] = jnp.dot(hs, w_ref[:, cols], preferred_element_type=F32)
        cw = cw_ref[:, cols]
        c = cb_ref[:, cols]
        for tap in range(CONV_WIDTH):
            c = c + u_ref[slot, pl.ds(HALO - (CONV_WIDTH - 1) + tap, bm), :] * cw[tap:tap + 1]
        return c

    acts = []
    for t in range(wa_ref.shape[1] // sub):
        cols = slice(t * sub, (t + 1) * sub)
        a = conv(2 * t, cols, wa_ref, cwa_ref, cba_ref)
        g = conv(2 * t + 1, cols, wg_ref, cwg_ref, cbg_ref)
        acts.append((a * jax.nn.sigmoid(a) * g).astype(MXU_DTYPE))
    acc_ref[...] += jnp.dot(jnp.concatenate(acts, axis=1), wd_ref[...], preferred_element_type=F32)

    @pl.when(j == nb - 1)
    def _():
        _residual_norm(acc_ref[...], x_ref, gp_ref, gn_ref, xo_ref, ho_ref)


def _ffn(h, w_up, conv_w, conv_b, w_down, x, g_post, g_next, seq):
    m, d = h.shape
    ff = w_up.shape[1] // 2
    bm = _pick(seq, 512, HALO)
    bn = _pick(ff, 512)
    sub = _pick(bn, MXU_COLS)
    nb = ff // bn
    halo_blocks = bm // HALO
    row = lambda i, j: (i, 0)
    const = lambda i, j: (0, 0)
    up_a = lambda i, j: (0, j)
    up_g = lambda i, j: (0, j + nb)
    return pl.pallas_call(
        functools.partial(_ffn_kernel, bm=bm, blocks_per_seq=seq // bm, nb=nb, sub=sub),
        grid=(m // bm, nb),
        in_specs=[
            pl.BlockSpec((bm, d), row),
            pl.BlockSpec((HALO, d), lambda i, j: (jnp.maximum(i * halo_blocks - 1, 0), 0)),
            pl.BlockSpec((d, bn), up_a),
            pl.BlockSpec((d, bn), up_g),
            pl.BlockSpec((CONV_WIDTH, bn), up_a),
            pl.BlockSpec((CONV_WIDTH, bn), up_g),
            pl.BlockSpec((1, bn), up_a),
            pl.BlockSpec((1, bn), up_g),
            pl.BlockSpec((bn, d), lambda i, j: (j, 0)),
            pl.BlockSpec((bm, d), row),
            pl.BlockSpec((1, d), const),
            pl.BlockSpec((1, d), const),
        ],
        out_specs=[pl.BlockSpec((bm, d), row), pl.BlockSpec((bm, d), row)],
        out_shape=[jax.ShapeDtypeStruct((m, d), F32), jax.ShapeDtypeStruct((m, d), MXU_DTYPE)],
        scratch_shapes=[pltpu.VMEM((HALO + bm, d), MXU_DTYPE),
                        pltpu.VMEM((2 * (bn // sub), HALO + bm, sub), F32),
                        pltpu.VMEM((bm, d), F32)],
        compiler_params=_cparams("parallel", "arbitrary"),
        name="ffn",
    )(h, h, w_up, w_up, conv_w, conv_w, conv_b[None, :], conv_b[None, :], w_down, x,
      g_post[None, :], g_next[None, :])


def kernel(x, mem, positions, g_pre_mix, w_in, b_gate, lam_q1, lam_k1, lam_q2, lam_k2, g_diff_sub, g_cq,
           w_uq, g_ckv, w_ukv, w_br_diff, w_br_mla, w_mix_out, g_post_mix, g_pre_x, g_mem, w_q_x, w_kv_x,
           w_o_x, g_post_x, g_pre_ffn, w_up, conv_w, conv_b, w_down, g_post_ffn):
    batch, seq, d = x.shape
    mem_tokens = mem.shape[1]
    depth = w_in.shape[0]
    m = batch * seq
    cast = lambda a: a.astype(MXU_DTYPE)

    qw = DIFF_HEADS * 2 * DIFF_QK_DIM
    o_v = 2 * qw
    o_cq = o_v + DIFF_WIDTH
    o_ckv = o_cq + MLA_Q_RANK
    o_kr = o_ckv + MLA_KV_RANK
    o_gt = o_kr + MLA_ROPE_DIM
    assert w_in.shape[2] == o_gt + 2 * d
    assert MLA_Q_RANK % MLA_KV_RANK == 0
    qhead = MLA_NOPE_DIM + MLA_ROPE_DIM
    kvhead = MLA_NOPE_DIM + MLA_V_DIM

    cos, sin = _rope_tables(positions)
    xf = x.reshape(m, d)
    memf = mem.reshape(batch * mem_tokens, d)
    h = _rmsnorm(xf, g_pre_mix[0])

    for l in range(depth):
        lam_init = 0.8 - 0.6 * math.exp(-0.3 * l)
        wl = w_in[l]
        w_qk = cast(_pair_halves(
            jnp.concatenate([wl[:, :qw] * (DIFF_QK_DIM ** -0.5), wl[:, qw:o_v]], axis=1)))
        w_vt = cast(wl[:, o_v:o_cq].T)
        w_lat = cast(wl[:, o_cq:o_kr])
        w_kpe = cast(_pair_halves(jnp.pad(wl[:, o_kr:o_gt], ((0, 0), (0, LANES - MLA_ROPE_DIM)))))
        w_gate = cast(wl[:, o_gt:])
        w_uq_h = jnp.pad(w_uq[l].reshape(MLA_Q_RANK, MLA_HEADS, qhead),
                         ((0, 0), (0, 0), (0, 2 * LANES - qhead)))
        w_uq_p = cast(jnp.concatenate([w_uq_h[:, :, :MLA_NOPE_DIM], _pair_halves(w_uq_h[:, :, MLA_NOPE_DIM:])],
                                      axis=2).reshape(MLA_Q_RANK, -1))
        w_ukv_h = w_ukv[l].reshape(MLA_KV_RANK, MLA_HEADS, kvhead)
        w_uk = cast(w_ukv_h[:, :, :MLA_NOPE_DIM].reshape(MLA_KV_RANK, -1))
        w_uvt = cast(w_ukv_h[:, :, MLA_NOPE_DIM:].reshape(MLA_KV_RANK, -1).T)

        bn_qk = _pick(2 * qw, 512)
        qk = _proj(h, w_qk, functools.partial(_ep_rope, q_tiles=qw // bn_qk), name="proj_qk",
                   extras=(cos, sin), extra_specs=_rope_specs)
        vt = _proj_t(h, w_vt, name="proj_vt")
        lat = _proj(h, w_lat, _ep_plain, name="proj_latent", bn_pref=MLA_Q_RANK + MLA_KV_RANK)
        kpe = _proj(h, w_kpe, _ep_rope, name="proj_kpe", extras=(cos, sin), extra_specs=_rope_specs)
        gates = _proj(h, w_gate, _ep_sigmoid_bias, name="proj_gates", extras=(b_gate[l][None, :],),
                      extra_specs=_bias_specs)
        q_mla = _latent_q(lat, 0, g_cq[l], w_uq_p, cos, sin, qhead ** -0.5 * LOG2E)
        k_mla, vt_mla = _latent_kv(lat, MLA_Q_RANK // MLA_KV_RANK, g_ckv[l], w_uk, w_uvt)
        lam_vecs = jnp.stack([lam_q1[l], lam_k1[l], lam_q2[l], lam_k2[l]])
        o_diff = _diff_attention(qk, vt, lam_vecs, g_diff_sub[l], lam_init, batch, seq)
        o_mla = _mla_attention(q_mla, k_mla, kpe, vt_mla, batch, seq)
        merged = _merge(o_diff, o_mla, cast(w_br_diff[l]), cast(w_br_mla[l]), gates)
        xf, h = _out_proj(merged, cast(w_mix_out[l]), xf, g_post_mix[l], g_pre_x[l])

        q_x = _proj(h, cast(w_q_x[l]), functools.partial(_ep_scale, scale=(d // MEM_HEADS) ** -0.5),
                    name="proj_qx")
        kv_x = _proj(_rmsnorm(memf, g_mem[l]), cast(w_kv_x[l]), _ep_plain, name="proj_kvx")
        o_x = _mem_attention(q_x, kv_x, seq, mem_tokens)
        xf, h = _out_proj(o_x, cast(w_o_x[l]), xf, g_post_x[l], g_pre_ffn[l])

        g_next = g_pre_mix[l + 1] if l + 1 < depth else g_pre_mix[0]
        xf, h = _ffn(h, cast(w_up[l]), conv_w[l], conv_b[l], cast(w_down[l]), xf, g_post_ffn[l], g_next, seq)

    return xf.reshape(batch, seq, d)
```

```python
import functools
import math

import jax
import jax.numpy as jnp
from jax import lax
from jax.experimental import pallas as pl
from jax.experimental.pallas import tpu as pltpu

CHUNK = 64
ROPE_THETA = 10000.0
EPS = 1e-6
DIFF_HEADS = 8
DIFF_QK_DIM = 64
DIFF_V_DIM = 128
MLA_HEADS = 8
MLA_Q_RANK = 512
MLA_KV_RANK = 256
MLA_NOPE_DIM = 128
MLA_ROPE_DIM = 64
MLA_V_DIM = 128
MEM_HEADS = 4
CONV_WIDTH = 3
DIFF_WIDTH = DIFF_HEADS * DIFF_V_DIM
MLA_WIDTH = MLA_HEADS * MLA_V_DIM

LANES = 128
BF16_SUBLANES = 16
MXU_COLS = 256
VMEM_LIMIT_BYTES = 56 * 2**20

MXU_DTYPE = jnp.bfloat16
F32 = jnp.float32
LOG2E = math.log2(math.e)


def _pick(n, pref, mult=LANES):
    if n <= pref:
        return n
    best = None
    for d in range(mult, pref + 1, mult):
        if n % d == 0:
            best = d
    assert best is not None, (n, pref, mult)
    return best


def _cparams(*sem, flags=None):
    return pltpu.CompilerParams(dimension_semantics=sem, vmem_limit_bytes=VMEM_LIMIT_BYTES, flags=flags)


def _rms(x, g):
    return x * lax.rsqrt(jnp.mean(x * x, axis=-1, keepdims=True) + EPS) * g


def _rope128(t, cos, sin):
    return t * cos + pltpu.roll(t, LANES // 2, 1) * sin


def _pair_halves(w):
    q = LANES // 4
    shape = w.shape
    w = w.reshape(*shape[:-1], shape[-1] // LANES, 2, 2, q)
    return jnp.swapaxes(w, -3, -2).reshape(shape)


_NT = (((1,), (1,)), ((), ()))


def _tables_kernel(pos_ref, inv_ref, sgn_ref, cos_ref, sin_ref):
    ang = pos_ref[...].astype(F32) * inv_ref[...]
    cos_ref[...] = jnp.cos(ang)
    sin_ref[...] = jnp.sin(ang) * sgn_ref[...]


def _rope_tables(positions):
    m = positions.size
    d = DIFF_QK_DIM
    inv = ROPE_THETA ** (-jnp.arange(0, d, 2, dtype=F32) / d)
    inv128 = jnp.tile(inv, LANES // (d // 2))[None, :]
    sgn128 = jnp.concatenate([-jnp.ones((LANES // 2,), F32), jnp.ones((LANES // 2,), F32)])[None, :]
    bm = _pick(m, 2048, 8)
    row = pl.BlockSpec((bm, LANES), lambda i: (i, 0))
    const = pl.BlockSpec((1, LANES), lambda i: (0, 0))
    return pl.pallas_call(
        _tables_kernel,
        grid=(m // bm,),
        in_specs=[pl.BlockSpec((bm, 1), lambda i: (i, 0)), const, const],
        out_specs=[row, row],
        out_shape=[jax.ShapeDtypeStruct((m, LANES), F32)] * 2,
        compiler_params=_cparams("parallel"),
        name="rope_tables",
    )(positions.reshape(m, 1), inv128, sgn128)


def _rmsnorm_kernel(x_ref, g_ref, o_ref):
    o_ref[...] = _rms(x_ref[...], g_ref[...]).astype(o_ref.dtype)


def _rmsnorm(x, g):
    m, d = x.shape
    bm = _pick(m, 512, 8)
    return pl.pallas_call(
        _rmsnorm_kernel,
        grid=(m // bm,),
        in_specs=[pl.BlockSpec((bm, d), lambda i: (i, 0)), pl.BlockSpec((1, d), lambda i: (0, 0))],
        out_specs=pl.BlockSpec((bm, d), lambda i: (i, 0)),
        out_shape=jax.ShapeDtypeStruct((m, d), MXU_DTYPE),
        compiler_params=_cparams("parallel"),
        name="rmsnorm",
    )(x, g[None, :])


PROJ_ROW_TILE = 256


def _proj_kernel(h_ref, w_ref, *rest, epilogue, sub, rows):
    *extra, o_ref = rest
    for r in range(h_ref.shape[0] // rows):
        rs = pl.ds(r * rows, rows)
        h = h_ref[rs, :]
        for t in range(w_ref.shape[1] // sub):
            cols = slice(t * sub, (t + 1) * sub)
            acc = jnp.dot(h, w_ref[:, cols], preferred_element_type=F32)
            o_ref[rs, cols] = epilogue(acc, rs, cols, *extra).astype(o_ref.dtype)


def _ep_plain(acc, rs, cols):
    return acc


def _ep_scale(acc, rs, cols, *, scale):
    return acc * scale


def _ep_rope(acc, rs, cols, cos_ref, sin_ref, *, q_tiles=0):
    cos, sin = cos_ref[rs, :], sin_ref[rs, :]
    n = acc.shape[1] // LANES
    out = jnp.concatenate(
        [_rope128(acc[:, c * LANES:(c + 1) * LANES], cos, sin) for c in range(n)], axis=1)
    if q_tiles:
        out = out * jnp.where(pl.program_id(1) < q_tiles, LOG2E, 1.0)
    return out


def _ep_sigmoid_bias(acc, rs, cols, b_ref):
    return jax.nn.sigmoid(acc + b_ref[:, cols])


def _proj(h, w, epilogue, *, name, extras=(), extra_specs=None, bm_pref=1024, bn_pref=512):
    m, k = h.shape
    n = w.shape[1]
    bm = _pick(m, bm_pref, BF16_SUBLANES)
    bn = _pick(n, bn_pref)
    specs = [pl.BlockSpec((bm, k), lambda i, j: (i, 0)), pl.BlockSpec((k, bn), lambda i, j: (0, j))]
    specs += list(extra_specs(bm, bn)) if extra_specs else []
    return pl.pallas_call(
        functools.partial(_proj_kernel, epilogue=epilogue, sub=_pick(bn, MXU_COLS),
                          rows=_pick(bm, PROJ_ROW_TILE, BF16_SUBLANES)),
        grid=(m // bm, n // bn),
        in_specs=specs,
        out_specs=pl.BlockSpec((bm, bn), lambda i, j: (i, j)),
        out_shape=jax.ShapeDtypeStruct((m, n), MXU_DTYPE),
        compiler_params=_cparams("parallel", "arbitrary"),
        name=name,
    )(h, w, *extras)


def _rope_specs(bm, bn):
    tab = pl.BlockSpec((bm, LANES), lambda i, j: (i, 0))
    return [tab, tab]


def _bias_specs(bm, bn):
    return [pl.BlockSpec((1, bn), lambda i, j: (0, j))]


def _proj_t_kernel(h_ref, wt_ref, o_ref):
    o_ref[...] = lax.dot_general(wt_ref[...], h_ref[...], _NT,
                                 preferred_element_type=F32).astype(o_ref.dtype)


def _proj_t(h, wt, *, name):
    m, k = h.shape
    n = wt.shape[0]
    bm = _pick(m, 1024)
    bn = _pick(n, 512, BF16_SUBLANES)
    return pl.pallas_call(
        _proj_t_kernel,
        grid=(m // bm, n // bn),
        in_specs=[pl.BlockSpec((bm, k), lambda i, j: (i, 0)), pl.BlockSpec((bn, k), lambda i, j: (j, 0))],
        out_specs=pl.BlockSpec((bn, bm), lambda i, j: (j, i)),
        out_shape=jax.ShapeDtypeStruct((n, m), MXU_DTYPE),
        compiler_params=_cparams("parallel", "arbitrary"),
        name=name,
    )(h, wt)


def _latent_q_kernel(c_ref, g_ref, w_ref, cos_ref, sin_ref, o_ref, *, scale):
    cn = _rms(c_ref[...].astype(F32), g_ref[...]).astype(MXU_DTYPE)
    acc = jnp.dot(cn, w_ref[...], preferred_element_type=F32)
    cos, sin = cos_ref[...], sin_ref[...]
    cols = []
    for c in range(acc.shape[1] // LANES):
        t = acc[:, c * LANES:(c + 1) * LANES]
        cols.append(_rope128(t, cos, sin) if c % 2 else t)
    o_ref[...] = (jnp.concatenate(cols, axis=1) * scale).astype(o_ref.dtype)


def _latent_kv_kernel(c_ref, g_ref, wk_ref, wvt_ref, k_ref, vt_ref):
    cn = _rms(c_ref[...].astype(F32), g_ref[...]).astype(MXU_DTYPE)
    k_ref[...] = jnp.dot(cn, wk_ref[...], preferred_element_type=F32).astype(k_ref.dtype)
    vt_ref[...] = lax.dot_general(wvt_ref[...], cn, _NT, preferred_element_type=F32).astype(vt_ref.dtype)


def _latent_q(src, col_block, g, w, cos, sin, scale):
    m = src.shape[0]
    rank, n = w.shape
    bm = _pick(m, 512, BF16_SUBLANES)
    tab = pl.BlockSpec((bm, LANES), lambda i: (i, 0))
    return pl.pallas_call(
        functools.partial(_latent_q_kernel, scale=scale),
        grid=(m // bm,),
        in_specs=[pl.BlockSpec((bm, rank), lambda i: (i, col_block)),
                  pl.BlockSpec((1, rank), lambda i: (0, 0)),
                  pl.BlockSpec((rank, n), lambda i: (0, 0)), tab, tab],
        out_specs=pl.BlockSpec((bm, n), lambda i: (i, 0)),
        out_shape=jax.ShapeDtypeStruct((m, n), MXU_DTYPE),
        compiler_params=_cparams("parallel"),
        name="mla_q",
    )(src, g[None, :], w, cos, sin)


def _latent_kv(src, col_block, g, wk, wvt):
    m = src.shape[0]
    rank, nk = wk.shape
    nv = wvt.shape[0]
    bm = _pick(m, 512)
    return pl.pallas_call(
        _latent_kv_kernel,
        grid=(m // bm,),
        in_specs=[pl.BlockSpec((bm, rank), lambda i: (i, col_block)),
                  pl.BlockSpec((1, rank), lambda i: (0, 0)),
                  pl.BlockSpec((rank, nk), lambda i: (0, 0)),
                  pl.BlockSpec((nv, rank), lambda i: (0, 0))],
        out_specs=[pl.BlockSpec((bm, nk), lambda i: (i, 0)), pl.BlockSpec((nv, bm), lambda i: (0, i))],
        out_shape=[jax.ShapeDtypeStruct((m, nk), MXU_DTYPE), jax.ShapeDtypeStruct((nv, m), MXU_DTYPE)],
        compiler_params=_cparams("parallel"),
        name="mla_kv",
    )(src, g[None, :], wk, wvt)


def _flash(q_parts, load_k, load_vt, qi, *, bq, bk, dv, unroll):
    diag = bq // bk
    assert unroll % diag == 0
    rel = (lax.broadcasted_iota(jnp.int32, (bk, bq), 0) // CHUNK
           - lax.broadcasted_iota(jnp.int32, (bk, bq), 1) // CHUNK)

    def group(jg, carries, count, masked):
        blocks = [jg * unroll + u for u in range(count)]
        scores = [[lax.dot_general(load_k(j, part), q, _NT, preferred_element_type=F32)
                   for part, q in enumerate(q_parts)] for j in blocks]
        carries = list(carries)
        for u, j in enumerate(blocks):
            d = u - (count - masked)
            for part, s in enumerate(scores[u]):
                m, l, acc = carries[part]
                if d >= 0:
                    s = jnp.where(rel <= -d * (bk // CHUNK), s, -jnp.inf)
                m_new = jnp.maximum(m, s.max(axis=0, keepdims=True))
                alpha = jnp.exp2(m - m_new)
                p = jnp.exp2(s - m_new)
                l = alpha * l + p.sum(axis=0, keepdims=True)
                acc = alpha * acc + jnp.dot(load_vt(j, part), p.astype(MXU_DTYPE),
                                            preferred_element_type=F32)
                carries[part] = (m_new, l, acc)
        return tuple(carries)

    init = (jnp.full((1, bq), -jnp.inf, F32), jnp.zeros((1, bq), F32), jnp.zeros((dv, bq), F32))
    carries = tuple(init for _ in q_parts)
    n_blocks = (qi + 1) * diag
    n_groups = (n_blocks + unroll - 1) // unroll
    carries = lax.fori_loop(0, n_groups - 1, lambda jg, c: group(jg, c, unroll, 0), carries)
    last = (n_blocks - (n_groups - 1) * unroll) // diag - 1
    tails = [functools.partial(group, count=(t + 1) * diag, masked=diag) for t in range(unroll // diag)]
    carries = lax.switch(last, [lambda c, f=f: f(n_groups - 1, c) for f in tails], carries)
    return [(acc, l) for _, l, acc in carries]


KV_GROUP = 4


def _kv_group(seq, bq, bk):
    group = max(bq // bk, min(KV_GROUP, seq // bk))
    assert (seq // bk) % group == 0 and group % (bq // bk) == 0
    return group


def _diff_attn_kernel(lam_ref, q_ref, k_ref, vt_ref, g_ref, o_ref, *, bq, bk, lam_init, unroll):
    qi = pl.program_id(2)
    lam_v = lam_ref[...]
    lam = (jnp.exp(jnp.sum(lam_v[0:1] * lam_v[1:2], axis=1, keepdims=True))
           - jnp.exp(jnp.sum(lam_v[2:3] * lam_v[3:4], axis=1, keepdims=True)) + lam_init)
    q = q_ref[...]
    is_q1 = (lax.broadcasted_iota(jnp.int32, q.shape, 1) & (DIFF_QK_DIM // 2)) == 0
    zero = jnp.zeros_like(q)
    q_parts = [jnp.where(is_q1, q, zero), jnp.where(is_q1, zero, q)]

    def load_k(j, part):
        return k_ref[pl.ds(pl.multiple_of(j * bk, bk), bk), :]

    def load_vt(j, part):
        return vt_ref[:, pl.ds(pl.multiple_of(j * bk, bk), bk)]

    (acc1, l1), (acc2, l2) = _flash(q_parts, load_k, load_vt, qi, bq=bq, bk=bk, dv=DIFF_V_DIM,
                                    unroll=unroll)
    o = acc1 * (1.0 / l1) - lam * (acc2 * (1.0 / l2))
    o = o * lax.rsqrt(jnp.mean(o * o, axis=0, keepdims=True) + EPS) * g_ref[...]
    o_ref[...] = (o * (1.0 - lam_init)).T.astype(o_ref.dtype)


def _diff_attention(qk, vt, lam_vecs, g_sub, lam_init, batch, seq):
    m = qk.shape[0]
    bq = _pick(seq, 512, CHUNK)
    bk = _pick(bq, 256, CHUNK)
    nq = seq // bq
    unroll = _kv_group(seq, bq, bk)
    return pl.pallas_call(
        functools.partial(_diff_attn_kernel, bq=bq, bk=bk, lam_init=lam_init, unroll=unroll),
        grid=(batch, DIFF_HEADS, nq),
        in_specs=[
            pl.BlockSpec((4, DIFF_QK_DIM), lambda b, h, i: (0, 0)),
            pl.BlockSpec((bq, LANES), lambda b, h, i: (b * nq + i, h)),
            pl.BlockSpec((seq, LANES), lambda b, h, i: (b, DIFF_HEADS + h)),
            pl.BlockSpec((DIFF_V_DIM, seq), lambda b, h, i: (h, b)),
            pl.BlockSpec((DIFF_V_DIM, 1), lambda b, h, i: (0, 0)),
        ],
        out_specs=pl.BlockSpec((bq, DIFF_V_DIM), lambda b, h, i: (b * nq + i, h)),
        out_shape=jax.ShapeDtypeStruct((m, DIFF_WIDTH), MXU_DTYPE),
        compiler_params=_cparams("parallel", "parallel", "arbitrary"),
        name="diff_attention",
    )(lam_vecs, qk, qk, vt, g_sub[:, None])


MLA_HEADS_PER_STEP = 2


def _mla_attn_kernel(q_ref, kn_ref, kpe_ref, vt_ref, o_ref, kcat_ref, *, bq, bk, unroll):
    qi = pl.program_id(2)
    qw = 2 * LANES

    @pl.when(qi == 0)
    def _():
        for hh in range(MLA_HEADS_PER_STEP):
            kcat_ref[hh, :, :MLA_NOPE_DIM] = kn_ref[:, hh * MLA_NOPE_DIM:(hh + 1) * MLA_NOPE_DIM]
            kcat_ref[hh, :, MLA_NOPE_DIM:] = kpe_ref[...]

    def load_k(j, part):
        return kcat_ref[part, pl.ds(pl.multiple_of(j * bk, bk), bk), :]

    def load_vt(j, part):
        return vt_ref[part * MLA_V_DIM:(part + 1) * MLA_V_DIM, pl.ds(pl.multiple_of(j * bk, bk), bk)]

    q_parts = [q_ref[:, hh * qw:(hh + 1) * qw] for hh in range(MLA_HEADS_PER_STEP)]
    outs = _flash(q_parts, load_k, load_vt, qi, bq=bq, bk=bk, dv=MLA_V_DIM, unroll=unroll)
    for hh, (acc, l) in enumerate(outs):
        o_ref[:, hh * MLA_V_DIM:(hh + 1) * MLA_V_DIM] = (acc * (1.0 / l)).T.astype(o_ref.dtype)


def _mla_attention(q, k_nope, kpe, vt, batch, seq):
    m = q.shape[0]
    bq = _pick(seq, 512, CHUNK)
    bk = _pick(bq, 256, CHUNK)
    nq = seq // bq
    hs = MLA_HEADS_PER_STEP
    qw = 2 * LANES
    unroll = _kv_group(seq, bq, bk)
    return pl.pallas_call(
        functools.partial(_mla_attn_kernel, bq=bq, bk=bk, unroll=unroll),
        grid=(batch, MLA_HEADS // hs, nq),
        in_specs=[
            pl.BlockSpec((bq, hs * qw), lambda b, h, i: (b * nq + i, h)),
            pl.BlockSpec((seq, hs * MLA_NOPE_DIM), lambda b, h, i: (b, h)),
            pl.BlockSpec((seq, LANES), lambda b, h, i: (b, 0)),
            pl.BlockSpec((hs * MLA_V_DIM, seq), lambda b, h, i: (h, b)),
        ],
        out_specs=pl.BlockSpec((bq, hs * MLA_V_DIM), lambda b, h, i: (b * nq + i, h)),
        out_shape=jax.ShapeDtypeStruct((m, MLA_WIDTH), MXU_DTYPE),
        scratch_shapes=[pltpu.VMEM((hs, seq, qw), MXU_DTYPE)],
        compiler_params=_cparams("parallel", "parallel", "arbitrary"),
        name="mla_attention",
    )(q, k_nope, kpe, vt)


def _merge_kernel(od_ref, om_ref, wd_ref, wm_ref, ga_ref, gb_ref, o_ref):
    a = jnp.dot(od_ref[...], wd_ref[...], preferred_element_type=F32)
    b = jnp.dot(om_ref[...], wm_ref[...], preferred_element_type=F32)
    o_ref[...] = (ga_ref[...].astype(F32) * a + gb_ref[...].astype(F32) * b).astype(o_ref.dtype)


def _merge(o_diff, o_mla, w_d, w_m, gates):
    m = o_diff.shape[0]
    n = w_d.shape[1]
    bm = _pick(m, 1024, BF16_SUBLANES)
    bn = _pick(n, 512)
    nb = n // bn
    return pl.pallas_call(
        _merge_kernel,
        grid=(m // bm, nb),
        in_specs=[
            pl.BlockSpec((bm, DIFF_WIDTH), lambda i, j: (i, 0)),
            pl.BlockSpec((bm, MLA_WIDTH), lambda i, j: (i, 0)),
            pl.BlockSpec((DIFF_WIDTH, bn), lambda i, j: (0, j)),
            pl.BlockSpec((MLA_WIDTH, bn), lambda i, j: (0, j)),
            pl.BlockSpec((bm, bn), lambda i, j: (i, j)),
            pl.BlockSpec((bm, bn), lambda i, j: (i, j + nb)),
        ],
        out_specs=pl.BlockSpec((bm, bn), lambda i, j: (i, j)),
        out_shape=jax.ShapeDtypeStruct((m, n), MXU_DTYPE),
        compiler_params=_cparams("parallel", "arbitrary"),
        name="branch_merge",
    )(o_diff, o_mla, w_d, w_m, gates, gates)


def _residual_norm(y, x_ref, gp_ref, gn_ref, xo_ref, ho_ref):
    xn = x_ref[...] + _rms(y, gp_ref[...])
    xo_ref[...] = xn
    ho_ref[...] = _rms(xn, gn_ref[...]).astype(ho_ref.dtype)


def _out_proj_kernel(lhs_ref, w_ref, x_ref, gp_ref, gn_ref, xo_ref, ho_ref, *, row_tiles):
    rows = lhs_ref.shape[0] // row_tiles
    for t in range(row_tiles):
        r = pl.ds(t * rows, rows)
        y = jnp.dot(lhs_ref[r, :], w_ref[...], preferred_element_type=F32)
        _residual_norm(y, x_ref.at[r], gp_ref, gn_ref, xo_ref.at[r], ho_ref.at[r])


def _out_proj(lhs, w, x, g_post, g_next):
    m, kdim = lhs.shape
    n = w.shape[1]
    bm = _pick(m, 512, BF16_SUBLANES)
    row = lambda i: (i, 0)
    const = lambda i: (0, 0)
    return pl.pallas_call(
        functools.partial(_out_proj_kernel, row_tiles=2 if bm % (2 * BF16_SUBLANES) == 0 else 1),
        grid=(m // bm,),
        in_specs=[
            pl.BlockSpec((bm, kdim), row),
            pl.BlockSpec((kdim, n), const),
            pl.BlockSpec((bm, n), row),
            pl.BlockSpec((1, n), const),
            pl.BlockSpec((1, n), const),
        ],
        out_specs=[pl.BlockSpec((bm, n), row), pl.BlockSpec((bm, n), row)],
        out_shape=[jax.ShapeDtypeStruct((m, n), F32), jax.ShapeDtypeStruct((m, n), MXU_DTYPE)],
        compiler_params=_cparams("parallel"),
        name="out_proj",
    )(lhs, w, x, g_post[None, :], g_next[None, :])


def _mem_attn_kernel(q_ref, kv_ref, o_ref, *, dh):
    width = MEM_HEADS * dh
    outs = []
    for h in range(MEM_HEADS):
        q = q_ref[:, h * dh:(h + 1) * dh]
        k = kv_ref[:, h * dh:(h + 1) * dh]
        v = kv_ref[:, width + h * dh:width + (h + 1) * dh]
        s = lax.dot_general(q, k, _NT, preferred_element_type=F32)
        p = jnp.exp(s - s.max(axis=1, keepdims=True))
        l = p.sum(axis=1, keepdims=True)
        outs.append(jnp.dot(p.astype(MXU_DTYPE), v, preferred_element_type=F32) / l)
    o_ref[...] = jnp.concatenate(outs, axis=1).astype(o_ref.dtype)


def _mem_attention(q, kv, seq, mem_tokens):
    m, d = q.shape
    bq = _pick(seq, 512, BF16_SUBLANES)
    per_batch = seq // bq
    return pl.pallas_call(
        functools.partial(_mem_attn_kernel, dh=d // MEM_HEADS),
        grid=(m // bq,),
        in_specs=[pl.BlockSpec((bq, d), lambda i: (i, 0)),
                  pl.BlockSpec((mem_tokens, 2 * d), lambda i: (i // per_batch, 0))],
        out_specs=pl.BlockSpec((bq, d), lambda i: (i, 0)),
        out_shape=jax.ShapeDtypeStruct((m, d), MXU_DTYPE),
        compiler_params=_cparams("parallel"),
        name="mem_attention",
    )(q, kv)


HALO = BF16_SUBLANES


def _ffn_kernel(h_ref, halo_ref, wa_ref, wg_ref, cwa_ref, cwg_ref, cba_ref, cbg_ref, wd_ref, x_ref, gp_ref,
                gn_ref, xo_ref, ho_ref, hs_ref, u_ref, acc_ref, *, bm, blocks_per_seq, nb, sub, row_tiles):
    i = pl.program_id(0)
    j = pl.program_id(1)

    @pl.when(j == 0)
    def _():
        halo = halo_ref[...]
        first = (i % blocks_per_seq) == 0
        hs_ref[:HALO, :] = jnp.where(first, jnp.zeros_like(halo), halo)
        hs_ref[HALO:, :] = h_ref[...]
        acc_ref[...] = jnp.zeros_like(acc_ref)

    hs = hs_ref[...]

    tiles = [slice(t * sub, (t + 1) * sub) for t in range(wa_ref.shape[1] // sub)]
    for t, cols in enumerate(tiles):
        u_ref[2 * t] = jnp.dot(hs, wa_ref[:, cols], preferred_element_type=F32)
        u_ref[2 * t + 1] = jnp.dot(hs, wg_ref[:, cols], preferred_element_type=F32)

    def conv(slot, cols, row0, rows, cw_ref, cb_ref):
        cw = cw_ref[:, cols]
        c = cb_ref[:, cols]
        for tap in range(CONV_WIDTH):
            c = c + u_ref[slot, pl.ds(row0 + HALO - (CONV_WIDTH - 1) + tap, rows), :] * cw[tap:tap + 1]
        return c

    rows = bm // row_tiles
    for r in range(row_tiles):
        acts = []
        for t, cols in enumerate(tiles):
            a = conv(2 * t, cols, r * rows, rows, cwa_ref, cba_ref)
            g = conv(2 * t + 1, cols, r * rows, rows, cwg_ref, cbg_ref)
            acts.append((a * jax.nn.sigmoid(a) * g).astype(MXU_DTYPE))
        acc_ref[pl.ds(r * rows, rows), :] += jnp.dot(jnp.concatenate(acts, axis=1), wd_ref[...],
                                                     preferred_element_type=F32)

    @pl.when(j == nb - 1)
    def _():
        _residual_norm(acc_ref[...], x_ref, gp_ref, gn_ref, xo_ref, ho_ref)


def _ffn(h, w_up, conv_w, conv_b, w_down, x, g_post, g_next, seq):
    m, d = h.shape
    ff = w_up.shape[1] // 2
    bm = _pick(seq, 512, HALO)
    bn = _pick(ff, 512)
    sub = _pick(bn, MXU_COLS)
    nb = ff // bn
    halo_blocks = bm // HALO
    row = lambda i, j: (i, 0)
    const = lambda i, j: (0, 0)
    up_a = lambda i, j: (0, j)
    up_g = lambda i, j: (0, j + nb)
    return pl.pallas_call(
        functools.partial(_ffn_kernel, bm=bm, blocks_per_seq=seq // bm, nb=nb, sub=sub,
                          row_tiles=2 if bm % (2 * BF16_SUBLANES) == 0 else 1),
        grid=(m // bm, nb),
        in_specs=[
            pl.BlockSpec((bm, d), row),
            pl.BlockSpec((HALO, d), lambda i, j: (jnp.maximum(i * halo_blocks - 1, 0), 0)),
            pl.BlockSpec((d, bn), up_a),
            pl.BlockSpec((d, bn), up_g),
            pl.BlockSpec((CONV_WIDTH, bn), up_a),
            pl.BlockSpec((CONV_WIDTH, bn), up_g),
            pl.BlockSpec((1, bn), up_a),
            pl.BlockSpec((1, bn), up_g),
            pl.BlockSpec((bn, d), lambda i, j: (j, 0)),
            pl.BlockSpec((bm, d), row),
            pl.BlockSpec((1, d), const),
            pl.BlockSpec((1, d), const),
        ],
        out_specs=[pl.BlockSpec((bm, d), row), pl.BlockSpec((bm, d), row)],
        out_shape=[jax.ShapeDtypeStruct((m, d), F32), jax.ShapeDtypeStruct((m, d), MXU_DTYPE)],
        scratch_shapes=[pltpu.VMEM((HALO + bm, d), MXU_DTYPE),
                        pltpu.VMEM((2 * (bn // sub), HALO + bm, sub), F32),
                        pltpu.VMEM((bm, d), F32)],
        compiler_params=_cparams("parallel", "arbitrary"),
        name="ffn",
    )(h, h, w_up, w_up, conv_w, conv_w, conv_b[None, :], conv_b[None, :], w_down, x,
      g_post[None, :], g_next[None, :])


def kernel(x, mem, positions, g_pre_mix, w_in, b_gate, lam_q1, lam_k1, lam_q2, lam_k2, g_diff_sub, g_cq,
           w_uq, g_ckv, w_ukv, w_br_diff, w_br_mla, w_mix_out, g_post_mix, g_pre_x, g_mem, w_q_x, w_kv_x,
           w_o_x, g_post_x, g_pre_ffn, w_up, conv_w, conv_b, w_down, g_post_ffn):
    batch, seq, d = x.shape
    mem_tokens = mem.shape[1]
    depth = w_in.shape[0]
    m = batch * seq
    cast = lambda a: a.astype(MXU_DTYPE)

    qw = DIFF_HEADS * 2 * DIFF_QK_DIM
    o_v = 2 * qw
    o_cq = o_v + DIFF_WIDTH
    o_ckv = o_cq + MLA_Q_RANK
    o_kr = o_ckv + MLA_KV_RANK
    o_gt = o_kr + MLA_ROPE_DIM
    assert w_in.shape[2] == o_gt + 2 * d
    assert MLA_Q_RANK % MLA_KV_RANK == 0
    qhead = MLA_NOPE_DIM + MLA_ROPE_DIM
    kvhead = MLA_NOPE_DIM + MLA_V_DIM

    cos, sin = _rope_tables(positions)
    xf = x.reshape(m, d)
    memf = mem.reshape(batch * mem_tokens, d)
    h = _rmsnorm(xf, g_pre_mix[0])

    for l in range(depth):
        lam_init = 0.8 - 0.6 * math.exp(-0.3 * l)
        wl = w_in[l]
        w_qk = cast(_pair_halves(
            jnp.concatenate([wl[:, :qw] * (DIFF_QK_DIM ** -0.5), wl[:, qw:o_v]], axis=1)))
        w_vt = cast(wl[:, o_v:o_cq].T)
        w_lat = cast(wl[:, o_cq:o_kr])
        w_kpe = cast(_pair_halves(jnp.pad(wl[:, o_kr:o_gt], ((0, 0), (0, LANES - MLA_ROPE_DIM)))))
        w_gate = cast(wl[:, o_gt:])
        w_uq_h = jnp.pad(w_uq[l].reshape(MLA_Q_RANK, MLA_HEADS, qhead),
                         ((0, 0), (0, 0), (0, 2 * LANES - qhead)))
        w_uq_p = cast(jnp.concatenate([w_uq_h[:, :, :MLA_NOPE_DIM], _pair_halves(w_uq_h[:, :, MLA_NOPE_DIM:])],
                                      axis=2).reshape(MLA_Q_RANK, -1))
        w_ukv_h = w_ukv[l].reshape(MLA_KV_RANK, MLA_HEADS, kvhead)
        w_uk = cast(w_ukv_h[:, :, :MLA_NOPE_DIM].reshape(MLA_KV_RANK, -1))
        w_uvt = cast(w_ukv_h[:, :, MLA_NOPE_DIM:].reshape(MLA_KV_RANK, -1).T)

        bn_qk = _pick(2 * qw, 512)
        qk = _proj(h, w_qk, functools.partial(_ep_rope, q_tiles=qw // bn_qk), name="proj_qk",
                   extras=(cos, sin), extra_specs=_rope_specs)
        vt = _proj_t(h, w_vt, name="proj_vt")
        lat = _proj(h, w_lat, _ep_plain, name="proj_latent", bn_pref=MLA_Q_RANK + MLA_KV_RANK)
        kpe = _proj(h, w_kpe, _ep_rope, name="proj_kpe", extras=(cos, sin), extra_specs=_rope_specs)
        gates = _proj(h, w_gate, _ep_sigmoid_bias, name="proj_gates", extras=(b_gate[l][None, :],),
                      extra_specs=_bias_specs)
        q_mla = _latent_q(lat, 0, g_cq[l], w_uq_p, cos, sin, qhead ** -0.5 * LOG2E)
        k_mla, vt_mla = _latent_kv(lat, MLA_Q_RANK // MLA_KV_RANK, g_ckv[l], w_uk, w_uvt)
        lam_vecs = jnp.stack([lam_q1[l], lam_k1[l], lam_q2[l], lam_k2[l]])
        o_diff = _diff_attention(qk, vt, lam_vecs, g_diff_sub[l], lam_init, batch, seq)
        o_mla = _mla_attention(q_mla, k_mla, kpe, vt_mla, batch, seq)
        merged = _merge(o_diff, o_mla, cast(w_br_diff[l]), cast(w_br_mla[l]), gates)
        xf, h = _out_proj(merged, cast(w_mix_out[l]), xf, g_post_mix[l], g_pre_x[l])

        q_x = _proj(h, cast(w_q_x[l]), functools.partial(_ep_scale, scale=(d // MEM_HEADS) ** -0.5),
                    name="proj_qx")
        kv_x = _proj(_rmsnorm(memf, g_mem[l]), cast(w_kv_x[l]), _ep_plain, name="proj_kvx")
        o_x = _mem_attention(q_x, kv_x, seq, mem_tokens)
        xf, h = _out_proj(o_x, cast(w_o_x[l]), xf, g_post_x[l], g_pre_ffn[l])

        g_next = g_pre_mix[l + 1] if l + 1 < depth else g_pre_mix[0]
        xf, h = _ffn(h, cast(w_up[l]), conv_w[l], conv_b[l], cast(w_down[l]), xf, g_post_ffn[l], g_next, seq)

    return xf.reshape(batch, seq, d)
```

```python
import functools
import math

import jax
import jax.numpy as jnp
from jax import lax
from jax.experimental import pallas as pl
from jax.experimental.pallas import tpu as pltpu

CHUNK = 64
ROPE_THETA = 10000.0
EPS = 1e-6
DIFF_HEADS = 8
DIFF_QK_DIM = 64
DIFF_V_DIM = 128
MLA_HEADS = 8
MLA_Q_RANK = 512
MLA_KV_RANK = 256
MLA_NOPE_DIM = 128
MLA_ROPE_DIM = 64
MLA_V_DIM = 128
MEM_HEADS = 4
CONV_WIDTH = 3
DIFF_WIDTH = DIFF_HEADS * DIFF_V_DIM
MLA_WIDTH = MLA_HEADS * MLA_V_DIM

LANES = 128
BF16_SUBLANES = 16
MXU_COLS = 256
VMEM_LIMIT_BYTES = 56 * 2**20

MXU_DTYPE = jnp.bfloat16
F32 = jnp.float32
LOG2E = math.log2(math.e)


def _pick(n, pref, mult=LANES):
    if n <= pref:
        return n
    best = None
    for d in range(mult, pref + 1, mult):
        if n % d == 0:
            best = d
    assert best is not None, (n, pref, mult)
    return best


def _cparams(*sem, flags=None):
    return pltpu.CompilerParams(dimension_semantics=sem, vmem_limit_bytes=VMEM_LIMIT_BYTES, flags=flags)


def _rms(x, g):
    return x * lax.rsqrt(jnp.mean(x * x, axis=-1, keepdims=True) + EPS) * g


def _rope128(t, cos, sin):
    return t * cos + pltpu.roll(t, LANES // 2, 1) * sin


def _pair_halves(w):
    q = LANES // 4
    shape = w.shape
    w = w.reshape(*shape[:-1], shape[-1] // LANES, 2, 2, q)
    return jnp.swapaxes(w, -3, -2).reshape(shape)


_NT = (((1,), (1,)), ((), ()))


def _tables_kernel(pos_ref, inv_ref, sgn_ref, cos_ref, sin_ref):
    ang = pos_ref[...].astype(F32) * inv_ref[...]
    cos_ref[...] = jnp.cos(ang)
    sin_ref[...] = jnp.sin(ang) * sgn_ref[...]


def _rope_tables(positions):
    m = positions.size
    d = DIFF_QK_DIM
    inv = ROPE_THETA ** (-jnp.arange(0, d, 2, dtype=F32) / d)
    inv128 = jnp.tile(inv, LANES // (d // 2))[None, :]
    sgn128 = jnp.concatenate([-jnp.ones((LANES // 2,), F32), jnp.ones((LANES // 2,), F32)])[None, :]
    bm = _pick(m, 2048, 8)
    row = pl.BlockSpec((bm, LANES), lambda i: (i, 0))
    const = pl.BlockSpec((1, LANES), lambda i: (0, 0))
    return pl.pallas_call(
        _tables_kernel,
        grid=(m // bm,),
        in_specs=[pl.BlockSpec((bm, 1), lambda i: (i, 0)), const, const],
        out_specs=[row, row],
        out_shape=[jax.ShapeDtypeStruct((m, LANES), F32)] * 2,
        compiler_params=_cparams("parallel"),
        name="rope_tables",
    )(positions.reshape(m, 1), inv128, sgn128)


def _rmsnorm_kernel(x_ref, g_ref, o_ref):
    o_ref[...] = _rms(x_ref[...], g_ref[...]).astype(o_ref.dtype)


def _rmsnorm(x, g):
    m, d = x.shape
    bm = _pick(m, 512, 8)
    return pl.pallas_call(
        _rmsnorm_kernel,
        grid=(m // bm,),
        in_specs=[pl.BlockSpec((bm, d), lambda i: (i, 0)), pl.BlockSpec((1, d), lambda i: (0, 0))],
        out_specs=pl.BlockSpec((bm, d), lambda i: (i, 0)),
        out_shape=jax.ShapeDtypeStruct((m, d), MXU_DTYPE),
        compiler_params=_cparams("parallel"),
        name="rmsnorm",
    )(x, g[None, :])


PROJ_ROW_TILE = 256


def _proj_kernel(h_ref, w_ref, *rest, epilogue, sub, rows):
    *extra, o_ref = rest
    for r in range(h_ref.shape[0] // rows):
        rs = pl.ds(r * rows, rows)
        h = h_ref[rs, :]
        for t in range(w_ref.shape[1] // sub):
            cols = slice(t * sub, (t + 1) * sub)
            acc = jnp.dot(h, w_ref[:, cols], preferred_element_type=F32)
            o_ref[rs, cols] = epilogue(acc, rs, cols, *extra).astype(o_ref.dtype)


def _ep_plain(acc, rs, cols):
    return acc


def _ep_scale(acc, rs, cols, *, scale):
    return acc * scale


def _ep_rope(acc, rs, cols, cos_ref, sin_ref, *, q_tiles=0):
    cos, sin = cos_ref[rs, :], sin_ref[rs, :]
    n = acc.shape[1] // LANES
    out = jnp.concatenate(
        [_rope128(acc[:, c * LANES:(c + 1) * LANES], cos, sin) for c in range(n)], axis=1)
    if q_tiles:
        out = out * jnp.where(pl.program_id(1) < q_tiles, LOG2E, 1.0)
    return out


def _ep_sigmoid_bias(acc, rs, cols, b_ref):
    return jax.nn.sigmoid(acc + b_ref[:, cols])


def _proj(h, w, epilogue, *, name, extras=(), extra_specs=None, bm_pref=1024, bn_pref=512):
    m, k = h.shape
    n = w.shape[1]
    bm = _pick(m, bm_pref, BF16_SUBLANES)
    bn = _pick(n, bn_pref)
    specs = [pl.BlockSpec((bm, k), lambda i, j: (i, 0)), pl.BlockSpec((k, bn), lambda i, j: (0, j))]
    specs += list(extra_specs(bm, bn)) if extra_specs else []
    return pl.pallas_call(
        functools.partial(_proj_kernel, epilogue=epilogue, sub=_pick(bn, MXU_COLS),
                          rows=_pick(bm, PROJ_ROW_TILE, BF16_SUBLANES)),
        grid=(m // bm, n // bn),
        in_specs=specs,
        out_specs=pl.BlockSpec((bm, bn), lambda i, j: (i, j)),
        out_shape=jax.ShapeDtypeStruct((m, n), MXU_DTYPE),
        compiler_params=_cparams("parallel", "arbitrary"),
        name=name,
    )(h, w, *extras)


def _rope_specs(bm, bn):
    tab = pl.BlockSpec((bm, LANES), lambda i, j: (i, 0))
    return [tab, tab]


def _bias_specs(bm, bn):
    return [pl.BlockSpec((1, bn), lambda i, j: (0, j))]


def _proj_t_kernel(h_ref, wt_ref, o_ref):
    o_ref[...] = lax.dot_general(wt_ref[...], h_ref[...], _NT,
                                 preferred_element_type=F32).astype(o_ref.dtype)


def _proj_t(h, wt, *, name):
    m, k = h.shape
    n = wt.shape[0]
    bm = _pick(m, 1024)
    bn = _pick(n, 512, BF16_SUBLANES)
    return pl.pallas_call(
        _proj_t_kernel,
        grid=(m // bm, n // bn),
        in_specs=[pl.BlockSpec((bm, k), lambda i, j: (i, 0)), pl.BlockSpec((bn, k), lambda i, j: (j, 0))],
        out_specs=pl.BlockSpec((bn, bm), lambda i, j: (j, i)),
        out_shape=jax.ShapeDtypeStruct((n, m), MXU_DTYPE),
        compiler_params=_cparams("parallel", "arbitrary"),
        name=name,
    )(h, wt)


def _latent_q_kernel(c_ref, g_ref, w_ref, cos_ref, sin_ref, o_ref, *, scale):
    cn = _rms(c_ref[...].astype(F32), g_ref[...]).astype(MXU_DTYPE)
    acc = jnp.dot(cn, w_ref[...], preferred_element_type=F32)
    cos, sin = cos_ref[...], sin_ref[...]
    cols = []
    for c in range(acc.shape[1] // LANES):
        t = acc[:, c * LANES:(c + 1) * LANES]
        cols.append(_rope128(t, cos, sin) if c % 2 else t)
    o_ref[...] = (jnp.concatenate(cols, axis=1) * scale).astype(o_ref.dtype)


def _latent_kv_kernel(c_ref, g_ref, wk_ref, wvt_ref, k_ref, vt_ref):
    cn = _rms(c_ref[...].astype(F32), g_ref[...]).astype(MXU_DTYPE)
    k_ref[...] = jnp.dot(cn, wk_ref[...], preferred_element_type=F32).astype(k_ref.dtype)
    vt_ref[...] = lax.dot_general(wvt_ref[...], cn, _NT, preferred_element_type=F32).astype(vt_ref.dtype)


def _latent_q(src, col_block, g, w, cos, sin, scale):
    m = src.shape[0]
    rank, n = w.shape
    bm = _pick(m, 512, BF16_SUBLANES)
    tab = pl.BlockSpec((bm, LANES), lambda i: (i, 0))
    return pl.pallas_call(
        functools.partial(_latent_q_kernel, scale=scale),
        grid=(m // bm,),
        in_specs=[pl.BlockSpec((bm, rank), lambda i: (i, col_block)),
                  pl.BlockSpec((1, rank), lambda i: (0, 0)),
                  pl.BlockSpec((rank, n), lambda i: (0, 0)), tab, tab],
        out_specs=pl.BlockSpec((bm, n), lambda i: (i, 0)),
        out_shape=jax.ShapeDtypeStruct((m, n), MXU_DTYPE),
        compiler_params=_cparams("parallel"),
        name="mla_q",
    )(src, g[None, :], w, cos, sin)


def _latent_kv(src, col_block, g, wk, wvt):
    m = src.shape[0]
    rank, nk = wk.shape
    nv = wvt.shape[0]
    bm = _pick(m, 512)
    return pl.pallas_call(
        _latent_kv_kernel,
        grid=(m // bm,),
        in_specs=[pl.BlockSpec((bm, rank), lambda i: (i, col_block)),
                  pl.BlockSpec((1, rank), lambda i: (0, 0)),
                  pl.BlockSpec((rank, nk), lambda i: (0, 0)),
                  pl.BlockSpec((nv, rank), lambda i: (0, 0))],
        out_specs=[pl.BlockSpec((bm, nk), lambda i: (i, 0)), pl.BlockSpec((nv, bm), lambda i: (0, i))],
        out_shape=[jax.ShapeDtypeStruct((m, nk), MXU_DTYPE), jax.ShapeDtypeStruct((nv, m), MXU_DTYPE)],
        compiler_params=_cparams("parallel"),
        name="mla_kv",
    )(src, g[None, :], wk, wvt)


def _flash(q_parts, load_k, load_vt, qi, *, bq, bk, dv, unroll):
    diag = bq // bk
    assert unroll % diag == 0
    rel = (lax.broadcasted_iota(jnp.int32, (bk, bq), 0) // CHUNK
           - lax.broadcasted_iota(jnp.int32, (bk, bq), 1) // CHUNK)

    def group(jg, carries, count, masked):
        blocks = [jg * unroll + u for u in range(count)]
        scores = [[lax.dot_general(load_k(j, part), q, _NT, preferred_element_type=F32)
                   for part, q in enumerate(q_parts)] for j in blocks]
        carries = list(carries)
        for u, j in enumerate(blocks):
            d = u - (count - masked)
            for part, s in enumerate(scores[u]):
                m, l, acc = carries[part]
                if d >= 0:
                    s = jnp.where(rel <= -d * (bk // CHUNK), s, -jnp.inf)
                m_new = jnp.maximum(m, s.max(axis=0, keepdims=True))
                alpha = jnp.exp2(m - m_new)
                p = jnp.exp2(s - m_new)
                l = alpha * l + p.sum(axis=0, keepdims=True)
                acc = alpha * acc + jnp.dot(load_vt(j, part), p.astype(MXU_DTYPE),
                                            preferred_element_type=F32)
                carries[part] = (m_new, l, acc)
        return tuple(carries)

    init = (jnp.full((1, bq), -jnp.inf, F32), jnp.zeros((1, bq), F32), jnp.zeros((dv, bq), F32))
    carries = tuple(init for _ in q_parts)
    n_blocks = (qi + 1) * diag
    n_groups = (n_blocks + unroll - 1) // unroll
    carries = lax.fori_loop(0, n_groups - 1, lambda jg, c: group(jg, c, unroll, 0), carries)
    last = (n_blocks - (n_groups - 1) * unroll) // diag - 1
    tails = [functools.partial(group, count=(t + 1) * diag, masked=diag) for t in range(unroll // diag)]
    carries = lax.switch(last, [lambda c, f=f: f(n_groups - 1, c) for f in tails], carries)
    return [(acc, l) for _, l, acc in carries]


KV_GROUP_KEYS = 1024


def _kv_group(seq, bq, bk):
    group = max(bq // bk, min(KV_GROUP_KEYS // bk, seq // bk))
    assert (seq // bk) % group == 0 and group % (bq // bk) == 0
    return group


DIFF_HEADS_PER_STEP = 2


def _diff_attn_kernel(lam_ref, q_ref, k_ref, vt_ref, g_ref, o_ref, *, bq, bk, lam_init, unroll):
    qi = pl.program_id(2)
    lam_v = lam_ref[...]
    lam = (jnp.exp(jnp.sum(lam_v[0:1] * lam_v[1:2], axis=1, keepdims=True))
           - jnp.exp(jnp.sum(lam_v[2:3] * lam_v[3:4], axis=1, keepdims=True)) + lam_init)
    hw = LANES
    is_q1 = (lax.broadcasted_iota(jnp.int32, (bq, hw), 1) & (DIFF_QK_DIM // 2)) == 0
    q_parts = []
    for hh in range(DIFF_HEADS_PER_STEP):
        q = q_ref[:, hh * hw:(hh + 1) * hw]
        zero = jnp.zeros_like(q)
        q_parts += [jnp.where(is_q1, q, zero), jnp.where(is_q1, zero, q)]

    def load_k(j, part):
        hh = part // 2
        return k_ref[pl.ds(pl.multiple_of(j * bk, bk), bk), hh * hw:(hh + 1) * hw]

    def load_vt(j, part):
        hh = part // 2
        return vt_ref[hh * DIFF_V_DIM:(hh + 1) * DIFF_V_DIM, pl.ds(pl.multiple_of(j * bk, bk), bk)]

    outs = _flash(q_parts, load_k, load_vt, qi, bq=bq, bk=bk, dv=DIFF_V_DIM, unroll=unroll)
    for hh in range(DIFF_HEADS_PER_STEP):
        (acc1, l1), (acc2, l2) = outs[2 * hh], outs[2 * hh + 1]
        o = acc1 * (1.0 / l1) - lam * (acc2 * (1.0 / l2))
        o = o * lax.rsqrt(jnp.mean(o * o, axis=0, keepdims=True) + EPS) * g_ref[...]
        o_ref[:, hh * DIFF_V_DIM:(hh + 1) * DIFF_V_DIM] = (o * (1.0 - lam_init)).T.astype(o_ref.dtype)


def _diff_attention(qk, vt, lam_vecs, g_sub, lam_init, batch, seq):
    m = qk.shape[0]
    bq = _pick(seq, 512, CHUNK)
    bk = _pick(bq, 512, CHUNK)
    nq = seq // bq
    hs = DIFF_HEADS_PER_STEP
    steps = DIFF_HEADS // hs
    unroll = _kv_group(seq, bq, bk)
    return pl.pallas_call(
        functools.partial(_diff_attn_kernel, bq=bq, bk=bk, lam_init=lam_init, unroll=unroll),
        grid=(batch, steps, nq),
        in_specs=[
            pl.BlockSpec((4, DIFF_QK_DIM), lambda b, h, i: (0, 0)),
            pl.BlockSpec((bq, hs * LANES), lambda b, h, i: (b * nq + i, h)),
            pl.BlockSpec((seq, hs * LANES), lambda b, h, i: (b, steps + h)),
            pl.BlockSpec((hs * DIFF_V_DIM, seq), lambda b, h, i: (h, b)),
            pl.BlockSpec((DIFF_V_DIM, 1), lambda b, h, i: (0, 0)),
        ],
        out_specs=pl.BlockSpec((bq, hs * DIFF_V_DIM), lambda b, h, i: (b * nq + i, h)),
        out_shape=jax.ShapeDtypeStruct((m, DIFF_WIDTH), MXU_DTYPE),
        compiler_params=_cparams("parallel", "parallel", "arbitrary"),
        name="diff_attention",
    )(lam_vecs, qk, qk, vt, g_sub[:, None])


MLA_HEADS_PER_STEP = 4


def _mla_attn_kernel(q_ref, kn_ref, kpe_ref, vt_ref, o_ref, kcat_ref, *, bq, bk, unroll):
    qi = pl.program_id(2)
    qw = 2 * LANES

    @pl.when(qi == 0)
    def _():
        for hh in range(MLA_HEADS_PER_STEP):
            kcat_ref[hh, :, :MLA_NOPE_DIM] = kn_ref[:, hh * MLA_NOPE_DIM:(hh + 1) * MLA_NOPE_DIM]
            kcat_ref[hh, :, MLA_NOPE_DIM:] = kpe_ref[...]

    def load_k(j, part):
        return kcat_ref[part, pl.ds(pl.multiple_of(j * bk, bk), bk), :]

    def load_vt(j, part):
        return vt_ref[part * MLA_V_DIM:(part + 1) * MLA_V_DIM, pl.ds(pl.multiple_of(j * bk, bk), bk)]

    q_parts = [q_ref[:, hh * qw:(hh + 1) * qw] for hh in range(MLA_HEADS_PER_STEP)]
    outs = _flash(q_parts, load_k, load_vt, qi, bq=bq, bk=bk, dv=MLA_V_DIM, unroll=unroll)
    for hh, (acc, l) in enumerate(outs):
        o_ref[:, hh * MLA_V_DIM:(hh + 1) * MLA_V_DIM] = (acc * (1.0 / l)).T.astype(o_ref.dtype)


def _mla_attention(q, k_nope, kpe, vt, batch, seq):
    m = q.shape[0]
    bq = _pick(seq, 512, CHUNK)
    bk = _pick(bq, 512, CHUNK)
    nq = seq // bq
    hs = MLA_HEADS_PER_STEP
    qw = 2 * LANES
    unroll = _kv_group(seq, bq, bk)
    return pl.pallas_call(
        functools.partial(_mla_attn_kernel, bq=bq, bk=bk, unroll=unroll),
        grid=(batch, MLA_HEADS // hs, nq),
        in_specs=[
            pl.BlockSpec((bq, hs * qw), lambda b, h, i: (b * nq + i, h)),
            pl.BlockSpec((seq, hs * MLA_NOPE_DIM), lambda b, h, i: (b, h)),
            pl.BlockSpec((seq, LANES), lambda b, h, i: (b, 0)),
            pl.BlockSpec((hs * MLA_V_DIM, seq), lambda b, h, i: (h, b)),
        ],
        out_specs=pl.BlockSpec((bq, hs * MLA_V_DIM), lambda b, h, i: (b * nq + i, h)),
        out_shape=jax.ShapeDtypeStruct((m, MLA_WIDTH), MXU_DTYPE),
        scratch_shapes=[pltpu.VMEM((hs, seq, qw), MXU_DTYPE)],
        compiler_params=_cparams("parallel", "parallel", "arbitrary"),
        name="mla_attention",
    )(q, k_nope, kpe, vt)


def _merge_kernel(od_ref, om_ref, wd_ref, wm_ref, ga_ref, gb_ref, o_ref):
    a = jnp.dot(od_ref[...], wd_ref[...], preferred_element_type=F32)
    b = jnp.dot(om_ref[...], wm_ref[...], preferred_element_type=F32)
    o_ref[...] = (ga_ref[...].astype(F32) * a + gb_ref[...].astype(F32) * b).astype(o_ref.dtype)


def _merge(o_diff, o_mla, w_d, w_m, gates):
    m = o_diff.shape[0]
    n = w_d.shape[1]
    bm = _pick(m, 1024, BF16_SUBLANES)
    bn = _pick(n, 512)
    nb = n // bn
    return pl.pallas_call(
        _merge_kernel,
        grid=(m // bm, nb),
        in_specs=[
            pl.BlockSpec((bm, DIFF_WIDTH), lambda i, j: (i, 0)),
            pl.BlockSpec((bm, MLA_WIDTH), lambda i, j: (i, 0)),
            pl.BlockSpec((DIFF_WIDTH, bn), lambda i, j: (0, j)),
            pl.BlockSpec((MLA_WIDTH, bn), lambda i, j: (0, j)),
            pl.BlockSpec((bm, bn), lambda i, j: (i, j)),
            pl.BlockSpec((bm, bn), lambda i, j: (i, j + nb)),
        ],
        out_specs=pl.BlockSpec((bm, bn), lambda i, j: (i, j)),
        out_shape=jax.ShapeDtypeStruct((m, n), MXU_DTYPE),
        compiler_params=_cparams("parallel", "arbitrary"),
        name="branch_merge",
    )(o_diff, o_mla, w_d, w_m, gates, gates)


def _residual_norm(y, x_ref, gp_ref, gn_ref, xo_ref, ho_ref):
    xn = x_ref[...] + _rms(y, gp_ref[...])
    xo_ref[...] = xn
    ho_ref[...] = _rms(xn, gn_ref[...]).astype(ho_ref.dtype)


def _out_proj_kernel(lhs_ref, w_ref, x_ref, gp_ref, gn_ref, xo_ref, ho_ref, *, row_tiles):
    rows = lhs_ref.shape[0] // row_tiles
    for t in range(row_tiles):
        r = pl.ds(t * rows, rows)
        y = jnp.dot(lhs_ref[r, :], w_ref[...], preferred_element_type=F32)
        _residual_norm(y, x_ref.at[r], gp_ref, gn_ref, xo_ref.at[r], ho_ref.at[r])


def _out_proj(lhs, w, x, g_post, g_next):
    m, kdim = lhs.shape
    n = w.shape[1]
    bm = _pick(m, 512, BF16_SUBLANES)
    row = lambda i: (i, 0)
    const = lambda i: (0, 0)
    return pl.pallas_call(
        functools.partial(_out_proj_kernel, row_tiles=2 if bm % (2 * BF16_SUBLANES) == 0 else 1),
        grid=(m // bm,),
        in_specs=[
            pl.BlockSpec((bm, kdim), row),
            pl.BlockSpec((kdim, n), const),
            pl.BlockSpec((bm, n), row),
            pl.BlockSpec((1, n), const),
            pl.BlockSpec((1, n), const),
        ],
        out_specs=[pl.BlockSpec((bm, n), row), pl.BlockSpec((bm, n), row)],
        out_shape=[jax.ShapeDtypeStruct((m, n), F32), jax.ShapeDtypeStruct((m, n), MXU_DTYPE)],
        compiler_params=_cparams("parallel"),
        name="out_proj",
    )(lhs, w, x, g_post[None, :], g_next[None, :])


def _mem_attn_kernel(q_ref, kv_ref, o_ref, *, dh):
    width = MEM_HEADS * dh
    outs = []
    for h in range(MEM_HEADS):
        q = q_ref[:, h * dh:(h + 1) * dh]
        k = kv_ref[:, h * dh:(h + 1) * dh]
        v = kv_ref[:, width + h * dh:width + (h + 1) * dh]
        s = lax.dot_general(q, k, _NT, preferred_element_type=F32)
        p = jnp.exp(s - s.max(axis=1, keepdims=True))
        l = p.sum(axis=1, keepdims=True)
        outs.append(jnp.dot(p.astype(MXU_DTYPE), v, preferred_element_type=F32) / l)
    o_ref[...] = jnp.concatenate(outs, axis=1).astype(o_ref.dtype)


def _mem_attention(q, kv, seq, mem_tokens):
    m, d = q.shape
    bq = _pick(seq, 512, BF16_SUBLANES)
    per_batch = seq // bq
    return pl.pallas_call(
        functools.partial(_mem_attn_kernel, dh=d // MEM_HEADS),
        grid=(m // bq,),
        in_specs=[pl.BlockSpec((bq, d), lambda i: (i, 0)),
                  pl.BlockSpec((mem_tokens, 2 * d), lambda i: (i // per_batch, 0))],
        out_specs=pl.BlockSpec((bq, d), lambda i: (i, 0)),
        out_shape=jax.ShapeDtypeStruct((m, d), MXU_DTYPE),
        compiler_params=_cparams("parallel"),
        name="mem_attention",
    )(q, kv)


HALO = BF16_SUBLANES


def _ffn_kernel(h_ref, halo_ref, wa_ref, wg_ref, cwa_ref, cwg_ref, cba_ref, cbg_ref, wd_ref, x_ref, gp_ref,
                gn_ref, xo_ref, ho_ref, hs_ref, u_ref, acc_ref, *, bm, blocks_per_seq, nb, sub, row_tiles):
    i = pl.program_id(0)
    j = pl.program_id(1)

    @pl.when(j == 0)
    def _():
        halo = halo_ref[...]
        first = (i % blocks_per_seq) == 0
        hs_ref[:HALO, :] = jnp.where(first, jnp.zeros_like(halo), halo)
        hs_ref[HALO:, :] = h_ref[...]
        acc_ref[...] = jnp.zeros_like(acc_ref)

    hs = hs_ref[...]

    tiles = [slice(t * sub, (t + 1) * sub) for t in range(wa_ref.shape[1] // sub)]
    for t, cols in enumerate(tiles):
        u_ref[2 * t] = jnp.dot(hs, wa_ref[:, cols], preferred_element_type=F32)
        u_ref[2 * t + 1] = jnp.dot(hs, wg_ref[:, cols], preferred_element_type=F32)

    def conv(slot, cols, row0, rows, cw_ref, cb_ref):
        cw = cw_ref[:, cols]
        c = cb_ref[:, cols]
        for tap in range(CONV_WIDTH):
            c = c + u_ref[slot, pl.ds(row0 + HALO - (CONV_WIDTH - 1) + tap, rows), :] * cw[tap:tap + 1]
        return c

    rows = bm // row_tiles
    for r in range(row_tiles):
        acts = []
        for t, cols in enumerate(tiles):
            a = conv(2 * t, cols, r * rows, rows, cwa_ref, cba_ref)
            g = conv(2 * t + 1, cols, r * rows, rows, cwg_ref, cbg_ref)
            acts.append((a * jax.nn.sigmoid(a) * g).astype(MXU_DTYPE))
        acc_ref[pl.ds(r * rows, rows), :] += jnp.dot(jnp.concatenate(acts, axis=1), wd_ref[...],
                                                     preferred_element_type=F32)

    @pl.when(j == nb - 1)
    def _():
        _residual_norm(acc_ref[...], x_ref, gp_ref, gn_ref, xo_ref, ho_ref)


def _ffn(h, w_up, conv_w, conv_b, w_down, x, g_post, g_next, seq):
    m, d = h.shape
    ff = w_up.shape[1] // 2
    bm = _pick(seq, 512, HALO)
    bn = _pick(ff, 512)
    sub = _pick(bn, MXU_COLS)
    nb = ff // bn
    halo_blocks = bm // HALO
    row = lambda i, j: (i, 0)
    const = lambda i, j: (0, 0)
    up_a = lambda i, j: (0, j)
    up_g = lambda i, j: (0, j + nb)
    return pl.pallas_call(
        functools.partial(_ffn_kernel, bm=bm, blocks_per_seq=seq // bm, nb=nb, sub=sub,
                          row_tiles=2 if bm % (2 * BF16_SUBLANES) == 0 else 1),
        grid=(m // bm, nb),
        in_specs=[
            pl.BlockSpec((bm, d), row),
            pl.BlockSpec((HALO, d), lambda i, j: (jnp.maximum(i * halo_blocks - 1, 0), 0)),
            pl.BlockSpec((d, bn), up_a),
            pl.BlockSpec((d, bn), up_g),
            pl.BlockSpec((CONV_WIDTH, bn), up_a),
            pl.BlockSpec((CONV_WIDTH, bn), up_g),
            pl.BlockSpec((1, bn), up_a),
            pl.BlockSpec((1, bn), up_g),
            pl.BlockSpec((bn, d), lambda i, j: (j, 0)),
            pl.BlockSpec((bm, d), row),
            pl.BlockSpec((1, d), const),
            pl.BlockSpec((1, d), const),
        ],
        out_specs=[pl.BlockSpec((bm, d), row), pl.BlockSpec((bm, d), row)],
        out_shape=[jax.ShapeDtypeStruct((m, d), F32), jax.ShapeDtypeStruct((m, d), MXU_DTYPE)],
        scratch_shapes=[pltpu.VMEM((HALO + bm, d), MXU_DTYPE),
                        pltpu.VMEM((2 * (bn // sub), HALO + bm, sub), F32),
                        pltpu.VMEM((bm, d), F32)],
        compiler_params=_cparams("parallel", "arbitrary"),
        name="ffn",
    )(h, h, w_up, w_up, conv_w, conv_w, conv_b[None, :], conv_b[None, :], w_down, x,
      g_post[None, :], g_next[None, :])


def kernel(x, mem, positions, g_pre_mix, w_in, b_gate, lam_q1, lam_k1, lam_q2, lam_k2, g_diff_sub, g_cq,
           w_uq, g_ckv, w_ukv, w_br_diff, w_br_mla, w_mix_out, g_post_mix, g_pre_x, g_mem, w_q_x, w_kv_x,
           w_o_x, g_post_x, g_pre_ffn, w_up, conv_w, conv_b, w_down, g_post_ffn):
    batch, seq, d = x.shape
    mem_tokens = mem.shape[1]
    depth = w_in.shape[0]
    m = batch * seq
    cast = lambda a: a.astype(MXU_DTYPE)

    qw = DIFF_HEADS * 2 * DIFF_QK_DIM
    o_v = 2 * qw
    o_cq = o_v + DIFF_WIDTH
    o_ckv = o_cq + MLA_Q_RANK
    o_kr = o_ckv + MLA_KV_RANK
    o_gt = o_kr + MLA_ROPE_DIM
    assert w_in.shape[2] == o_gt + 2 * d
    assert MLA_Q_RANK % MLA_KV_RANK == 0
    qhead = MLA_NOPE_DIM + MLA_ROPE_DIM
    kvhead = MLA_NOPE_DIM + MLA_V_DIM

    cos, sin = _rope_tables(positions)
    xf = x.reshape(m, d)
    memf = mem.reshape(batch * mem_tokens, d)
    h = _rmsnorm(xf, g_pre_mix[0])
    w_in_c = cast(w_in)

    for l in range(depth):
        lam_init = 0.8 - 0.6 * math.exp(-0.3 * l)
        wl = w_in_c[l]
        w_qk = cast(_pair_halves(
            jnp.concatenate([wl[:, :qw] * (DIFF_QK_DIM ** -0.5), wl[:, qw:o_v]], axis=1)))
        w_vt = cast(wl[:, o_v:o_cq].T)
        w_lat = cast(wl[:, o_cq:o_kr])
        w_kpe = cast(_pair_halves(jnp.pad(wl[:, o_kr:o_gt], ((0, 0), (0, LANES - MLA_ROPE_DIM)))))
        w_gate = cast(wl[:, o_gt:])
        w_uq_h = jnp.pad(w_uq[l].reshape(MLA_Q_RANK, MLA_HEADS, qhead),
                         ((0, 0), (0, 0), (0, 2 * LANES - qhead)))
        w_uq_p = cast(jnp.concatenate([w_uq_h[:, :, :MLA_NOPE_DIM], _pair_halves(w_uq_h[:, :, MLA_NOPE_DIM:])],
                                      axis=2).reshape(MLA_Q_RANK, -1))
        w_ukv_h = w_ukv[l].reshape(MLA_KV_RANK, MLA_HEADS, kvhead)
        w_uk = cast(w_ukv_h[:, :, :MLA_NOPE_DIM].reshape(MLA_KV_RANK, -1))
        w_uvt = cast(w_ukv_h[:, :, MLA_NOPE_DIM:].reshape(MLA_KV_RANK, -1).T)

        bn_qk = _pick(2 * qw, 512)
        qk = _proj(h, w_qk, functools.partial(_ep_rope, q_tiles=qw // bn_qk), name="proj_qk",
                   extras=(cos, sin), extra_specs=_rope_specs)
        vt = _proj_t(h, w_vt, name="proj_vt")
        lat = _proj(h, w_lat, _ep_plain, name="proj_latent", bn_pref=MLA_Q_RANK + MLA_KV_RANK)
        kpe = _proj(h, w_kpe, _ep_rope, name="proj_kpe", extras=(cos, sin), extra_specs=_rope_specs)
        gates = _proj(h, w_gate, _ep_sigmoid_bias, name="proj_gates", extras=(b_gate[l][None, :],),
                      extra_specs=_bias_specs)
        q_mla = _latent_q(lat, 0, g_cq[l], w_uq_p, cos, sin, qhead ** -0.5 * LOG2E)
        k_mla, vt_mla = _latent_kv(lat, MLA_Q_RANK // MLA_KV_RANK, g_ckv[l], w_uk, w_uvt)
        lam_vecs = jnp.stack([lam_q1[l], lam_k1[l], lam_q2[l], lam_k2[l]])
        o_diff = _diff_attention(qk, vt, lam_vecs, g_diff_sub[l], lam_init, batch, seq)
        o_mla = _mla_attention(q_mla, k_mla, kpe, vt_mla, batch, seq)
        merged = _merge(o_diff, o_mla, cast(w_br_diff[l]), cast(w_br_mla[l]), gates)
        xf, h = _out_proj(merged, cast(w_mix_out[l]), xf, g_post_mix[l], g_pre_x[l])

        q_x = _proj(h, cast(w_q_x[l]), functools.partial(_ep_scale, scale=(d // MEM_HEADS) ** -0.5),
                    name="proj_qx")
        kv_x = _proj(_rmsnorm(memf, g_mem[l]), cast(w_kv_x[l]), _ep_plain, name="proj_kvx")
        o_x = _mem_attention(q_x, kv_x, seq, mem_tokens)
        xf, h = _out_proj(o_x, cast(w_o_x[l]), xf, g_post_x[l], g_pre_ffn[l])

        g_next = g_pre_mix[l + 1] if l + 1 < depth else g_pre_mix[0]
        xf, h = _ffn(h, cast(w_up[l]), conv_w[l], conv_b[l], cast(w_down[l]), xf, g_post_ffn[l], g_next, seq)

    return xf.reshape(batch, seq, d)
```

```python
import functools
import math

import jax
import jax.numpy as jnp
from jax import lax
from jax.experimental import pallas as pl
from jax.experimental.pallas import tpu as pltpu

CHUNK = 64
ROPE_THETA = 10000.0
EPS = 1e-6
DIFF_HEADS = 8
DIFF_QK_DIM = 64
DIFF_V_DIM = 128
MLA_HEADS = 8
MLA_Q_RANK = 512
MLA_KV_RANK = 256
MLA_NOPE_DIM = 128
MLA_ROPE_DIM = 64
MLA_V_DIM = 128
MEM_HEADS = 4
CONV_WIDTH = 3
DIFF_WIDTH = DIFF_HEADS * DIFF_V_DIM
MLA_WIDTH = MLA_HEADS * MLA_V_DIM

LANES = 128
BF16_SUBLANES = 16
MXU_COLS = 256
VMEM_LIMIT_BYTES = 56 * 2**20

MXU_DTYPE = jnp.bfloat16
F32 = jnp.float32
LOG2E = math.log2(math.e)


def _pick(n, pref, mult=LANES):
    if n <= pref:
        return n
    best = None
    for d in range(mult, pref + 1, mult):
        if n % d == 0:
            best = d
    assert best is not None, (n, pref, mult)
    return best


def _cparams(*sem):
    return pltpu.CompilerParams(dimension_semantics=sem, vmem_limit_bytes=VMEM_LIMIT_BYTES)


def _rms(x, g):
    return x * lax.rsqrt(jnp.mean(x * x, axis=-1, keepdims=True) + EPS) * g


def _rope128(t, cos, sin):
    return t * cos + pltpu.roll(t, LANES // 2, 1) * sin


def _pair_halves(w):
    q = LANES // 4
    shape = w.shape
    w = w.reshape(*shape[:-1], shape[-1] // LANES, 2, 2, q)
    return jnp.swapaxes(w, -3, -2).reshape(shape)


_NT = (((1,), (1,)), ((), ()))


def _tables_kernel(pos_ref, inv_ref, sgn_ref, cos_ref, sin_ref):
    ang = pos_ref[...].astype(F32) * inv_ref[...]
    cos_ref[...] = jnp.cos(ang)
    sin_ref[...] = jnp.sin(ang) * sgn_ref[...]


def _rope_tables(positions):
    m = positions.size
    d = DIFF_QK_DIM
    inv = ROPE_THETA ** (-jnp.arange(0, d, 2, dtype=F32) / d)
    inv128 = jnp.tile(inv, LANES // (d // 2))[None, :]
    sgn128 = jnp.concatenate([-jnp.ones((LANES // 2,), F32), jnp.ones((LANES // 2,), F32)])[None, :]
    bm = _pick(m, 2048, 8)
    row = pl.BlockSpec((bm, LANES), lambda i: (i, 0))
    const = pl.BlockSpec((1, LANES), lambda i: (0, 0))
    return pl.pallas_call(
        _tables_kernel,
        grid=(m // bm,),
        in_specs=[pl.BlockSpec((bm, 1), lambda i: (i, 0)), const, const],
        out_specs=[row, row],
        out_shape=[jax.ShapeDtypeStruct((m, LANES), F32)] * 2,
        compiler_params=_cparams("parallel"),
        name="rope_tables",
    )(positions.reshape(m, 1), inv128, sgn128)


def _rmsnorm_kernel(x_ref, g_ref, o_ref):
    o_ref[...] = _rms(x_ref[...], g_ref[...]).astype(o_ref.dtype)


def _rmsnorm(x, g):
    m, d = x.shape
    bm = _pick(m, 512, 8)
    return pl.pallas_call(
        _rmsnorm_kernel,
        grid=(m // bm,),
        in_specs=[pl.BlockSpec((bm, d), lambda i: (i, 0)), pl.BlockSpec((1, d), lambda i: (0, 0))],
        out_specs=pl.BlockSpec((bm, d), lambda i: (i, 0)),
        out_shape=jax.ShapeDtypeStruct((m, d), MXU_DTYPE),
        compiler_params=_cparams("parallel"),
        name="rmsnorm",
    )(x, g[None, :])


PROJ_ROW_TILE = 256


def _proj_kernel(h_ref, w_ref, *rest, epilogue, sub, rows):
    *extra, o_ref = rest
    for r in range(h_ref.shape[0] // rows):
        rs = pl.ds(r * rows, rows)
        h = h_ref[rs, :]
        for t in range(w_ref.shape[1] // sub):
            cols = slice(t * sub, (t + 1) * sub)
            acc = jnp.dot(h, w_ref[:, cols], preferred_element_type=F32)
            o_ref[rs, cols] = epilogue(acc, rs, cols, *extra).astype(o_ref.dtype)


def _ep_plain(acc, rs, cols):
    return acc


def _ep_scale(acc, rs, cols, *, scale):
    return acc * scale


def _ep_rope(acc, rs, cols, cos_ref, sin_ref, *, q_tiles=0, rope_from=0):
    if cols.start < rope_from:
        return acc
    cos, sin = cos_ref[rs, :], sin_ref[rs, :]
    n = acc.shape[1] // LANES
    out = jnp.concatenate(
        [_rope128(acc[:, c * LANES:(c + 1) * LANES], cos, sin) for c in range(n)], axis=1)
    if q_tiles:
        out = out * jnp.where(pl.program_id(1) < q_tiles, LOG2E, 1.0)
    return out


def _proj(h, w, epilogue, *, name, extras=(), extra_specs=None, bm_pref=1024, bn_pref=512):
    m, k = h.shape
    n = w.shape[1]
    bm = _pick(m, bm_pref, BF16_SUBLANES)
    bn = _pick(n, bn_pref)
    specs = [pl.BlockSpec((bm, k), lambda i, j: (i, 0)), pl.BlockSpec((k, bn), lambda i, j: (0, j))]
    specs += list(extra_specs(bm, bn)) if extra_specs else []
    return pl.pallas_call(
        functools.partial(_proj_kernel, epilogue=epilogue, sub=_pick(bn, MXU_COLS),
                          rows=_pick(bm, PROJ_ROW_TILE, BF16_SUBLANES)),
        grid=(m // bm, n // bn),
        in_specs=specs,
        out_specs=pl.BlockSpec((bm, bn), lambda i, j: (i, j)),
        out_shape=jax.ShapeDtypeStruct((m, n), MXU_DTYPE),
        compiler_params=_cparams("parallel", "arbitrary"),
        name=name,
    )(h, w, *extras)


def _rope_specs(bm, bn):
    tab = pl.BlockSpec((bm, LANES), lambda i, j: (i, 0))
    return [tab, tab]


def _proj_t_kernel(h_ref, wt_ref, o_ref):
    o_ref[...] = lax.dot_general(wt_ref[...], h_ref[...], _NT,
                                 preferred_element_type=F32).astype(o_ref.dtype)


def _proj_t(h, wt, *, name):
    m, k = h.shape
    n = wt.shape[0]
    bm = _pick(m, 1024)
    bn = _pick(n, 512, BF16_SUBLANES)
    return pl.pallas_call(
        _proj_t_kernel,
        grid=(m // bm, n // bn),
        in_specs=[pl.BlockSpec((bm, k), lambda i, j: (i, 0)), pl.BlockSpec((bn, k), lambda i, j: (j, 0))],
        out_specs=pl.BlockSpec((bn, bm), lambda i, j: (j, i)),
        out_shape=jax.ShapeDtypeStruct((n, m), MXU_DTYPE),
        compiler_params=_cparams("parallel", "arbitrary"),
        name=name,
    )(h, wt)


def _latent_q_kernel(c_ref, g_ref, w_ref, cos_ref, sin_ref, o_ref, *, scale):
    cn = _rms(c_ref[...].astype(F32), g_ref[...]).astype(MXU_DTYPE)
    acc = jnp.dot(cn, w_ref[...], preferred_element_type=F32)
    cos, sin = cos_ref[...], sin_ref[...]
    cols = []
    for c in range(acc.shape[1] // LANES):
        t = acc[:, c * LANES:(c + 1) * LANES]
        cols.append(_rope128(t, cos, sin) if c % 2 else t)
    o_ref[...] = (jnp.concatenate(cols, axis=1) * scale).astype(o_ref.dtype)


def _latent_kv_kernel(c_ref, g_ref, wk_ref, wvt_ref, k_ref, vt_ref):
    cn = _rms(c_ref[...].astype(F32), g_ref[...]).astype(MXU_DTYPE)
    k_ref[...] = jnp.dot(cn, wk_ref[...], preferred_element_type=F32).astype(k_ref.dtype)
    vt_ref[...] = lax.dot_general(wvt_ref[...], cn, _NT, preferred_element_type=F32).astype(vt_ref.dtype)


def _latent_q(src, col_block, g, w, cos, sin, scale):
    m = src.shape[0]
    rank, n = w.shape
    bm = _pick(m, 512, BF16_SUBLANES)
    tab = pl.BlockSpec((bm, LANES), lambda i: (i, 0))
    return pl.pallas_call(
        functools.partial(_latent_q_kernel, scale=scale),
        grid=(m // bm,),
        in_specs=[pl.BlockSpec((bm, rank), lambda i: (i, col_block)),
                  pl.BlockSpec((1, rank), lambda i: (0, 0)),
                  pl.BlockSpec((rank, n), lambda i: (0, 0)), tab, tab],
        out_specs=pl.BlockSpec((bm, n), lambda i: (i, 0)),
        out_shape=jax.ShapeDtypeStruct((m, n), MXU_DTYPE),
        compiler_params=_cparams("parallel"),
        name="mla_q",
    )(src, g[None, :], w, cos, sin)


def _latent_kv(src, col_block, g, wk, wvt):
    m = src.shape[0]
    rank, nk = wk.shape
    nv = wvt.shape[0]
    bm = _pick(m, 512)
    return pl.pallas_call(
        _latent_kv_kernel,
        grid=(m // bm,),
        in_specs=[pl.BlockSpec((bm, rank), lambda i: (i, col_block)),
                  pl.BlockSpec((1, rank), lambda i: (0, 0)),
                  pl.BlockSpec((rank, nk), lambda i: (0, 0)),
                  pl.BlockSpec((nv, rank), lambda i: (0, 0))],
        out_specs=[pl.BlockSpec((bm, nk), lambda i: (i, 0)), pl.BlockSpec((nv, bm), lambda i: (0, i))],
        out_shape=[jax.ShapeDtypeStruct((m, nk), MXU_DTYPE), jax.ShapeDtypeStruct((nv, m), MXU_DTYPE)],
        compiler_params=_cparams("parallel"),
        name="mla_kv",
    )(src, g[None, :], wk, wvt)


def _flash(q_parts, load_k, load_vt, qi, *, bq, bk, dv, unroll):
    diag = bq // bk
    assert unroll % diag == 0
    rel = (lax.broadcasted_iota(jnp.int32, (bk, bq), 0) // CHUNK
           - lax.broadcasted_iota(jnp.int32, (bk, bq), 1) // CHUNK)

    def group(jg, carries, count, masked):
        blocks = [jg * unroll + u for u in range(count)]
        scores = [[lax.dot_general(load_k(j, part), q, _NT, preferred_element_type=F32)
                   for part, q in enumerate(q_parts)] for j in blocks]
        carries = list(carries)
        for u, j in enumerate(blocks):
            d = u - (count - masked)
            for part, s in enumerate(scores[u]):
                m, l, acc = carries[part]
                if d >= 0:
                    s = jnp.where(rel <= -d * (bk // CHUNK), s, -jnp.inf)
                m_new = jnp.maximum(m, s.max(axis=0, keepdims=True))
                alpha = jnp.exp2(m - m_new)
                p = jnp.exp2(s - m_new)
                l = alpha * l + p.sum(axis=0, keepdims=True)
                acc = alpha * acc + jnp.dot(load_vt(j, part), p.astype(MXU_DTYPE),
                                            preferred_element_type=F32)
                carries[part] = (m_new, l, acc)
        return tuple(carries)

    init = (jnp.full((1, bq), -jnp.inf, F32), jnp.zeros((1, bq), F32), jnp.zeros((dv, bq), F32))
    carries = tuple(init for _ in q_parts)
    n_blocks = (qi + 1) * diag
    n_groups = (n_blocks + unroll - 1) // unroll
    carries = lax.fori_loop(0, n_groups - 1, lambda jg, c: group(jg, c, unroll, 0), carries)
    last = (n_blocks - (n_groups - 1) * unroll) // diag - 1
    tails = [functools.partial(group, count=(t + 1) * diag, masked=diag) for t in range(unroll // diag)]
    carries = lax.switch(last, [lambda c, f=f: f(n_groups - 1, c) for f in tails], carries)
    return [(acc, l) for _, l, acc in carries]


KV_GROUP_KEYS = 1024


def _kv_group(seq, bq, bk):
    group = max(bq // bk, min(KV_GROUP_KEYS // bk, seq // bk))
    assert (seq // bk) % group == 0 and group % (bq // bk) == 0
    return group


DIFF_HEADS_PER_STEP = 2


def _diff_attn_kernel(lam_ref, q_ref, k_ref, vt_ref, g_ref, o_ref, *, bq, bk, lam_init, unroll):
    qi = pl.program_id(2)
    lam_v = lam_ref[...]
    lam = (jnp.exp(jnp.sum(lam_v[0:1] * lam_v[1:2], axis=1, keepdims=True))
           - jnp.exp(jnp.sum(lam_v[2:3] * lam_v[3:4], axis=1, keepdims=True)) + lam_init)
    hw = LANES
    is_q1 = (lax.broadcasted_iota(jnp.int32, (bq, hw), 1) & (DIFF_QK_DIM // 2)) == 0
    q_parts = []
    for hh in range(DIFF_HEADS_PER_STEP):
        q = q_ref[:, hh * hw:(hh + 1) * hw]
        zero = jnp.zeros_like(q)
        q_parts += [jnp.where(is_q1, q, zero), jnp.where(is_q1, zero, q)]

    def load_k(j, part):
        hh = part // 2
        return k_ref[pl.ds(pl.multiple_of(j * bk, bk), bk), hh * hw:(hh + 1) * hw]

    def load_vt(j, part):
        hh = part // 2
        return vt_ref[hh * DIFF_V_DIM:(hh + 1) * DIFF_V_DIM, pl.ds(pl.multiple_of(j * bk, bk), bk)]

    outs = _flash(q_parts, load_k, load_vt, qi, bq=bq, bk=bk, dv=DIFF_V_DIM, unroll=unroll)
    for hh in range(DIFF_HEADS_PER_STEP):
        (acc1, l1), (acc2, l2) = outs[2 * hh], outs[2 * hh + 1]
        o = acc1 * (1.0 / l1) - lam * (acc2 * (1.0 / l2))
        o = o * lax.rsqrt(jnp.mean(o * o, axis=0, keepdims=True) + EPS) * g_ref[...]
        o_ref[:, hh * DIFF_V_DIM:(hh + 1) * DIFF_V_DIM] = (o * (1.0 - lam_init)).T.astype(o_ref.dtype)


def _diff_attention(qk, vt, lam_vecs, g_sub, lam_init, batch, seq):
    m = qk.shape[0]
    bq = _pick(seq, 512, CHUNK)
    bk = _pick(bq, 512, CHUNK)
    nq = seq // bq
    hs = DIFF_HEADS_PER_STEP
    steps = DIFF_HEADS // hs
    unroll = _kv_group(seq, bq, bk)
    return pl.pallas_call(
        functools.partial(_diff_attn_kernel, bq=bq, bk=bk, lam_init=lam_init, unroll=unroll),
        grid=(batch, steps, nq),
        in_specs=[
            pl.BlockSpec((4, DIFF_QK_DIM), lambda b, h, i: (0, 0)),
            pl.BlockSpec((bq, hs * LANES), lambda b, h, i: (b * nq + i, h)),
            pl.BlockSpec((seq, hs * LANES), lambda b, h, i: (b, steps + h)),
            pl.BlockSpec((hs * DIFF_V_DIM, seq), lambda b, h, i: (h, b)),
            pl.BlockSpec((DIFF_V_DIM, 1), lambda b, h, i: (0, 0)),
        ],
        out_specs=pl.BlockSpec((bq, hs * DIFF_V_DIM), lambda b, h, i: (b * nq + i, h)),
        out_shape=jax.ShapeDtypeStruct((m, DIFF_WIDTH), MXU_DTYPE),
        compiler_params=_cparams("parallel", "parallel", "arbitrary"),
        name="diff_attention",
    )(lam_vecs, qk, qk, vt, g_sub[:, None])


MLA_HEADS_PER_STEP = 4


def _mla_attn_kernel(q_ref, kn_ref, kpe_ref, vt_ref, o_ref, kcat_ref, *, bq, bk, unroll):
    qi = pl.program_id(2)
    qw = 2 * LANES

    @pl.when(qi == 0)
    def _():
        for hh in range(MLA_HEADS_PER_STEP):
            kcat_ref[hh, :, :MLA_NOPE_DIM] = kn_ref[:, hh * MLA_NOPE_DIM:(hh + 1) * MLA_NOPE_DIM]
            kcat_ref[hh, :, MLA_NOPE_DIM:] = kpe_ref[...]

    def load_k(j, part):
        return kcat_ref[part, pl.ds(pl.multiple_of(j * bk, bk), bk), :]

    def load_vt(j, part):
        return vt_ref[part * MLA_V_DIM:(part + 1) * MLA_V_DIM, pl.ds(pl.multiple_of(j * bk, bk), bk)]

    q_parts = [q_ref[:, hh * qw:(hh + 1) * qw] for hh in range(MLA_HEADS_PER_STEP)]
    outs = _flash(q_parts, load_k, load_vt, qi, bq=bq, bk=bk, dv=MLA_V_DIM, unroll=unroll)
    for hh, (acc, l) in enumerate(outs):
        o_ref[:, hh * MLA_V_DIM:(hh + 1) * MLA_V_DIM] = (acc * (1.0 / l)).T.astype(o_ref.dtype)


def _mla_attention(q, k_nope, kpe_src, kpe_block, vt, batch, seq):
    m = q.shape[0]
    bq = _pick(seq, 512, CHUNK)
    bk = _pick(bq, 512, CHUNK)
    nq = seq // bq
    hs = MLA_HEADS_PER_STEP
    qw = 2 * LANES
    unroll = _kv_group(seq, bq, bk)
    return pl.pallas_call(
        functools.partial(_mla_attn_kernel, bq=bq, bk=bk, unroll=unroll),
        grid=(batch, MLA_HEADS // hs, nq),
        in_specs=[
            pl.BlockSpec((bq, hs * qw), lambda b, h, i: (b * nq + i, h)),
            pl.BlockSpec((seq, hs * MLA_NOPE_DIM), lambda b, h, i: (b, h)),
            pl.BlockSpec((seq, LANES), lambda b, h, i: (b, kpe_block)),
            pl.BlockSpec((hs * MLA_V_DIM, seq), lambda b, h, i: (h, b)),
        ],
        out_specs=pl.BlockSpec((bq, hs * MLA_V_DIM), lambda b, h, i: (b * nq + i, h)),
        out_shape=jax.ShapeDtypeStruct((m, MLA_WIDTH), MXU_DTYPE),
        scratch_shapes=[pltpu.VMEM((hs, seq, qw), MXU_DTYPE)],
        compiler_params=_cparams("parallel", "parallel", "arbitrary"),
        name="mla_attention",
    )(q, k_nope, kpe_src, vt)


def _merge_kernel(h_ref, od_ref, om_ref, wga_ref, wgb_ref, ba_ref, bb_ref, wd_ref, wm_ref, o_ref, *, sub, rows):
    for r in range(h_ref.shape[0] // rows):
        rs = pl.ds(r * rows, rows)
        h, od, om = h_ref[rs, :], od_ref[rs, :], om_ref[rs, :]
        for t in range(wd_ref.shape[1] // sub):
            cols = slice(t * sub, (t + 1) * sub)
            ga = jax.nn.sigmoid(jnp.dot(h, wga_ref[:, cols], preferred_element_type=F32) + ba_ref[:, cols])
            gb = jax.nn.sigmoid(jnp.dot(h, wgb_ref[:, cols], preferred_element_type=F32) + bb_ref[:, cols])
            a = jnp.dot(od, wd_ref[:, cols], preferred_element_type=F32)
            b = jnp.dot(om, wm_ref[:, cols], preferred_element_type=F32)
            o_ref[rs, cols] = (ga * a + gb * b).astype(o_ref.dtype)


def _merge(h, o_diff, o_mla, w_gate, b_gate, w_d, w_m):
    m, d = h.shape
    n = w_d.shape[1]
    bm = _pick(m, 1024, BF16_SUBLANES)
    bn = _pick(n, 512)
    nb = n // bn
    lo = lambda i, j: (0, j)
    hi = lambda i, j: (0, j + nb)
    row = lambda i, j: (i, 0)
    return pl.pallas_call(
        functools.partial(_merge_kernel, sub=_pick(bn, MXU_COLS), rows=_pick(bm, PROJ_ROW_TILE, BF16_SUBLANES)),
        grid=(m // bm, nb),
        in_specs=[
            pl.BlockSpec((bm, d), row),
            pl.BlockSpec((bm, DIFF_WIDTH), row),
            pl.BlockSpec((bm, MLA_WIDTH), row),
            pl.BlockSpec((d, bn), lo),
            pl.BlockSpec((d, bn), hi),
            pl.BlockSpec((1, bn), lo),
            pl.BlockSpec((1, bn), hi),
            pl.BlockSpec((DIFF_WIDTH, bn), lo),
            pl.BlockSpec((MLA_WIDTH, bn), lo),
        ],
        out_specs=pl.BlockSpec((bm, bn), lambda i, j: (i, j)),
        out_shape=jax.ShapeDtypeStruct((m, n), MXU_DTYPE),
        compiler_params=_cparams("parallel", "arbitrary"),
        name="branch_merge",
    )(h, o_diff, o_mla, w_gate, w_gate, b_gate[None, :], b_gate[None, :], w_d, w_m)


def _residual_norm(y, x_ref, gp_ref, gn_ref, xo_ref, ho_ref):
    xn = x_ref[...] + _rms(y, gp_ref[...])
    xo_ref[...] = xn
    ho_ref[...] = _rms(xn, gn_ref[...]).astype(ho_ref.dtype)


def _out_proj_kernel(lhs_ref, w_ref, x_ref, gp_ref, gn_ref, xo_ref, ho_ref, *, row_tiles):
    rows = lhs_ref.shape[0] // row_tiles
    for t in range(row_tiles):
        r = pl.ds(t * rows, rows)
        y = jnp.dot(lhs_ref[r, :], w_ref[...], preferred_element_type=F32)
        _residual_norm(y, x_ref.at[r], gp_ref, gn_ref, xo_ref.at[r], ho_ref.at[r])


def _out_proj(lhs, w, x, g_post, g_next):
    m, kdim = lhs.shape
    n = w.shape[1]
    bm = _pick(m, 512, BF16_SUBLANES)
    row = lambda i: (i, 0)
    const = lambda i: (0, 0)
    return pl.pallas_call(
        functools.partial(_out_proj_kernel, row_tiles=2 if bm % (2 * BF16_SUBLANES) == 0 else 1),
        grid=(m // bm,),
        in_specs=[
            pl.BlockSpec((bm, kdim), row),
            pl.BlockSpec((kdim, n), const),
            pl.BlockSpec((bm, n), row),
            pl.BlockSpec((1, n), const),
            pl.BlockSpec((1, n), const),
        ],
        out_specs=[pl.BlockSpec((bm, n), row), pl.BlockSpec((bm, n), row)],
        out_shape=[jax.ShapeDtypeStruct((m, n), F32), jax.ShapeDtypeStruct((m, n), MXU_DTYPE)],
        compiler_params=_cparams("parallel"),
        name="out_proj",
    )(lhs, w, x, g_post[None, :], g_next[None, :])


def _mem_attn_kernel(q_ref, kv_ref, o_ref, *, dh):
    width = MEM_HEADS * dh
    outs = []
    for h in range(MEM_HEADS):
        q = q_ref[:, h * dh:(h + 1) * dh]
        k = kv_ref[:, h * dh:(h + 1) * dh]
        v = kv_ref[:, width + h * dh:width + (h + 1) * dh]
        s = lax.dot_general(q, k, _NT, preferred_element_type=F32)
        p = jnp.exp(s - s.max(axis=1, keepdims=True))
        l = p.sum(axis=1, keepdims=True)
        outs.append(jnp.dot(p.astype(MXU_DTYPE), v, preferred_element_type=F32) / l)
    o_ref[...] = jnp.concatenate(outs, axis=1).astype(o_ref.dtype)


def _mem_attention(q, kv, seq, mem_tokens):
    m, d = q.shape
    bq = _pick(seq, 512, BF16_SUBLANES)
    per_batch = seq // bq
    return pl.pallas_call(
        functools.partial(_mem_attn_kernel, dh=d // MEM_HEADS),
        grid=(m // bq,),
        in_specs=[pl.BlockSpec((bq, d), lambda i: (i, 0)),
                  pl.BlockSpec((mem_tokens, 2 * d), lambda i: (i // per_batch, 0))],
        out_specs=pl.BlockSpec((bq, d), lambda i: (i, 0)),
        out_shape=jax.ShapeDtypeStruct((m, d), MXU_DTYPE),
        compiler_params=_cparams("parallel"),
        name="mem_attention",
    )(q, kv)


HALO = BF16_SUBLANES


def _ffn_kernel(h_ref, halo_ref, wa_ref, wg_ref, cwa_ref, cwg_ref, cba_ref, cbg_ref, wd_ref, x_ref, gp_ref,
                gn_ref, xo_ref, ho_ref, hs_ref, u_ref, acc_ref, *, bm, blocks_per_seq, nb, sub, row_tiles):
    i = pl.program_id(0)
    j = pl.program_id(1)

    @pl.when(j == 0)
    def _():
        halo = halo_ref[...]
        first = (i % blocks_per_seq) == 0
        hs_ref[:HALO, :] = jnp.where(first, jnp.zeros_like(halo), halo)
        hs_ref[HALO:, :] = h_ref[...]
        acc_ref[...] = jnp.zeros_like(acc_ref)

    hs = hs_ref[...]

    tiles = [slice(t * sub, (t + 1) * sub) for t in range(wa_ref.shape[1] // sub)]
    for t, cols in enumerate(tiles):
        u_ref[2 * t] = jnp.dot(hs, wa_ref[:, cols], preferred_element_type=F32)
        u_ref[2 * t + 1] = jnp.dot(hs, wg_ref[:, cols], preferred_element_type=F32)

    def conv(slot, cols, row0, rows, cw_ref, cb_ref):
        cw = cw_ref[:, cols]
        c = cb_ref[:, cols]
        for tap in range(CONV_WIDTH):
            c = c + u_ref[slot, pl.ds(row0 + HALO - (CONV_WIDTH - 1) + tap, rows), :] * cw[tap:tap + 1]
        return c

    rows = bm // row_tiles
    for r in range(row_tiles):
        acts = []
        for t, cols in enumerate(tiles):
            a = conv(2 * t, cols, r * rows, rows, cwa_ref, cba_ref)
            g = conv(2 * t + 1, cols, r * rows, rows, cwg_ref, cbg_ref)
            acts.append((a * jax.nn.sigmoid(a) * g).astype(MXU_DTYPE))
        acc_ref[pl.ds(r * rows, rows), :] += jnp.dot(jnp.concatenate(acts, axis=1), wd_ref[...],
                                                     preferred_element_type=F32)

    @pl.when(j == nb - 1)
    def _():
        _residual_norm(acc_ref[...], x_ref, gp_ref, gn_ref, xo_ref, ho_ref)


def _ffn(h, w_up, conv_w, conv_b, w_down, x, g_post, g_next, seq):
    m, d = h.shape
    ff = w_up.shape[1] // 2
    bm = _pick(seq, 512, HALO)
    bn = _pick(ff, 512)
    sub = _pick(bn, MXU_COLS)
    nb = ff // bn
    halo_blocks = bm // HALO
    row = lambda i, j: (i, 0)
    const = lambda i, j: (0, 0)
    up_a = lambda i, j: (0, j)
    up_g = lambda i, j: (0, j + nb)
    return pl.pallas_call(
        functools.partial(_ffn_kernel, bm=bm, blocks_per_seq=seq // bm, nb=nb, sub=sub,
                          row_tiles=2 if bm % (2 * BF16_SUBLANES) == 0 else 1),
        grid=(m // bm, nb),
        in_specs=[
            pl.BlockSpec((bm, d), row),
            pl.BlockSpec((HALO, d), lambda i, j: (jnp.maximum(i * halo_blocks - 1, 0), 0)),
            pl.BlockSpec((d, bn), up_a),
            pl.BlockSpec((d, bn), up_g),
            pl.BlockSpec((CONV_WIDTH, bn), up_a),
            pl.BlockSpec((CONV_WIDTH, bn), up_g),
            pl.BlockSpec((1, bn), up_a),
            pl.BlockSpec((1, bn), up_g),
            pl.BlockSpec((bn, d), lambda i, j: (j, 0)),
            pl.BlockSpec((bm, d), row),
            pl.BlockSpec((1, d), const),
            pl.BlockSpec((1, d), const),
        ],
        out_specs=[pl.BlockSpec((bm, d), row), pl.BlockSpec((bm, d), row)],
        out_shape=[jax.ShapeDtypeStruct((m, d), F32), jax.ShapeDtypeStruct((m, d), MXU_DTYPE)],
        scratch_shapes=[pltpu.VMEM((HALO + bm, d), MXU_DTYPE),
                        pltpu.VMEM((2 * (bn // sub), HALO + bm, sub), F32),
                        pltpu.VMEM((bm, d), F32)],
        compiler_params=_cparams("parallel", "arbitrary"),
        name="ffn",
    )(h, h, w_up, w_up, conv_w, conv_w, conv_b[None, :], conv_b[None, :], w_down, x,
      g_post[None, :], g_next[None, :])


def kernel(x, mem, positions, g_pre_mix, w_in, b_gate, lam_q1, lam_k1, lam_q2, lam_k2, g_diff_sub, g_cq,
           w_uq, g_ckv, w_ukv, w_br_diff, w_br_mla, w_mix_out, g_post_mix, g_pre_x, g_mem, w_q_x, w_kv_x,
           w_o_x, g_post_x, g_pre_ffn, w_up, conv_w, conv_b, w_down, g_post_ffn):
    batch, seq, d = x.shape
    mem_tokens = mem.shape[1]
    depth = w_in.shape[0]
    m = batch * seq
    cast = lambda a: a.astype(MXU_DTYPE)

    qw = DIFF_HEADS * 2 * DIFF_QK_DIM
    o_v = 2 * qw
    o_cq = o_v + DIFF_WIDTH
    o_ckv = o_cq + MLA_Q_RANK
    o_kr = o_ckv + MLA_KV_RANK
    o_gt = o_kr + MLA_ROPE_DIM
    assert w_in.shape[2] == o_gt + 2 * d
    assert MLA_Q_RANK % MLA_KV_RANK == 0
    qhead = MLA_NOPE_DIM + MLA_ROPE_DIM
    lat_width = -(-(o_kr - o_cq + LANES) // MXU_COLS) * MXU_COLS
    kvhead = MLA_NOPE_DIM + MLA_V_DIM

    cos, sin = _rope_tables(positions)
    xf = x.reshape(m, d)
    memf = mem.reshape(batch * mem_tokens, d)
    h = _rmsnorm(xf, g_pre_mix[0])
    w_in_c = cast(w_in)

    for l in range(depth):
        lam_init = 0.8 - 0.6 * math.exp(-0.3 * l)
        wl = w_in_c[l]
        w_qk = cast(_pair_halves(
            jnp.concatenate([wl[:, :qw] * (DIFF_QK_DIM ** -0.5), wl[:, qw:o_v]], axis=1)))
        w_vt = cast(wl[:, o_v:o_cq].T)
        w_kpe = _pair_halves(jnp.pad(wl[:, o_kr:o_gt], ((0, 0), (0, LANES - MLA_ROPE_DIM))))
        w_lat = jnp.pad(jnp.concatenate([wl[:, o_cq:o_kr], w_kpe], axis=1),
                        ((0, 0), (0, lat_width - (o_kr - o_cq) - LANES)))
        w_gate = wl[:, o_gt:]
        w_uq_h = jnp.pad(w_uq[l].reshape(MLA_Q_RANK, MLA_HEADS, qhead),
                         ((0, 0), (0, 0), (0, 2 * LANES - qhead)))
        w_uq_p = cast(jnp.concatenate([w_uq_h[:, :, :MLA_NOPE_DIM], _pair_halves(w_uq_h[:, :, MLA_NOPE_DIM:])],
                                      axis=2).reshape(MLA_Q_RANK, -1))
        w_ukv_h = w_ukv[l].reshape(MLA_KV_RANK, MLA_HEADS, kvhead)
        w_uk = cast(w_ukv_h[:, :, :MLA_NOPE_DIM].reshape(MLA_KV_RANK, -1))
        w_uvt = cast(w_ukv_h[:, :, MLA_NOPE_DIM:].reshape(MLA_KV_RANK, -1).T)

        bn_qk = _pick(2 * qw, 512)
        qk = _proj(h, w_qk, functools.partial(_ep_rope, q_tiles=qw // bn_qk), name="proj_qk",
                   extras=(cos, sin), extra_specs=_rope_specs)
        vt = _proj_t(h, w_vt, name="proj_vt")
        lat = _proj(h, w_lat, functools.partial(_ep_rope, rope_from=o_kr - o_cq), name="proj_latent",
                    extras=(cos, sin), extra_specs=_rope_specs, bn_pref=lat_width)
        q_mla = _latent_q(lat, 0, g_cq[l], w_uq_p, cos, sin, qhead ** -0.5 * LOG2E)
        k_mla, vt_mla = _latent_kv(lat, MLA_Q_RANK // MLA_KV_RANK, g_ckv[l], w_uk, w_uvt)
        lam_vecs = jnp.stack([lam_q1[l], lam_k1[l], lam_q2[l], lam_k2[l]])
        o_diff = _diff_attention(qk, vt, lam_vecs, g_diff_sub[l], lam_init, batch, seq)
        o_mla = _mla_attention(q_mla, k_mla, lat, (o_kr - o_cq) // LANES, vt_mla, batch, seq)
        merged = _merge(h, o_diff, o_mla, w_gate, b_gate[l], cast(w_br_diff[l]), cast(w_br_mla[l]))
        xf, h = _out_proj(merged, cast(w_mix_out[l]), xf, g_post_mix[l], g_pre_x[l])

        q_x = _proj(h, cast(w_q_x[l]), functools.partial(_ep_scale, scale=(d // MEM_HEADS) ** -0.5),
                    name="proj_qx")
        kv_x = _proj(_rmsnorm(memf, g_mem[l]), cast(w_kv_x[l]), _ep_plain, name="proj_kvx")
        o_x = _mem_attention(q_x, kv_x, seq, mem_tokens)
        xf, h = _out_proj(o_x, cast(w_o_x[l]), xf, g_post_x[l], g_pre_ffn[l])

        g_next = g_pre_mix[l + 1] if l + 1 < depth else g_pre_mix[0]
        xf, h = _ffn(h, cast(w_up[l]), conv_w[l], conv_b[l], cast(w_down[l]), xf, g_post_ffn[l], g_next, seq)

    return xf.reshape(batch, seq, d)
```

```python
import functools
import math

import jax
import jax.numpy as jnp
from jax import lax
from jax.experimental import pallas as pl
from jax.experimental.pallas import tpu as pltpu

CHUNK = 64
ROPE_THETA = 10000.0
EPS = 1e-6
DIFF_HEADS = 8
DIFF_QK_DIM = 64
DIFF_V_DIM = 128
MLA_HEADS = 8
MLA_Q_RANK = 512
MLA_KV_RANK = 256
MLA_NOPE_DIM = 128
MLA_ROPE_DIM = 64
MLA_V_DIM = 128
MEM_HEADS = 4
CONV_WIDTH = 3
DIFF_WIDTH = DIFF_HEADS * DIFF_V_DIM
MLA_WIDTH = MLA_HEADS * MLA_V_DIM

LANES = 128
BF16_SUBLANES = 16
MXU_COLS = 256
VMEM_LIMIT_BYTES = 56 * 2**20

MXU_DTYPE = jnp.bfloat16
F32 = jnp.float32
LOG2E = math.log2(math.e)


def _pick(n, pref, mult=LANES):
    if n <= pref:
        return n
    best = None
    for d in range(mult, pref + 1, mult):
        if n % d == 0:
            best = d
    assert best is not None, (n, pref, mult)
    return best


def _cparams(*sem):
    return pltpu.CompilerParams(dimension_semantics=sem, vmem_limit_bytes=VMEM_LIMIT_BYTES)


def _rms(x, g):
    return x * lax.rsqrt(jnp.mean(x * x, axis=-1, keepdims=True) + EPS) * g


def _rope128(t, cos, sin):
    return t * cos + pltpu.roll(t, LANES // 2, 1) * sin


def _pair_halves(w):
    q = LANES // 4
    shape = w.shape
    w = w.reshape(*shape[:-1], shape[-1] // LANES, 2, 2, q)
    return jnp.swapaxes(w, -3, -2).reshape(shape)


_NT = (((1,), (1,)), ((), ()))


def _tables_kernel(pos_ref, inv_ref, sgn_ref, cos_ref, sin_ref):
    ang = pos_ref[...].astype(F32) * inv_ref[...]
    cos_ref[...] = jnp.cos(ang)
    sin_ref[...] = jnp.sin(ang) * sgn_ref[...]


def _rope_tables(positions):
    m = positions.size
    d = DIFF_QK_DIM
    inv = ROPE_THETA ** (-jnp.arange(0, d, 2, dtype=F32) / d)
    inv128 = jnp.tile(inv, LANES // (d // 2))[None, :]
    sgn128 = jnp.concatenate([-jnp.ones((LANES // 2,), F32), jnp.ones((LANES // 2,), F32)])[None, :]
    bm = _pick(m, 2048, 8)
    row = pl.BlockSpec((bm, LANES), lambda i: (i, 0))
    const = pl.BlockSpec((1, LANES), lambda i: (0, 0))
    return pl.pallas_call(
        _tables_kernel,
        grid=(m // bm,),
        in_specs=[pl.BlockSpec((bm, 1), lambda i: (i, 0)), const, const],
        out_specs=[row, row],
        out_shape=[jax.ShapeDtypeStruct((m, LANES), F32)] * 2,
        compiler_params=_cparams("parallel"),
        name="rope_tables",
    )(positions.reshape(m, 1), inv128, sgn128)


def _rmsnorm_kernel(x_ref, g_ref, o_ref):
    o_ref[...] = _rms(x_ref[...], g_ref[...]).astype(o_ref.dtype)


def _rmsnorm(x, g):
    m, d = x.shape
    bm = _pick(m, 512, 8)
    return pl.pallas_call(
        _rmsnorm_kernel,
        grid=(m // bm,),
        in_specs=[pl.BlockSpec((bm, d), lambda i: (i, 0)), pl.BlockSpec((1, d), lambda i: (0, 0))],
        out_specs=pl.BlockSpec((bm, d), lambda i: (i, 0)),
        out_shape=jax.ShapeDtypeStruct((m, d), MXU_DTYPE),
        compiler_params=_cparams("parallel"),
        name="rmsnorm",
    )(x, g[None, :])


PROJ_ROW_TILE = 256


def _proj_kernel(h_ref, w_ref, *rest, epilogue, sub, rows):
    *extra, o_ref = rest
    for r in range(h_ref.shape[0] // rows):
        rs = pl.ds(r * rows, rows)
        h = h_ref[rs, :]
        for t in range(w_ref.shape[1] // sub):
            cols = slice(t * sub, (t + 1) * sub)
            acc = jnp.dot(h, w_ref[:, cols], preferred_element_type=F32)
            o_ref[rs, cols] = epilogue(acc, rs, cols, *extra).astype(o_ref.dtype)


def _ep_plain(acc, rs, cols):
    return acc


def _ep_rope(acc, rs, cols, cos_ref, sin_ref, *, q_tiles=0, rope_from=0):
    if cols.start < rope_from:
        return acc
    cos, sin = cos_ref[rs, :], sin_ref[rs, :]
    n = acc.shape[1] // LANES
    out = jnp.concatenate(
        [_rope128(acc[:, c * LANES:(c + 1) * LANES], cos, sin) for c in range(n)], axis=1)
    if q_tiles:
        out = out * jnp.where(pl.program_id(1) < q_tiles, LOG2E, 1.0)
    return out


def _proj(h, w, epilogue, *, name, extras=(), extra_specs=None, bm_pref=1024, bn_pref=512):
    m, k = h.shape
    n = w.shape[1]
    bm = _pick(m, bm_pref, BF16_SUBLANES)
    bn = _pick(n, bn_pref)
    specs = [pl.BlockSpec((bm, k), lambda i, j: (i, 0)), pl.BlockSpec((k, bn), lambda i, j: (0, j))]
    specs += list(extra_specs(bm, bn)) if extra_specs else []
    return pl.pallas_call(
        functools.partial(_proj_kernel, epilogue=epilogue, sub=_pick(bn, MXU_COLS),
                          rows=_pick(bm, PROJ_ROW_TILE, BF16_SUBLANES)),
        grid=(m // bm, n // bn),
        in_specs=specs,
        out_specs=pl.BlockSpec((bm, bn), lambda i, j: (i, j)),
        out_shape=jax.ShapeDtypeStruct((m, n), MXU_DTYPE),
        compiler_params=_cparams("parallel", "arbitrary"),
        name=name,
    )(h, w, *extras)


def _rope_specs(bm, bn):
    tab = pl.BlockSpec((bm, LANES), lambda i, j: (i, 0))
    return [tab, tab]


def _proj_t_kernel(h_ref, wt_ref, o_ref):
    o_ref[...] = lax.dot_general(wt_ref[...], h_ref[...], _NT,
                                 preferred_element_type=F32).astype(o_ref.dtype)


def _proj_t(h, wt, *, name):
    m, k = h.shape
    n = wt.shape[0]
    bm = _pick(m, 1024)
    bn = _pick(n, 512, BF16_SUBLANES)
    return pl.pallas_call(
        _proj_t_kernel,
        grid=(m // bm, n // bn),
        in_specs=[pl.BlockSpec((bm, k), lambda i, j: (i, 0)), pl.BlockSpec((bn, k), lambda i, j: (j, 0))],
        out_specs=pl.BlockSpec((bn, bm), lambda i, j: (j, i)),
        out_shape=jax.ShapeDtypeStruct((n, m), MXU_DTYPE),
        compiler_params=_cparams("parallel", "arbitrary"),
        name=name,
    )(h, wt)


def _latent_q_kernel(c_ref, g_ref, w_ref, cos_ref, sin_ref, o_ref, *, scale):
    cn = _rms(c_ref[...].astype(F32), g_ref[...]).astype(MXU_DTYPE)
    acc = jnp.dot(cn, w_ref[...], preferred_element_type=F32)
    cos, sin = cos_ref[...], sin_ref[...]
    cols = []
    for c in range(acc.shape[1] // LANES):
        t = acc[:, c * LANES:(c + 1) * LANES]
        cols.append(_rope128(t, cos, sin) if c % 2 else t)
    o_ref[...] = (jnp.concatenate(cols, axis=1) * scale).astype(o_ref.dtype)


def _latent_kv_kernel(c_ref, g_ref, wk_ref, wvt_ref, k_ref, vt_ref):
    cn = _rms(c_ref[...].astype(F32), g_ref[...]).astype(MXU_DTYPE)
    k_ref[...] = jnp.dot(cn, wk_ref[...], preferred_element_type=F32).astype(k_ref.dtype)
    vt_ref[...] = lax.dot_general(wvt_ref[...], cn, _NT, preferred_element_type=F32).astype(vt_ref.dtype)


def _latent_q(src, col_block, g, w, cos, sin, scale):
    m = src.shape[0]
    rank, n = w.shape
    bm = _pick(m, 512, BF16_SUBLANES)
    tab = pl.BlockSpec((bm, LANES), lambda i: (i, 0))
    return pl.pallas_call(
        functools.partial(_latent_q_kernel, scale=scale),
        grid=(m // bm,),
        in_specs=[pl.BlockSpec((bm, rank), lambda i: (i, col_block)),
                  pl.BlockSpec((1, rank), lambda i: (0, 0)),
                  pl.BlockSpec((rank, n), lambda i: (0, 0)), tab, tab],
        out_specs=pl.BlockSpec((bm, n), lambda i: (i, 0)),
        out_shape=jax.ShapeDtypeStruct((m, n), MXU_DTYPE),
        compiler_params=_cparams("parallel"),
        name="mla_q",
    )(src, g[None, :], w, cos, sin)


def _latent_kv(src, col_block, g, wk, wvt):
    m = src.shape[0]
    rank, nk = wk.shape
    nv = wvt.shape[0]
    bm = _pick(m, 512)
    return pl.pallas_call(
        _latent_kv_kernel,
        grid=(m // bm,),
        in_specs=[pl.BlockSpec((bm, rank), lambda i: (i, col_block)),
                  pl.BlockSpec((1, rank), lambda i: (0, 0)),
                  pl.BlockSpec((rank, nk), lambda i: (0, 0)),
                  pl.BlockSpec((nv, rank), lambda i: (0, 0))],
        out_specs=[pl.BlockSpec((bm, nk), lambda i: (i, 0)), pl.BlockSpec((nv, bm), lambda i: (0, i))],
        out_shape=[jax.ShapeDtypeStruct((m, nk), MXU_DTYPE), jax.ShapeDtypeStruct((nv, m), MXU_DTYPE)],
        compiler_params=_cparams("parallel"),
        name="mla_kv",
    )(src, g[None, :], wk, wvt)


def _flash(q_parts, load_k, load_vt, qi, *, bq, bk, dv, unroll):
    diag = bq // bk
    assert unroll % diag == 0
    rel = (lax.broadcasted_iota(jnp.int32, (bk, bq), 0) // CHUNK
           - lax.broadcasted_iota(jnp.int32, (bk, bq), 1) // CHUNK)

    def group(jg, carries, count, masked):
        blocks = [jg * unroll + u for u in range(count)]
        scores = [[lax.dot_general(load_k(j, part), q, _NT, preferred_element_type=F32)
                   for part, q in enumerate(q_parts)] for j in blocks]
        carries = list(carries)
        for u, j in enumerate(blocks):
            d = u - (count - masked)
            for part, s in enumerate(scores[u]):
                m, l, acc = carries[part]
                if d >= 0:
                    s = jnp.where(rel <= -d * (bk // CHUNK), s, -jnp.inf)
                m_new = jnp.maximum(m, s.max(axis=0, keepdims=True))
                alpha = jnp.exp2(m - m_new)
                p = jnp.exp2(s - m_new)
                l = alpha * l + p.sum(axis=0, keepdims=True)
                acc = alpha * acc + jnp.dot(load_vt(j, part), p.astype(MXU_DTYPE),
                                            preferred_element_type=F32)
                carries[part] = (m_new, l, acc)
        return tuple(carries)

    init = (jnp.full((1, bq), -jnp.inf, F32), jnp.zeros((1, bq), F32), jnp.zeros((dv, bq), F32))
    carries = tuple(init for _ in q_parts)
    n_blocks = (qi + 1) * diag
    n_groups = (n_blocks + unroll - 1) // unroll
    carries = lax.fori_loop(0, n_groups - 1, lambda jg, c: group(jg, c, unroll, 0), carries)
    last = (n_blocks - (n_groups - 1) * unroll) // diag - 1
    tails = [functools.partial(group, count=(t + 1) * diag, masked=diag) for t in range(unroll // diag)]
    carries = lax.switch(last, [lambda c, f=f: f(n_groups - 1, c) for f in tails], carries)
    return [(acc, l) for _, l, acc in carries]


KV_GROUP_KEYS = 1024


def _kv_group(seq, bq, bk):
    group = max(bq // bk, min(KV_GROUP_KEYS // bk, seq // bk))
    assert (seq // bk) % group == 0 and group % (bq // bk) == 0
    return group


DIFF_HEADS_PER_STEP = 2


def _diff_attn_kernel(lam_ref, q_ref, k_ref, vt_ref, g_ref, o_ref, *, bq, bk, lam_init, unroll):
    qi = pl.program_id(2)
    lam_v = lam_ref[...]
    lam = (jnp.exp(jnp.sum(lam_v[0:1] * lam_v[1:2], axis=1, keepdims=True))
           - jnp.exp(jnp.sum(lam_v[2:3] * lam_v[3:4], axis=1, keepdims=True)) + lam_init)
    hw = LANES
    is_q1 = (lax.broadcasted_iota(jnp.int32, (bq, hw), 1) & (DIFF_QK_DIM // 2)) == 0
    q_parts = []
    for hh in range(DIFF_HEADS_PER_STEP):
        q = q_ref[:, hh * hw:(hh + 1) * hw]
        zero = jnp.zeros_like(q)
        q_parts += [jnp.where(is_q1, q, zero), jnp.where(is_q1, zero, q)]

    def load_k(j, part):
        hh = part // 2
        return k_ref[pl.ds(pl.multiple_of(j * bk, bk), bk), hh * hw:(hh + 1) * hw]

    def load_vt(j, part):
        hh = part // 2
        return vt_ref[hh * DIFF_V_DIM:(hh + 1) * DIFF_V_DIM, pl.ds(pl.multiple_of(j * bk, bk), bk)]

    outs = _flash(q_parts, load_k, load_vt, qi, bq=bq, bk=bk, dv=DIFF_V_DIM, unroll=unroll)
    for hh in range(DIFF_HEADS_PER_STEP):
        (acc1, l1), (acc2, l2) = outs[2 * hh], outs[2 * hh + 1]
        o = acc1 * (1.0 / l1) - lam * (acc2 * (1.0 / l2))
        o = o * lax.rsqrt(jnp.mean(o * o, axis=0, keepdims=True) + EPS) * g_ref[...]
        o_ref[:, hh * DIFF_V_DIM:(hh + 1) * DIFF_V_DIM] = (o * (1.0 - lam_init)).T.astype(o_ref.dtype)


def _diff_attention(qk, vt, lam_vecs, g_sub, lam_init, batch, seq):
    m = qk.shape[0]
    bq = _pick(seq, 512, CHUNK)
    bk = _pick(bq, 512, CHUNK)
    nq = seq // bq
    hs = DIFF_HEADS_PER_STEP
    steps = DIFF_HEADS // hs
    unroll = _kv_group(seq, bq, bk)
    return pl.pallas_call(
        functools.partial(_diff_attn_kernel, bq=bq, bk=bk, lam_init=lam_init, unroll=unroll),
        grid=(batch, steps, nq),
        in_specs=[
            pl.BlockSpec((4, DIFF_QK_DIM), lambda b, h, i: (0, 0)),
            pl.BlockSpec((bq, hs * LANES), lambda b, h, i: (b * nq + i, h)),
            pl.BlockSpec((seq, hs * LANES), lambda b, h, i: (b, steps + h)),
            pl.BlockSpec((hs * DIFF_V_DIM, seq), lambda b, h, i: (h, b)),
            pl.BlockSpec((DIFF_V_DIM, 1), lambda b, h, i: (0, 0)),
        ],
        out_specs=pl.BlockSpec((bq, hs * DIFF_V_DIM), lambda b, h, i: (b * nq + i, h)),
        out_shape=jax.ShapeDtypeStruct((m, DIFF_WIDTH), MXU_DTYPE),
        compiler_params=_cparams("parallel", "parallel", "arbitrary"),
        name="diff_attention",
    )(lam_vecs, qk, qk, vt, g_sub[:, None])


MLA_HEADS_PER_STEP = 4


def _mla_attn_kernel(q_ref, kn_ref, kpe_ref, vt_ref, o_ref, kcat_ref, *, bq, bk, unroll):
    qi = pl.program_id(2)
    qw = 2 * LANES

    @pl.when(qi == 0)
    def _():
        for hh in range(MLA_HEADS_PER_STEP):
            kcat_ref[hh, :, :MLA_NOPE_DIM] = kn_ref[:, hh * MLA_NOPE_DIM:(hh + 1) * MLA_NOPE_DIM]
            kcat_ref[hh, :, MLA_NOPE_DIM:] = kpe_ref[...]

    def load_k(j, part):
        return kcat_ref[part, pl.ds(pl.multiple_of(j * bk, bk), bk), :]

    def load_vt(j, part):
        return vt_ref[part * MLA_V_DIM:(part + 1) * MLA_V_DIM, pl.ds(pl.multiple_of(j * bk, bk), bk)]

    q_parts = [q_ref[:, hh * qw:(hh + 1) * qw] for hh in range(MLA_HEADS_PER_STEP)]
    outs = _flash(q_parts, load_k, load_vt, qi, bq=bq, bk=bk, dv=MLA_V_DIM, unroll=unroll)
    for hh, (acc, l) in enumerate(outs):
        o_ref[:, hh * MLA_V_DIM:(hh + 1) * MLA_V_DIM] = (acc * (1.0 / l)).T.astype(o_ref.dtype)


def _mla_attention(q, k_nope, kpe_src, kpe_block, vt, batch, seq):
    m = q.shape[0]
    bq = _pick(seq, 512, CHUNK)
    bk = _pick(bq, 512, CHUNK)
    nq = seq // bq
    hs = MLA_HEADS_PER_STEP
    qw = 2 * LANES
    unroll = _kv_group(seq, bq, bk)
    return pl.pallas_call(
        functools.partial(_mla_attn_kernel, bq=bq, bk=bk, unroll=unroll),
        grid=(batch, MLA_HEADS // hs, nq),
        in_specs=[
            pl.BlockSpec((bq, hs * qw), lambda b, h, i: (b * nq + i, h)),
            pl.BlockSpec((seq, hs * MLA_NOPE_DIM), lambda b, h, i: (b, h)),
            pl.BlockSpec((seq, LANES), lambda b, h, i: (b, kpe_block)),
            pl.BlockSpec((hs * MLA_V_DIM, seq), lambda b, h, i: (h, b)),
        ],
        out_specs=pl.BlockSpec((bq, hs * MLA_V_DIM), lambda b, h, i: (b * nq + i, h)),
        out_shape=jax.ShapeDtypeStruct((m, MLA_WIDTH), MXU_DTYPE),
        scratch_shapes=[pltpu.VMEM((hs, seq, qw), MXU_DTYPE)],
        compiler_params=_cparams("parallel", "parallel", "arbitrary"),
        name="mla_attention",
    )(q, k_nope, kpe_src, vt)


def _merge_kernel(h_ref, od_ref, om_ref, wga_ref, wgb_ref, ba_ref, bb_ref, wd_ref, wm_ref, o_ref, *, sub, rows):
    for r in range(h_ref.shape[0] // rows):
        rs = pl.ds(r * rows, rows)
        h, od, om = h_ref[rs, :], od_ref[rs, :], om_ref[rs, :]
        for t in range(wd_ref.shape[1] // sub):
            cols = slice(t * sub, (t + 1) * sub)
            ga = jax.nn.sigmoid(jnp.dot(h, wga_ref[:, cols], preferred_element_type=F32) + ba_ref[:, cols])
            gb = jax.nn.sigmoid(jnp.dot(h, wgb_ref[:, cols], preferred_element_type=F32) + bb_ref[:, cols])
            a = jnp.dot(od, wd_ref[:, cols], preferred_element_type=F32)
            b = jnp.dot(om, wm_ref[:, cols], preferred_element_type=F32)
            o_ref[rs, cols] = (ga * a + gb * b).astype(o_ref.dtype)


def _merge(h, o_diff, o_mla, w_gate, b_gate, w_d, w_m):
    m, d = h.shape
    n = w_d.shape[1]
    bm = _pick(m, 1024, BF16_SUBLANES)
    bn = _pick(n, 512)
    nb = n // bn
    lo = lambda i, j: (0, j)
    hi = lambda i, j: (0, j + nb)
    row = lambda i, j: (i, 0)
    return pl.pallas_call(
        functools.partial(_merge_kernel, sub=_pick(bn, MXU_COLS), rows=_pick(bm, PROJ_ROW_TILE, BF16_SUBLANES)),
        grid=(m // bm, nb),
        in_specs=[
            pl.BlockSpec((bm, d), row),
            pl.BlockSpec((bm, DIFF_WIDTH), row),
            pl.BlockSpec((bm, MLA_WIDTH), row),
            pl.BlockSpec((d, bn), lo),
            pl.BlockSpec((d, bn), hi),
            pl.BlockSpec((1, bn), lo),
            pl.BlockSpec((1, bn), hi),
            pl.BlockSpec((DIFF_WIDTH, bn), lo),
            pl.BlockSpec((MLA_WIDTH, bn), lo),
        ],
        out_specs=pl.BlockSpec((bm, bn), lambda i, j: (i, j)),
        out_shape=jax.ShapeDtypeStruct((m, n), MXU_DTYPE),
        compiler_params=_cparams("parallel", "arbitrary"),
        name="branch_merge",
    )(h, o_diff, o_mla, w_gate, w_gate, b_gate[None, :], b_gate[None, :], w_d, w_m)


def _residual_norm(y, x_ref, gp_ref, gn_ref, xo_ref, ho_ref):
    xn = x_ref[...] + _rms(y, gp_ref[...])
    xo_ref[...] = xn
    ho_ref[...] = _rms(xn, gn_ref[...]).astype(ho_ref.dtype)


def _out_proj_kernel(lhs_ref, w_ref, x_ref, gp_ref, gn_ref, xo_ref, ho_ref, *, row_tiles):
    rows = lhs_ref.shape[0] // row_tiles
    for t in range(row_tiles):
        r = pl.ds(t * rows, rows)
        y = jnp.dot(lhs_ref[r, :], w_ref[...], preferred_element_type=F32)
        _residual_norm(y, x_ref.at[r], gp_ref, gn_ref, xo_ref.at[r], ho_ref.at[r])


def _out_proj(lhs, w, x, g_post, g_next):
    m, kdim = lhs.shape
    n = w.shape[1]
    bm = _pick(m, 512, BF16_SUBLANES)
    row = lambda i: (i, 0)
    const = lambda i: (0, 0)
    return pl.pallas_call(
        functools.partial(_out_proj_kernel, row_tiles=2 if bm % (2 * BF16_SUBLANES) == 0 else 1),
        grid=(m // bm,),
        in_specs=[
            pl.BlockSpec((bm, kdim), row),
            pl.BlockSpec((kdim, n), const),
            pl.BlockSpec((bm, n), row),
            pl.BlockSpec((1, n), const),
            pl.BlockSpec((1, n), const),
        ],
        out_specs=[pl.BlockSpec((bm, n), row), pl.BlockSpec((bm, n), row)],
        out_shape=[jax.ShapeDtypeStruct((m, n), F32), jax.ShapeDtypeStruct((m, n), MXU_DTYPE)],
        compiler_params=_cparams("parallel"),
        name="out_proj",
    )(lhs, w, x, g_post[None, :], g_next[None, :])


def _mem_attn_kernel(h_ref, wq_ref, kv_ref, wo_ref, x_ref, gp_ref, gn_ref, xo_ref, ho_ref, o_ref, *, dh, scale):
    width = MEM_HEADS * dh
    h = h_ref[...]
    for hd in range(MEM_HEADS):
        cols = slice(hd * dh, (hd + 1) * dh)
        q = (jnp.dot(h, wq_ref[:, cols], preferred_element_type=F32) * scale).astype(MXU_DTYPE)
        k = kv_ref[:, cols]
        v = kv_ref[:, width + hd * dh:width + (hd + 1) * dh]
        s = lax.dot_general(q, k, _NT, preferred_element_type=F32)
        p = jnp.exp(s - s.max(axis=1, keepdims=True))
        l = p.sum(axis=1, keepdims=True)
        o_ref[:, cols] = (jnp.dot(p.astype(MXU_DTYPE), v, preferred_element_type=F32) / l).astype(o_ref.dtype)
    row_tiles = 2 if o_ref.shape[0] % (2 * BF16_SUBLANES) == 0 else 1
    rows = o_ref.shape[0] // row_tiles
    for t in range(row_tiles):
        r = pl.ds(t * rows, rows)
        y = jnp.dot(o_ref[r, :], wo_ref[...], preferred_element_type=F32)
        _residual_norm(y, x_ref.at[r], gp_ref, gn_ref, xo_ref.at[r], ho_ref.at[r])


def _mem_attention(h, w_q, kv, w_o, x, g_post, g_next, seq, mem_tokens):
    m, d = h.shape
    dh = d // MEM_HEADS
    bq = _pick(seq, 512, BF16_SUBLANES)
    per_batch = seq // bq
    row = lambda i: (i, 0)
    const = lambda i: (0, 0)
    resident = functools.partial(pl.BlockSpec, index_map=const, pipeline_mode=pl.Buffered(1))
    return pl.pallas_call(
        functools.partial(_mem_attn_kernel, dh=dh, scale=dh ** -0.5),
        grid=(m // bq,),
        in_specs=[pl.BlockSpec((bq, d), row),
                  resident((d, d)),
                  pl.BlockSpec((mem_tokens, 2 * d), lambda i: (i // per_batch, 0)),
                  resident((d, d)),
                  pl.BlockSpec((bq, d), row),
                  pl.BlockSpec((1, d), const),
                  pl.BlockSpec((1, d), const)],
        out_specs=[pl.BlockSpec((bq, d), row), pl.BlockSpec((bq, d), row)],
        out_shape=[jax.ShapeDtypeStruct((m, d), F32), jax.ShapeDtypeStruct((m, d), MXU_DTYPE)],
        scratch_shapes=[pltpu.VMEM((bq, d), MXU_DTYPE)],
        compiler_params=_cparams("parallel"),
        name="mem_attention",
    )(h, w_q, kv, w_o, x, g_post[None, :], g_next[None, :])


HALO = BF16_SUBLANES


def _ffn_kernel(h_ref, halo_ref, wa_ref, wg_ref, cwa_ref, cwg_ref, cba_ref, cbg_ref, wd_ref, x_ref, gp_ref,
                gn_ref, xo_ref, ho_ref, hs_ref, u_ref, acc_ref, *, bm, blocks_per_seq, nb, sub, row_tiles):
    i = pl.program_id(0)
    j = pl.program_id(1)

    @pl.when(j == 0)
    def _():
        halo = halo_ref[...]
        first = (i % blocks_per_seq) == 0
        hs_ref[:HALO, :] = jnp.where(first, jnp.zeros_like(halo), halo)
        hs_ref[HALO:, :] = h_ref[...]
        acc_ref[...] = jnp.zeros_like(acc_ref)

    hs = hs_ref[...]

    tiles = [slice(t * sub, (t + 1) * sub) for t in range(wa_ref.shape[1] // sub)]
    for t, cols in enumerate(tiles):
        u_ref[2 * t] = jnp.dot(hs, wa_ref[:, cols], preferred_element_type=F32)
        u_ref[2 * t + 1] = jnp.dot(hs, wg_ref[:, cols], preferred_element_type=F32)

    def conv(slot, cols, row0, rows, cw_ref, cb_ref):
        cw = cw_ref[:, cols]
        c = cb_ref[:, cols]
        for tap in range(CONV_WIDTH):
            c = c + u_ref[slot, pl.ds(row0 + HALO - (CONV_WIDTH - 1) + tap, rows), :] * cw[tap:tap + 1]
        return c

    rows = bm // row_tiles
    for r in range(row_tiles):
        acts = []
        for t, cols in enumerate(tiles):
            a = conv(2 * t, cols, r * rows, rows, cwa_ref, cba_ref)
            g = conv(2 * t + 1, cols, r * rows, rows, cwg_ref, cbg_ref)
            acts.append((a * jax.nn.sigmoid(a) * g).astype(MXU_DTYPE))
        acc_ref[pl.ds(r * rows, rows), :] += jnp.dot(jnp.concatenate(acts, axis=1), wd_ref[...],
                                                     preferred_element_type=F32)

    @pl.when(j == nb - 1)
    def _():
        _residual_norm(acc_ref[...], x_ref, gp_ref, gn_ref, xo_ref, ho_ref)


def _ffn(h, w_up, conv_w, conv_b, w_down, x, g_post, g_next, seq):
    m, d = h.shape
    ff = w_up.shape[1] // 2
    bm = _pick(seq, 512, HALO)
    bn = _pick(ff, 512)
    sub = _pick(bn, MXU_COLS)
    nb = ff // bn
    halo_blocks = bm // HALO
    row = lambda i, j: (i, 0)
    const = lambda i, j: (0, 0)
    up_a = lambda i, j: (0, j)
    up_g = lambda i, j: (0, j + nb)
    return pl.pallas_call(
        functools.partial(_ffn_kernel, bm=bm, blocks_per_seq=seq // bm, nb=nb, sub=sub,
                          row_tiles=2 if bm % (2 * BF16_SUBLANES) == 0 else 1),
        grid=(m // bm, nb),
        in_specs=[
            pl.BlockSpec((bm, d), row),
            pl.BlockSpec((HALO, d), lambda i, j: (jnp.maximum(i * halo_blocks - 1, 0), 0)),
            pl.BlockSpec((d, bn), up_a),
            pl.BlockSpec((d, bn), up_g),
            pl.BlockSpec((CONV_WIDTH, bn), up_a),
            pl.BlockSpec((CONV_WIDTH, bn), up_g),
            pl.BlockSpec((1, bn), up_a),
            pl.BlockSpec((1, bn), up_g),
            pl.BlockSpec((bn, d), lambda i, j: (j, 0)),
            pl.BlockSpec((bm, d), row),
            pl.BlockSpec((1, d), const),
            pl.BlockSpec((1, d), const),
        ],
        out_specs=[pl.BlockSpec((bm, d), row), pl.BlockSpec((bm, d), row)],
        out_shape=[jax.ShapeDtypeStruct((m, d), F32), jax.ShapeDtypeStruct((m, d), MXU_DTYPE)],
        scratch_shapes=[pltpu.VMEM((HALO + bm, d), MXU_DTYPE),
                        pltpu.VMEM((2 * (bn // sub), HALO + bm, sub), F32),
                        pltpu.VMEM((bm, d), F32)],
        compiler_params=_cparams("parallel", "arbitrary"),
        name="ffn",
    )(h, h, w_up, w_up, conv_w, conv_w, conv_b[None, :], conv_b[None, :], w_down, x,
      g_post[None, :], g_next[None, :])


def kernel(x, mem, positions, g_pre_mix, w_in, b_gate, lam_q1, lam_k1, lam_q2, lam_k2, g_diff_sub, g_cq,
           w_uq, g_ckv, w_ukv, w_br_diff, w_br_mla, w_mix_out, g_post_mix, g_pre_x, g_mem, w_q_x, w_kv_x,
           w_o_x, g_post_x, g_pre_ffn, w_up, conv_w, conv_b, w_down, g_post_ffn):
    batch, seq, d = x.shape
    mem_tokens = mem.shape[1]
    depth = w_in.shape[0]
    m = batch * seq
    cast = lambda a: a.astype(MXU_DTYPE)

    qw = DIFF_HEADS * 2 * DIFF_QK_DIM
    o_v = 2 * qw
    o_cq = o_v + DIFF_WIDTH
    o_ckv = o_cq + MLA_Q_RANK
    o_kr = o_ckv + MLA_KV_RANK
    o_gt = o_kr + MLA_ROPE_DIM
    assert w_in.shape[2] == o_gt + 2 * d
    assert MLA_Q_RANK % MLA_KV_RANK == 0
    qhead = MLA_NOPE_DIM + MLA_ROPE_DIM
    lat_width = -(-(o_kr - o_cq + LANES) // MXU_COLS) * MXU_COLS
    kvhead = MLA_NOPE_DIM + MLA_V_DIM

    cos, sin = _rope_tables(positions)
    xf = x.reshape(m, d)
    memf = mem.reshape(batch * mem_tokens, d)
    h = _rmsnorm(xf, g_pre_mix[0])
    w_in_c = cast(w_in)

    for l in range(depth):
        lam_init = 0.8 - 0.6 * math.exp(-0.3 * l)
        wl = w_in_c[l]
        w_qk = cast(_pair_halves(
            jnp.concatenate([wl[:, :qw] * (DIFF_QK_DIM ** -0.5), wl[:, qw:o_v]], axis=1)))
        w_vt = cast(wl[:, o_v:o_cq].T)
        w_kpe = _pair_halves(jnp.pad(wl[:, o_kr:o_gt], ((0, 0), (0, LANES - MLA_ROPE_DIM))))
        w_lat = jnp.pad(jnp.concatenate([wl[:, o_cq:o_kr], w_kpe], axis=1),
                        ((0, 0), (0, lat_width - (o_kr - o_cq) - LANES)))
        w_gate = wl[:, o_gt:]
        w_uq_h = jnp.pad(w_uq[l].reshape(MLA_Q_RANK, MLA_HEADS, qhead),
                         ((0, 0), (0, 0), (0, 2 * LANES - qhead)))
        w_uq_p = cast(jnp.concatenate([w_uq_h[:, :, :MLA_NOPE_DIM], _pair_halves(w_uq_h[:, :, MLA_NOPE_DIM:])],
                                      axis=2).reshape(MLA_Q_RANK, -1))
        w_ukv_h = w_ukv[l].reshape(MLA_KV_RANK, MLA_HEADS, kvhead)
        w_uk = cast(w_ukv_h[:, :, :MLA_NOPE_DIM].reshape(MLA_KV_RANK, -1))
        w_uvt = cast(w_ukv_h[:, :, MLA_NOPE_DIM:].reshape(MLA_KV_RANK, -1).T)

        bn_qk = _pick(2 * qw, 512)
        qk = _proj(h, w_qk, functools.partial(_ep_rope, q_tiles=qw // bn_qk), name="proj_qk",
                   extras=(cos, sin), extra_specs=_rope_specs)
        vt = _proj_t(h, w_vt, name="proj_vt")
        lat = _proj(h, w_lat, functools.partial(_ep_rope, rope_from=o_kr - o_cq), name="proj_latent",
                    extras=(cos, sin), extra_specs=_rope_specs, bn_pref=lat_width)
        q_mla = _latent_q(lat, 0, g_cq[l], w_uq_p, cos, sin, qhead ** -0.5 * LOG2E)
        k_mla, vt_mla = _latent_kv(lat, MLA_Q_RANK // MLA_KV_RANK, g_ckv[l], w_uk, w_uvt)
        lam_vecs = jnp.stack([lam_q1[l], lam_k1[l], lam_q2[l], lam_k2[l]])
        o_diff = _diff_attention(qk, vt, lam_vecs, g_diff_sub[l], lam_init, batch, seq)
        o_mla = _mla_attention(q_mla, k_mla, lat, (o_kr - o_cq) // LANES, vt_mla, batch, seq)
        merged = _merge(h, o_diff, o_mla, w_gate, b_gate[l], cast(w_br_diff[l]), cast(w_br_mla[l]))
        xf, h = _out_proj(merged, cast(w_mix_out[l]), xf, g_post_mix[l], g_pre_x[l])

        kv_x = _proj(_rmsnorm(memf, g_mem[l]), cast(w_kv_x[l]), _ep_plain, name="proj_kvx")
        xf, h = _mem_attention(h, cast(w_q_x[l]), kv_x, cast(w_o_x[l]), xf, g_post_x[l], g_pre_ffn[l],
                               seq, mem_tokens)

        g_next = g_pre_mix[l + 1] if l + 1 < depth else g_pre_mix[0]
        xf, h = _ffn(h, cast(w_up[l]), conv_w[l], conv_b[l], cast(w_down[l]), xf, g_post_ffn[l], g_next, seq)

    return xf.reshape(batch, seq, d)
```

```python
import functools
import math

import jax
import jax.numpy as jnp
from jax import lax
from jax.experimental import pallas as pl
from jax.experimental.pallas import tpu as pltpu

CHUNK = 64
ROPE_THETA = 10000.0
EPS = 1e-6
DIFF_HEADS = 8
DIFF_QK_DIM = 64
DIFF_V_DIM = 128
MLA_HEADS = 8
MLA_Q_RANK = 512
MLA_KV_RANK = 256
MLA_NOPE_DIM = 128
MLA_ROPE_DIM = 64
MLA_V_DIM = 128
MEM_HEADS = 4
CONV_WIDTH = 3
DIFF_WIDTH = DIFF_HEADS * DIFF_V_DIM
MLA_WIDTH = MLA_HEADS * MLA_V_DIM

LANES = 128
BF16_SUBLANES = 16
MXU_COLS = 256
VMEM_LIMIT_BYTES = 56 * 2**20

MXU_DTYPE = jnp.bfloat16
F32 = jnp.float32
LOG2E = math.log2(math.e)


def _pick(n, pref, mult=LANES):
    if n <= pref:
        return n
    best = None
    for d in range(mult, pref + 1, mult):
        if n % d == 0:
            best = d
    assert best is not None, (n, pref, mult)
    return best


def _cparams(*sem):
    return pltpu.CompilerParams(dimension_semantics=sem, vmem_limit_bytes=VMEM_LIMIT_BYTES)


def _rms(x, g):
    return x * lax.rsqrt(jnp.mean(x * x, axis=-1, keepdims=True) + EPS) * g


def _rope128(t, cos, sin):
    return t * cos + pltpu.roll(t, LANES // 2, 1) * sin


def _pair_halves(w):
    q = LANES // 4
    shape = w.shape
    w = w.reshape(*shape[:-1], shape[-1] // LANES, 2, 2, q)
    return jnp.swapaxes(w, -3, -2).reshape(shape)


_NT = (((1,), (1,)), ((), ()))


def _w_in_kernel(w_ref, qk_ref, vt_ref, lat_ref, gate_ref, *, qw, o_v, o_cq, o_kr, o_gt):
    x = w_ref[...]
    q = LANES // 4
    qk = x[:, :o_v]
    col = lax.broadcasted_iota(jnp.int32, qk.shape, 1)
    qk = jnp.where(col < qw, qk * (DIFF_QK_DIM ** -0.5), qk)
    lane = col % LANES
    qk = jnp.where((lane >= q) & (lane < 2 * q), pltpu.roll(qk, o_v - q, 1),
                   jnp.where((lane >= 2 * q) & (lane < 3 * q), pltpu.roll(qk, q, 1), qk))
    qk_ref[...] = qk.astype(qk_ref.dtype)
    vt_ref[...] = x[:, o_v:o_cq].T.astype(vt_ref.dtype)
    t = x[:, o_kr:o_kr + LANES]
    lane = lax.broadcasted_iota(jnp.int32, t.shape, 1)
    kpe = jnp.where(lane < q, t, jnp.where((lane >= 2 * q) & (lane < 3 * q), pltpu.roll(t, q, 1), 0.0))
    pad = jnp.zeros((x.shape[0], lat_ref.shape[1] - (o_kr - o_cq) - LANES), F32)
    lat_ref[...] = jnp.concatenate([x[:, o_cq:o_kr], kpe, pad], axis=1).astype(lat_ref.dtype)
    gate_ref[...] = x[:, o_gt:].astype(gate_ref.dtype)


def _split_w_in(w_in, lat_width, *, qw, o_v, o_cq, o_kr, o_gt):
    depth, d, n_in = w_in.shape
    assert MLA_ROPE_DIM == 2 * (LANES // 4) and o_kr % LANES == 0 and o_kr + LANES <= n_in
    rows = _pick(d, 256, LANES)
    block = lambda width: pl.BlockSpec((None, rows, width), lambda l, r: (l, r, 0))
    return pl.pallas_call(
        functools.partial(_w_in_kernel, qw=qw, o_v=o_v, o_cq=o_cq, o_kr=o_kr, o_gt=o_gt),
        grid=(depth, d // rows),
        in_specs=[block(n_in)],
        out_specs=[block(o_v), pl.BlockSpec((None, o_cq - o_v, rows), lambda l, r: (l, 0, r)),
                   block(lat_width), block(n_in - o_gt)],
        out_shape=[jax.ShapeDtypeStruct((depth, d, o_v), MXU_DTYPE),
                   jax.ShapeDtypeStruct((depth, o_cq - o_v, d), MXU_DTYPE),
                   jax.ShapeDtypeStruct((depth, d, lat_width), MXU_DTYPE),
                   jax.ShapeDtypeStruct((depth, d, n_in - o_gt), MXU_DTYPE)],
        compiler_params=_cparams("parallel", "parallel"),
        name="split_w_in",
    )(w_in)


def _tables_kernel(pos_ref, inv_ref, sgn_ref, cos_ref, sin_ref):
    ang = pos_ref[...].astype(F32) * inv_ref[...]
    cos_ref[...] = jnp.cos(ang)
    sin_ref[...] = jnp.sin(ang) * sgn_ref[...]


def _rope_tables(positions):
    m = positions.size
    d = DIFF_QK_DIM
    inv = ROPE_THETA ** (-jnp.arange(0, d, 2, dtype=F32) / d)
    inv128 = jnp.tile(inv, LANES // (d // 2))[None, :]
    sgn128 = jnp.concatenate([-jnp.ones((LANES // 2,), F32), jnp.ones((LANES // 2,), F32)])[None, :]
    bm = _pick(m, 2048, 8)
    row = pl.BlockSpec((bm, LANES), lambda i: (i, 0))
    const = pl.BlockSpec((1, LANES), lambda i: (0, 0))
    return pl.pallas_call(
        _tables_kernel,
        grid=(m // bm,),
        in_specs=[pl.BlockSpec((bm, 1), lambda i: (i, 0)), const, const],
        out_specs=[row, row],
        out_shape=[jax.ShapeDtypeStruct((m, LANES), F32)] * 2,
        compiler_params=_cparams("parallel"),
        name="rope_tables",
    )(positions.reshape(m, 1), inv128, sgn128)


def _rmsnorm_kernel(x_ref, g_ref, o_ref):
    o_ref[...] = _rms(x_ref[...], g_ref[...]).astype(o_ref.dtype)


def _rmsnorm(x, g):
    m, d = x.shape
    bm = _pick(m, 512, 8)
    return pl.pallas_call(
        _rmsnorm_kernel,
        grid=(m // bm,),
        in_specs=[pl.BlockSpec((bm, d), lambda i: (i, 0)), pl.BlockSpec((1, d), lambda i: (0, 0))],
        out_specs=pl.BlockSpec((bm, d), lambda i: (i, 0)),
        out_shape=jax.ShapeDtypeStruct((m, d), MXU_DTYPE),
        compiler_params=_cparams("parallel"),
        name="rmsnorm",
    )(x, g[None, :])


PROJ_ROW_TILE = 256


def _proj_kernel(h_ref, w_ref, *rest, epilogue, sub, rows):
    *extra, o_ref = rest
    for r in range(h_ref.shape[0] // rows):
        rs = pl.ds(r * rows, rows)
        h = h_ref[rs, :]
        for t in range(w_ref.shape[1] // sub):
            cols = slice(t * sub, (t + 1) * sub)
            acc = jnp.dot(h, w_ref[:, cols], preferred_element_type=F32)
            o_ref[rs, cols] = epilogue(acc, rs, cols, *extra).astype(o_ref.dtype)


def _ep_plain(acc, rs, cols):
    return acc


def _ep_rope(acc, rs, cols, cos_ref, sin_ref, *, q_tiles=0, rope_from=0):
    if cols.start < rope_from:
        return acc
    cos, sin = cos_ref[rs, :], sin_ref[rs, :]
    n = acc.shape[1] // LANES
    out = jnp.concatenate(
        [_rope128(acc[:, c * LANES:(c + 1) * LANES], cos, sin) for c in range(n)], axis=1)
    if q_tiles:
        out = out * jnp.where(pl.program_id(1) < q_tiles, LOG2E, 1.0)
    return out


def _proj(h, w, epilogue, *, name, extras=(), extra_specs=None, bm_pref=1024, bn_pref=512):
    m, k = h.shape
    n = w.shape[1]
    bm = _pick(m, bm_pref, BF16_SUBLANES)
    bn = _pick(n, bn_pref)
    specs = [pl.BlockSpec((bm, k), lambda i, j: (i, 0)), pl.BlockSpec((k, bn), lambda i, j: (0, j))]
    specs += list(extra_specs(bm, bn)) if extra_specs else []
    return pl.pallas_call(
        functools.partial(_proj_kernel, epilogue=epilogue, sub=_pick(bn, MXU_COLS),
                          rows=_pick(bm, PROJ_ROW_TILE, BF16_SUBLANES)),
        grid=(m // bm, n // bn),
        in_specs=specs,
        out_specs=pl.BlockSpec((bm, bn), lambda i, j: (i, j)),
        out_shape=jax.ShapeDtypeStruct((m, n), MXU_DTYPE),
        compiler_params=_cparams("parallel", "arbitrary"),
        name=name,
    )(h, w, *extras)


def _rope_specs(bm, bn):
    tab = pl.BlockSpec((bm, LANES), lambda i, j: (i, 0))
    return [tab, tab]


def _proj_t_kernel(h_ref, wt_ref, o_ref):
    o_ref[...] = lax.dot_general(wt_ref[...], h_ref[...], _NT,
                                 preferred_element_type=F32).astype(o_ref.dtype)


def _proj_t(h, wt, *, name):
    m, k = h.shape
    n = wt.shape[0]
    bm = _pick(m, 1024)
    bn = _pick(n, 512, BF16_SUBLANES)
    return pl.pallas_call(
        _proj_t_kernel,
        grid=(m // bm, n // bn),
        in_specs=[pl.BlockSpec((bm, k), lambda i, j: (i, 0)), pl.BlockSpec((bn, k), lambda i, j: (j, 0))],
        out_specs=pl.BlockSpec((bn, bm), lambda i, j: (j, i)),
        out_shape=jax.ShapeDtypeStruct((n, m), MXU_DTYPE),
        compiler_params=_cparams("parallel", "arbitrary"),
        name=name,
    )(h, wt)


def _latent_q_kernel(c_ref, g_ref, w_ref, cos_ref, sin_ref, o_ref, *, scale):
    cn = _rms(c_ref[...].astype(F32), g_ref[...]).astype(MXU_DTYPE)
    acc = jnp.dot(cn, w_ref[...], preferred_element_type=F32)
    cos, sin = cos_ref[...], sin_ref[...]
    cols = []
    for c in range(acc.shape[1] // LANES):
        t = acc[:, c * LANES:(c + 1) * LANES]
        cols.append(_rope128(t, cos, sin) if c % 2 else t)
    o_ref[...] = (jnp.concatenate(cols, axis=1) * scale).astype(o_ref.dtype)


def _latent_kv_kernel(c_ref, g_ref, wk_ref, wvt_ref, k_ref, vt_ref):
    cn = _rms(c_ref[...].astype(F32), g_ref[...]).astype(MXU_DTYPE)
    k_ref[...] = jnp.dot(cn, wk_ref[...], preferred_element_type=F32).astype(k_ref.dtype)
    vt_ref[...] = lax.dot_general(wvt_ref[...], cn, _NT, preferred_element_type=F32).astype(vt_ref.dtype)


def _latent_q(src, col_block, g, w, cos, sin, scale):
    m = src.shape[0]
    rank, n = w.shape
    bm = _pick(m, 512, BF16_SUBLANES)
    tab = pl.BlockSpec((bm, LANES), lambda i: (i, 0))
    return pl.pallas_call(
        functools.partial(_latent_q_kernel, scale=scale),
        grid=(m // bm,),
        in_specs=[pl.BlockSpec((bm, rank), lambda i: (i, col_block)),
                  pl.BlockSpec((1, rank), lambda i: (0, 0)),
                  pl.BlockSpec((rank, n), lambda i: (0, 0)), tab, tab],
        out_specs=pl.BlockSpec((bm, n), lambda i: (i, 0)),
        out_shape=jax.ShapeDtypeStruct((m, n), MXU_DTYPE),
        compiler_params=_cparams("parallel"),
        name="mla_q",
    )(src, g[None, :], w, cos, sin)


def _latent_kv(src, col_block, g, wk, wvt):
    m = src.shape[0]
    rank, nk = wk.shape
    nv = wvt.shape[0]
    bm = _pick(m, 512)
    return pl.pallas_call(
        _latent_kv_kernel,
        grid=(m // bm,),
        in_specs=[pl.BlockSpec((bm, rank), lambda i: (i, col_block)),
                  pl.BlockSpec((1, rank), lambda i: (0, 0)),
                  pl.BlockSpec((rank, nk), lambda i: (0, 0)),
                  pl.BlockSpec((nv, rank), lambda i: (0, 0))],
        out_specs=[pl.BlockSpec((bm, nk), lambda i: (i, 0)), pl.BlockSpec((nv, bm), lambda i: (0, i))],
        out_shape=[jax.ShapeDtypeStruct((m, nk), MXU_DTYPE), jax.ShapeDtypeStruct((nv, m), MXU_DTYPE)],
        compiler_params=_cparams("parallel"),
        name="mla_kv",
    )(src, g[None, :], wk, wvt)


def _flash(q_parts, load_k, load_vt, qi, *, bq, bk, dv, unroll):
    diag = bq // bk
    assert unroll % diag == 0
    rel = (lax.broadcasted_iota(jnp.int32, (bk, bq), 0) // CHUNK
           - lax.broadcasted_iota(jnp.int32, (bk, bq), 1) // CHUNK)

    def group(jg, carries, count, masked):
        blocks = [jg * unroll + u for u in range(count)]
        scores = [[lax.dot_general(load_k(j, part), q, _NT, preferred_element_type=F32)
                   for part, q in enumerate(q_parts)] for j in blocks]
        carries = list(carries)
        for u, j in enumerate(blocks):
            d = u - (count - masked)
            for part, s in enumerate(scores[u]):
                m, l, acc = carries[part]
                if d >= 0:
                    s = jnp.where(rel <= -d * (bk // CHUNK), s, -jnp.inf)
                m_new = jnp.maximum(m, s.max(axis=0, keepdims=True))
                alpha = jnp.exp2(m - m_new)
                p = jnp.exp2(s - m_new)
                l = alpha * l + p.sum(axis=0, keepdims=True)
                acc = alpha * acc + jnp.dot(load_vt(j, part), p.astype(MXU_DTYPE),
                                            preferred_element_type=F32)
                carries[part] = (m_new, l, acc)
        return tuple(carries)

    init = (jnp.full((1, bq), -jnp.inf, F32), jnp.zeros((1, bq), F32), jnp.zeros((dv, bq), F32))
    carries = tuple(init for _ in q_parts)
    n_blocks = (qi + 1) * diag
    n_groups = (n_blocks + unroll - 1) // unroll
    carries = lax.fori_loop(0, n_groups - 1, lambda jg, c: group(jg, c, unroll, 0), carries)
    last = (n_blocks - (n_groups - 1) * unroll) // diag - 1
    tails = [functools.partial(group, count=(t + 1) * diag, masked=diag) for t in range(unroll // diag)]
    carries = lax.switch(last, [lambda c, f=f: f(n_groups - 1, c) for f in tails], carries)
    return [(acc, l) for _, l, acc in carries]


KV_GROUP_KEYS = 1024


def _kv_group(seq, bq, bk):
    group = max(bq // bk, min(KV_GROUP_KEYS // bk, seq // bk))
    assert (seq // bk) % group == 0 and group % (bq // bk) == 0
    return group


DIFF_HEADS_PER_STEP = 2


def _diff_attn_kernel(lam_ref, q_ref, k_ref, vt_ref, g_ref, o_ref, *, bq, bk, lam_init, unroll):
    qi = pl.program_id(2)
    lam_v = lam_ref[...]
    lam = (jnp.exp(jnp.sum(lam_v[0:1] * lam_v[1:2], axis=1, keepdims=True))
           - jnp.exp(jnp.sum(lam_v[2:3] * lam_v[3:4], axis=1, keepdims=True)) + lam_init)
    hw = LANES
    is_q1 = (lax.broadcasted_iota(jnp.int32, (bq, hw), 1) & (DIFF_QK_DIM // 2)) == 0
    q_parts = []
    for hh in range(DIFF_HEADS_PER_STEP):
        q = q_ref[:, hh * hw:(hh + 1) * hw]
        zero = jnp.zeros_like(q)
        q_parts += [jnp.where(is_q1, q, zero), jnp.where(is_q1, zero, q)]

    def load_k(j, part):
        hh = part // 2
        return k_ref[pl.ds(pl.multiple_of(j * bk, bk), bk), hh * hw:(hh + 1) * hw]

    def load_vt(j, part):
        hh = part // 2
        return vt_ref[hh * DIFF_V_DIM:(hh + 1) * DIFF_V_DIM, pl.ds(pl.multiple_of(j * bk, bk), bk)]

    outs = _flash(q_parts, load_k, load_vt, qi, bq=bq, bk=bk, dv=DIFF_V_DIM, unroll=unroll)
    for hh in range(DIFF_HEADS_PER_STEP):
        (acc1, l1), (acc2, l2) = outs[2 * hh], outs[2 * hh + 1]
        o = acc1 * (1.0 / l1) - lam * (acc2 * (1.0 / l2))
        o = o * lax.rsqrt(jnp.mean(o * o, axis=0, keepdims=True) + EPS) * g_ref[...]
        o_ref[:, hh * DIFF_V_DIM:(hh + 1) * DIFF_V_DIM] = (o * (1.0 - lam_init)).T.astype(o_ref.dtype)


def _diff_attention(qk, vt, lam_vecs, g_sub, lam_init, batch, seq):
    m = qk.shape[0]
    bq = _pick(seq, 512, CHUNK)
    bk = _pick(bq, 512, CHUNK)
    nq = seq // bq
    hs = DIFF_HEADS_PER_STEP
    steps = DIFF_HEADS // hs
    unroll = _kv_group(seq, bq, bk)
    return pl.pallas_call(
        functools.partial(_diff_attn_kernel, bq=bq, bk=bk, lam_init=lam_init, unroll=unroll),
        grid=(batch, steps, nq),
        in_specs=[
            pl.BlockSpec((4, DIFF_QK_DIM), lambda b, h, i: (0, 0)),
            pl.BlockSpec((bq, hs * LANES), lambda b, h, i: (b * nq + i, h)),
            pl.BlockSpec((seq, hs * LANES), lambda b, h, i: (b, steps + h)),
            pl.BlockSpec((hs * DIFF_V_DIM, seq), lambda b, h, i: (h, b)),
            pl.BlockSpec((DIFF_V_DIM, 1), lambda b, h, i: (0, 0)),
        ],
        out_specs=pl.BlockSpec((bq, hs * DIFF_V_DIM), lambda b, h, i: (b * nq + i, h)),
        out_shape=jax.ShapeDtypeStruct((m, DIFF_WIDTH), MXU_DTYPE),
        compiler_params=_cparams("parallel", "parallel", "arbitrary"),
        name="diff_attention",
    )(lam_vecs, qk, qk, vt, g_sub[:, None])


MLA_HEADS_PER_STEP = 4


def _mla_attn_kernel(q_ref, kn_ref, kpe_ref, vt_ref, o_ref, kcat_ref, *, bq, bk, unroll):
    qi = pl.program_id(2)
    qw = 2 * LANES

    @pl.when(qi == 0)
    def _():
        for hh in range(MLA_HEADS_PER_STEP):
            kcat_ref[hh, :, :MLA_NOPE_DIM] = kn_ref[:, hh * MLA_NOPE_DIM:(hh + 1) * MLA_NOPE_DIM]
            kcat_ref[hh, :, MLA_NOPE_DIM:] = kpe_ref[...]

    def load_k(j, part):
        return kcat_ref[part, pl.ds(pl.multiple_of(j * bk, bk), bk), :]

    def load_vt(j, part):
        return vt_ref[part * MLA_V_DIM:(part + 1) * MLA_V_DIM, pl.ds(pl.multiple_of(j * bk, bk), bk)]

    q_parts = [q_ref[:, hh * qw:(hh + 1) * qw] for hh in range(MLA_HEADS_PER_STEP)]
    outs = _flash(q_parts, load_k, load_vt, qi, bq=bq, bk=bk, dv=MLA_V_DIM, unroll=unroll)
    for hh, (acc, l) in enumerate(outs):
        o_ref[:, hh * MLA_V_DIM:(hh + 1) * MLA_V_DIM] = (acc * (1.0 / l)).T.astype(o_ref.dtype)


def _mla_attention(q, k_nope, kpe_src, kpe_block, vt, batch, seq):
    m = q.shape[0]
    bq = _pick(seq, 512, CHUNK)
    bk = _pick(bq, 512, CHUNK)
    nq = seq // bq
    hs = MLA_HEADS_PER_STEP
    qw = 2 * LANES
    unroll = _kv_group(seq, bq, bk)
    return pl.pallas_call(
        functools.partial(_mla_attn_kernel, bq=bq, bk=bk, unroll=unroll),
        grid=(batch, MLA_HEADS // hs, nq),
        in_specs=[
            pl.BlockSpec((bq, hs * qw), lambda b, h, i: (b * nq + i, h)),
            pl.BlockSpec((seq, hs * MLA_NOPE_DIM), lambda b, h, i: (b, h)),
            pl.BlockSpec((seq, LANES), lambda b, h, i: (b, kpe_block)),
            pl.BlockSpec((hs * MLA_V_DIM, seq), lambda b, h, i: (h, b)),
        ],
        out_specs=pl.BlockSpec((bq, hs * MLA_V_DIM), lambda b, h, i: (b * nq + i, h)),
        out_shape=jax.ShapeDtypeStruct((m, MLA_WIDTH), MXU_DTYPE),
        scratch_shapes=[pltpu.VMEM((hs, seq, qw), MXU_DTYPE)],
        compiler_params=_cparams("parallel", "parallel", "arbitrary"),
        name="mla_attention",
    )(q, k_nope, kpe_src, vt)


def _merge_kernel(h_ref, od_ref, om_ref, wga_ref, wgb_ref, ba_ref, bb_ref, wd_ref, wm_ref, o_ref, *, sub, rows):
    for r in range(h_ref.shape[0] // rows):
        rs = pl.ds(r * rows, rows)
        h, od, om = h_ref[rs, :], od_ref[rs, :], om_ref[rs, :]
        for t in range(wd_ref.shape[1] // sub):
            cols = slice(t * sub, (t + 1) * sub)
            ga = jax.nn.sigmoid(jnp.dot(h, wga_ref[:, cols], preferred_element_type=F32) + ba_ref[:, cols])
            gb = jax.nn.sigmoid(jnp.dot(h, wgb_ref[:, cols], preferred_element_type=F32) + bb_ref[:, cols])
            a = jnp.dot(od, wd_ref[:, cols], preferred_element_type=F32)
            b = jnp.dot(om, wm_ref[:, cols], preferred_element_type=F32)
            o_ref[rs, cols] = (ga * a + gb * b).astype(o_ref.dtype)


def _merge(h, o_diff, o_mla, w_gate, b_gate, w_d, w_m):
    m, d = h.shape
    n = w_d.shape[1]
    bm = _pick(m, 1024, BF16_SUBLANES)
    bn = _pick(n, 512)
    nb = n // bn
    lo = lambda i, j: (0, j)
    hi = lambda i, j: (0, j + nb)
    row = lambda i, j: (i, 0)
    return pl.pallas_call(
        functools.partial(_merge_kernel, sub=_pick(bn, MXU_COLS), rows=_pick(bm, PROJ_ROW_TILE, BF16_SUBLANES)),
        grid=(m // bm, nb),
        in_specs=[
            pl.BlockSpec((bm, d), row),
            pl.BlockSpec((bm, DIFF_WIDTH), row),
            pl.BlockSpec((bm, MLA_WIDTH), row),
            pl.BlockSpec((d, bn), lo),
            pl.BlockSpec((d, bn), hi),
            pl.BlockSpec((1, bn), lo),
            pl.BlockSpec((1, bn), hi),
            pl.BlockSpec((DIFF_WIDTH, bn), lo),
            pl.BlockSpec((MLA_WIDTH, bn), lo),
        ],
        out_specs=pl.BlockSpec((bm, bn), lambda i, j: (i, j)),
        out_shape=jax.ShapeDtypeStruct((m, n), MXU_DTYPE),
        compiler_params=_cparams("parallel", "arbitrary"),
        name="branch_merge",
    )(h, o_diff, o_mla, w_gate, w_gate, b_gate[None, :], b_gate[None, :], w_d, w_m)


def _residual_norm(y, x_ref, gp_ref, gn_ref, xo_ref, ho_ref):
    xn = x_ref[...] + _rms(y, gp_ref[...])
    xo_ref[...] = xn
    ho_ref[...] = _rms(xn, gn_ref[...]).astype(ho_ref.dtype)


def _out_proj_kernel(lhs_ref, w_ref, x_ref, gp_ref, gn_ref, xo_ref, ho_ref, *, row_tiles):
    rows = lhs_ref.shape[0] // row_tiles
    for t in range(row_tiles):
        r = pl.ds(t * rows, rows)
        y = jnp.dot(lhs_ref[r, :], w_ref[...], preferred_element_type=F32)
        _residual_norm(y, x_ref.at[r], gp_ref, gn_ref, xo_ref.at[r], ho_ref.at[r])


def _out_proj(lhs, w, x, g_post, g_next):
    m, kdim = lhs.shape
    n = w.shape[1]
    bm = _pick(m, 512, BF16_SUBLANES)
    row = lambda i: (i, 0)
    const = lambda i: (0, 0)
    return pl.pallas_call(
        functools.partial(_out_proj_kernel, row_tiles=2 if bm % (2 * BF16_SUBLANES) == 0 else 1),
        grid=(m // bm,),
        in_specs=[
            pl.BlockSpec((bm, kdim), row),
            pl.BlockSpec((kdim, n), const),
            pl.BlockSpec((bm, n), row),
            pl.BlockSpec((1, n), const),
            pl.BlockSpec((1, n), const),
        ],
        out_specs=[pl.BlockSpec((bm, n), row), pl.BlockSpec((bm, n), row)],
        out_shape=[jax.ShapeDtypeStruct((m, n), F32), jax.ShapeDtypeStruct((m, n), MXU_DTYPE)],
        compiler_params=_cparams("parallel"),
        name="out_proj",
    )(lhs, w, x, g_post[None, :], g_next[None, :])


def _mem_attn_kernel(h_ref, wq_ref, kv_ref, wo_ref, x_ref, gp_ref, gn_ref, xo_ref, ho_ref, o_ref, *, dh, scale):
    width = MEM_HEADS * dh
    h = h_ref[...]
    for hd in range(MEM_HEADS):
        cols = slice(hd * dh, (hd + 1) * dh)
        q = (jnp.dot(h, wq_ref[:, cols], preferred_element_type=F32) * scale).astype(MXU_DTYPE)
        k = kv_ref[:, cols]
        v = kv_ref[:, width + hd * dh:width + (hd + 1) * dh]
        s = lax.dot_general(q, k, _NT, preferred_element_type=F32)
        p = jnp.exp(s - s.max(axis=1, keepdims=True))
        l = p.sum(axis=1, keepdims=True)
        o_ref[:, cols] = (jnp.dot(p.astype(MXU_DTYPE), v, preferred_element_type=F32) / l).astype(o_ref.dtype)
    row_tiles = 2 if o_ref.shape[0] % (2 * BF16_SUBLANES) == 0 else 1
    rows = o_ref.shape[0] // row_tiles
    for t in range(row_tiles):
        r = pl.ds(t * rows, rows)
        y = jnp.dot(o_ref[r, :], wo_ref[...], preferred_element_type=F32)
        _residual_norm(y, x_ref.at[r], gp_ref, gn_ref, xo_ref.at[r], ho_ref.at[r])


def _mem_attention(h, w_q, kv, w_o, x, g_post, g_next, seq, mem_tokens):
    m, d = h.shape
    dh = d // MEM_HEADS
    bq = _pick(seq, 512, BF16_SUBLANES)
    per_batch = seq // bq
    row = lambda i: (i, 0)
    const = lambda i: (0, 0)
    resident = functools.partial(pl.BlockSpec, index_map=const, pipeline_mode=pl.Buffered(1))
    return pl.pallas_call(
        functools.partial(_mem_attn_kernel, dh=dh, scale=dh ** -0.5),
        grid=(m // bq,),
        in_specs=[pl.BlockSpec((bq, d), row),
                  resident((d, d)),
                  pl.BlockSpec((mem_tokens, 2 * d), lambda i: (i // per_batch, 0)),
                  resident((d, d)),
                  pl.BlockSpec((bq, d), row),
                  pl.BlockSpec((1, d), const),
                  pl.BlockSpec((1, d), const)],
        out_specs=[pl.BlockSpec((bq, d), row), pl.BlockSpec((bq, d), row)],
        out_shape=[jax.ShapeDtypeStruct((m, d), F32), jax.ShapeDtypeStruct((m, d), MXU_DTYPE)],
        scratch_shapes=[pltpu.VMEM((bq, d), MXU_DTYPE)],
        compiler_params=_cparams("parallel"),
        name="mem_attention",
    )(h, w_q, kv, w_o, x, g_post[None, :], g_next[None, :])


HALO = BF16_SUBLANES


def _ffn_kernel(h_ref, halo_ref, wa_ref, wg_ref, cwa_ref, cwg_ref, cba_ref, cbg_ref, wd_ref, x_ref, gp_ref,
                gn_ref, xo_ref, ho_ref, hs_ref, u_ref, acc_ref, *, bm, blocks_per_seq, nb, sub, row_tiles):
    i = pl.program_id(0)
    j = pl.program_id(1)

    @pl.when(j == 0)
    def _():
        halo = halo_ref[...]
        first = (i % blocks_per_seq) == 0
        hs_ref[:HALO, :] = jnp.where(first, jnp.zeros_like(halo), halo)
        hs_ref[HALO:, :] = h_ref[...]
        acc_ref[...] = jnp.zeros_like(acc_ref)

    hs = hs_ref[...]

    tiles = [slice(t * sub, (t + 1) * sub) for t in range(wa_ref.shape[1] // sub)]
    for t, cols in enumerate(tiles):
        u_ref[2 * t] = jnp.dot(hs, wa_ref[:, cols], preferred_element_type=F32)
        u_ref[2 * t + 1] = jnp.dot(hs, wg_ref[:, cols], preferred_element_type=F32)

    def conv(slot, cols, row0, rows, cw_ref, cb_ref):
        cw = cw_ref[:, cols]
        c = cb_ref[:, cols]
        for tap in range(CONV_WIDTH):
            c = c + u_ref[slot, pl.ds(row0 + HALO - (CONV_WIDTH - 1) + tap, rows), :] * cw[tap:tap + 1]
        return c

    rows = bm // row_tiles
    for r in range(row_tiles):
        acts = []
        for t, cols in enumerate(tiles):
            a = conv(2 * t, cols, r * rows, rows, cwa_ref, cba_ref)
            g = conv(2 * t + 1, cols, r * rows, rows, cwg_ref, cbg_ref)
            acts.append((a * jax.nn.sigmoid(a) * g).astype(MXU_DTYPE))
        acc_ref[pl.ds(r * rows, rows), :] += jnp.dot(jnp.concatenate(acts, axis=1), wd_ref[...],
                                                     preferred_element_type=F32)

    @pl.when(j == nb - 1)
    def _():
        _residual_norm(acc_ref[...], x_ref, gp_ref, gn_ref, xo_ref, ho_ref)


def _ffn(h, w_up, conv_w, conv_b, w_down, x, g_post, g_next, seq):
    m, d = h.shape
    ff = w_up.shape[1] // 2
    bm = _pick(seq, 512, HALO)
    bn = _pick(ff, 512)
    sub = _pick(bn, MXU_COLS)
    nb = ff // bn
    halo_blocks = bm // HALO
    row = lambda i, j: (i, 0)
    const = lambda i, j: (0, 0)
    up_a = lambda i, j: (0, j)
    up_g = lambda i, j: (0, j + nb)
    return pl.pallas_call(
        functools.partial(_ffn_kernel, bm=bm, blocks_per_seq=seq // bm, nb=nb, sub=sub,
                          row_tiles=2 if bm % (2 * BF16_SUBLANES) == 0 else 1),
        grid=(m // bm, nb),
        in_specs=[
            pl.BlockSpec((bm, d), row),
            pl.BlockSpec((HALO, d), lambda i, j: (jnp.maximum(i * halo_blocks - 1, 0), 0)),
            pl.BlockSpec((d, bn), up_a),
            pl.BlockSpec((d, bn), up_g),
            pl.BlockSpec((CONV_WIDTH, bn), up_a),
            pl.BlockSpec((CONV_WIDTH, bn), up_g),
            pl.BlockSpec((1, bn), up_a),
            pl.BlockSpec((1, bn), up_g),
            pl.BlockSpec((bn, d), lambda i, j: (j, 0)),
            pl.BlockSpec((bm, d), row),
            pl.BlockSpec((1, d), const),
            pl.BlockSpec((1, d), const),
        ],
        out_specs=[pl.BlockSpec((bm, d), row), pl.BlockSpec((bm, d), row)],
        out_shape=[jax.ShapeDtypeStruct((m, d), F32), jax.ShapeDtypeStruct((m, d), MXU_DTYPE)],
        scratch_shapes=[pltpu.VMEM((HALO + bm, d), MXU_DTYPE),
                        pltpu.VMEM((2 * (bn // sub), HALO + bm, sub), F32),
                        pltpu.VMEM((bm, d), F32)],
        compiler_params=_cparams("parallel", "arbitrary"),
        name="ffn",
    )(h, h, w_up, w_up, conv_w, conv_w, conv_b[None, :], conv_b[None, :], w_down, x,
      g_post[None, :], g_next[None, :])


def kernel(x, mem, positions, g_pre_mix, w_in, b_gate, lam_q1, lam_k1, lam_q2, lam_k2, g_diff_sub, g_cq,
           w_uq, g_ckv, w_ukv, w_br_diff, w_br_mla, w_mix_out, g_post_mix, g_pre_x, g_mem, w_q_x, w_kv_x,
           w_o_x, g_post_x, g_pre_ffn, w_up, conv_w, conv_b, w_down, g_post_ffn):
    batch, seq, d = x.shape
    mem_tokens = mem.shape[1]
    depth = w_in.shape[0]
    m = batch * seq
    cast = lambda a: a.astype(MXU_DTYPE)

    qw = DIFF_HEADS * 2 * DIFF_QK_DIM
    o_v = 2 * qw
    o_cq = o_v + DIFF_WIDTH
    o_ckv = o_cq + MLA_Q_RANK
    o_kr = o_ckv + MLA_KV_RANK
    o_gt = o_kr + MLA_ROPE_DIM
    assert w_in.shape[2] == o_gt + 2 * d
    assert MLA_Q_RANK % MLA_KV_RANK == 0
    qhead = MLA_NOPE_DIM + MLA_ROPE_DIM
    lat_width = -(-(o_kr - o_cq + LANES) // MXU_COLS) * MXU_COLS
    kvhead = MLA_NOPE_DIM + MLA_V_DIM

    cos, sin = _rope_tables(positions)
    xf = x.reshape(m, d)
    memf = mem.reshape(batch * mem_tokens, d)
    h = _rmsnorm(xf, g_pre_mix[0])
    w_qk_all, w_vt_all, w_lat_all, w_gate_all = _split_w_in(
        w_in, lat_width, qw=qw, o_v=o_v, o_cq=o_cq, o_kr=o_kr, o_gt=o_gt)

    for l in range(depth):
        lam_init = 0.8 - 0.6 * math.exp(-0.3 * l)
        w_qk, w_vt, w_lat, w_gate = w_qk_all[l], w_vt_all[l], w_lat_all[l], w_gate_all[l]
        w_uq_h = jnp.pad(w_uq[l].reshape(MLA_Q_RANK, MLA_HEADS, qhead),
                         ((0, 0), (0, 0), (0, 2 * LANES - qhead)))
        w_uq_p = cast(jnp.concatenate([w_uq_h[:, :, :MLA_NOPE_DIM], _pair_halves(w_uq_h[:, :, MLA_NOPE_DIM:])],
                                      axis=2).reshape(MLA_Q_RANK, -1))
        w_ukv_h = w_ukv[l].reshape(MLA_KV_RANK, MLA_HEADS, kvhead)
        w_uk = cast(w_ukv_h[:, :, :MLA_NOPE_DIM].reshape(MLA_KV_RANK, -1))
        w_uvt = cast(w_ukv_h[:, :, MLA_NOPE_DIM:].reshape(MLA_KV_RANK, -1).T)

        bn_qk = _pick(2 * qw, 512)
        qk = _proj(h, w_qk, functools.partial(_ep_rope, q_tiles=qw // bn_qk), name="proj_qk",
                   extras=(cos, sin), extra_specs=_rope_specs)
        vt = _proj_t(h, w_vt, name="proj_vt")
        lat = _proj(h, w_lat, functools.partial(_ep_rope, rope_from=o_kr - o_cq), name="proj_latent",
                    extras=(cos, sin), extra_specs=_rope_specs, bn_pref=lat_width)
        q_mla = _latent_q(lat, 0, g_cq[l], w_uq_p, cos, sin, qhead ** -0.5 * LOG2E)
        k_mla, vt_mla = _latent_kv(lat, MLA_Q_RANK // MLA_KV_RANK, g_ckv[l], w_uk, w_uvt)
        lam_vecs = jnp.stack([lam_q1[l], lam_k1[l], lam_q2[l], lam_k2[l]])
        o_diff = _diff_attention(qk, vt, lam_vecs, g_diff_sub[l], lam_init, batch, seq)
        o_mla = _mla_attention(q_mla, k_mla, lat, (o_kr - o_cq) // LANES, vt_mla, batch, seq)
        merged = _merge(h, o_diff, o_mla, w_gate, b_gate[l], cast(w_br_diff[l]), cast(w_br_mla[l]))
        xf, h = _out_proj(merged, cast(w_mix_out[l]), xf, g_post_mix[l], g_pre_x[l])

        kv_x = _proj(_rmsnorm(memf, g_mem[l]), cast(w_kv_x[l]), _ep_plain, name="proj_kvx")
        xf, h = _mem_attention(h, cast(w_q_x[l]), kv_x, cast(w_o_x[l]), xf, g_post_x[l], g_pre_ffn[l],
                               seq, mem_tokens)

        g_next = g_pre_mix[l + 1] if l + 1 < depth else g_pre_mix[0]
        xf, h = _ffn(h, cast(w_up[l]), conv_w[l], conv_b[l], cast(w_down[l]), xf, g_post_ffn[l], g_next, seq)

    return xf.reshape(batch, seq, d)
```

```python
import functools
import math

import jax
import jax.numpy as jnp
from jax import lax
from jax.experimental import pallas as pl
from jax.experimental.pallas import tpu as pltpu

CHUNK = 64
ROPE_THETA = 10000.0
EPS = 1e-6
DIFF_HEADS = 8
DIFF_QK_DIM = 64
DIFF_V_DIM = 128
MLA_HEADS = 8
MLA_Q_RANK = 512
MLA_KV_RANK = 256
MLA_NOPE_DIM = 128
MLA_ROPE_DIM = 64
MLA_V_DIM = 128
MEM_HEADS = 4
CONV_WIDTH = 3
DIFF_WIDTH = DIFF_HEADS * DIFF_V_DIM
MLA_WIDTH = MLA_HEADS * MLA_V_DIM

LANES = 128
BF16_SUBLANES = 16
MXU_COLS = 256
VMEM_LIMIT_BYTES = 56 * 2**20

MXU_DTYPE = jnp.bfloat16
F32 = jnp.float32
LOG2E = math.log2(math.e)


def _pick(n, pref, mult=LANES):
    if n <= pref:
        return n
    best = None
    for d in range(mult, pref + 1, mult):
        if n % d == 0:
            best = d
    assert best is not None, (n, pref, mult)
    return best


def _cparams(*sem):
    return pltpu.CompilerParams(dimension_semantics=sem, vmem_limit_bytes=VMEM_LIMIT_BYTES)


def _layered(w):
    return w if isinstance(w, tuple) else (w, None)


def _wshape(w):
    return _layered(w)[0].shape[-2:]


def _wspec(w, block, index_map, **kwargs):
    layer = _layered(w)[1]
    if layer is None:
        return pl.BlockSpec(block, index_map, **kwargs)
    return pl.BlockSpec((None, *block), lambda *g: (layer, *index_map(*g)), **kwargs)


def _rms(x, g):
    return x * lax.rsqrt(jnp.mean(x * x, axis=-1, keepdims=True) + EPS) * g


def _rope128(t, cos, sin):
    return t * cos + pltpu.roll(t, LANES // 2, 1) * sin


def _pair_halves(w):
    q = LANES // 4
    shape = w.shape
    w = w.reshape(*shape[:-1], shape[-1] // LANES, 2, 2, q)
    return jnp.swapaxes(w, -3, -2).reshape(shape)


_NT = (((1,), (1,)), ((), ()))


def _w_in_kernel(w_ref, qk_ref, vt_ref, lat_ref, gate_ref, *, qw, o_v, o_cq, o_kr, o_gt):
    x = w_ref[...]
    q = LANES // 4
    qk = x[:, :o_v]
    col = lax.broadcasted_iota(jnp.int32, qk.shape, 1)
    qk = jnp.where(col < qw, qk * (DIFF_QK_DIM ** -0.5), qk)
    lane = col % LANES
    qk = jnp.where((lane >= q) & (lane < 2 * q), pltpu.roll(qk, o_v - q, 1),
                   jnp.where((lane >= 2 * q) & (lane < 3 * q), pltpu.roll(qk, q, 1), qk))
    qk_ref[...] = qk.astype(qk_ref.dtype)
    vt_ref[...] = x[:, o_v:o_cq].T.astype(vt_ref.dtype)
    t = x[:, o_kr:o_kr + LANES]
    lane = lax.broadcasted_iota(jnp.int32, t.shape, 1)
    kpe = jnp.where(lane < q, t, jnp.where((lane >= 2 * q) & (lane < 3 * q), pltpu.roll(t, q, 1), 0.0))
    pad = jnp.zeros((x.shape[0], lat_ref.shape[1] - (o_kr - o_cq) - LANES), F32)
    lat_ref[...] = jnp.concatenate([x[:, o_cq:o_kr], kpe, pad], axis=1).astype(lat_ref.dtype)
    gate_ref[...] = x[:, o_gt:].astype(gate_ref.dtype)


def _split_w_in(w_in, lat_width, *, qw, o_v, o_cq, o_kr, o_gt):
    depth, d, n_in = w_in.shape
    assert MLA_ROPE_DIM == 2 * (LANES // 4) and o_kr % LANES == 0 and o_kr + LANES <= n_in
    rows = _pick(d, 256, LANES)
    block = lambda width: pl.BlockSpec((None, rows, width), lambda l, r: (l, r, 0))
    return pl.pallas_call(
        functools.partial(_w_in_kernel, qw=qw, o_v=o_v, o_cq=o_cq, o_kr=o_kr, o_gt=o_gt),
        grid=(depth, d // rows),
        in_specs=[block(n_in)],
        out_specs=[block(o_v), pl.BlockSpec((None, o_cq - o_v, rows), lambda l, r: (l, 0, r)),
                   block(lat_width), block(n_in - o_gt)],
        out_shape=[jax.ShapeDtypeStruct((depth, d, o_v), MXU_DTYPE),
                   jax.ShapeDtypeStruct((depth, o_cq - o_v, d), MXU_DTYPE),
                   jax.ShapeDtypeStruct((depth, d, lat_width), MXU_DTYPE),
                   jax.ShapeDtypeStruct((depth, d, n_in - o_gt), MXU_DTYPE)],
        compiler_params=_cparams("parallel", "parallel"),
        name="split_w_in",
    )(w_in)


def _tables_kernel(pos_ref, inv_ref, sgn_ref, cos_ref, sin_ref):
    ang = pos_ref[...].astype(F32) * inv_ref[...]
    cos_ref[...] = jnp.cos(ang)
    sin_ref[...] = jnp.sin(ang) * sgn_ref[...]


def _rope_tables(positions):
    m = positions.size
    d = DIFF_QK_DIM
    inv = ROPE_THETA ** (-jnp.arange(0, d, 2, dtype=F32) / d)
    inv128 = jnp.tile(inv, LANES // (d // 2))[None, :]
    sgn128 = jnp.concatenate([-jnp.ones((LANES // 2,), F32), jnp.ones((LANES // 2,), F32)])[None, :]
    bm = _pick(m, 2048, 8)
    row = pl.BlockSpec((bm, LANES), lambda i: (i, 0))
    const = pl.BlockSpec((1, LANES), lambda i: (0, 0))
    return pl.pallas_call(
        _tables_kernel,
        grid=(m // bm,),
        in_specs=[pl.BlockSpec((bm, 1), lambda i: (i, 0)), const, const],
        out_specs=[row, row],
        out_shape=[jax.ShapeDtypeStruct((m, LANES), F32)] * 2,
        compiler_params=_cparams("parallel"),
        name="rope_tables",
    )(positions.reshape(m, 1), inv128, sgn128)


def _rmsnorm_kernel(x_ref, g_ref, o_ref):
    o_ref[...] = _rms(x_ref[...], g_ref[...]).astype(o_ref.dtype)


def _rmsnorm(x, g):
    m, d = x.shape
    bm = _pick(m, 512, 8)
    return pl.pallas_call(
        _rmsnorm_kernel,
        grid=(m // bm,),
        in_specs=[pl.BlockSpec((bm, d), lambda i: (i, 0)), pl.BlockSpec((1, d), lambda i: (0, 0))],
        out_specs=pl.BlockSpec((bm, d), lambda i: (i, 0)),
        out_shape=jax.ShapeDtypeStruct((m, d), MXU_DTYPE),
        compiler_params=_cparams("parallel"),
        name="rmsnorm",
    )(x, g[None, :])


PROJ_ROW_TILE = 256


def _proj_kernel(h_ref, w_ref, *rest, epilogue, sub, rows):
    *extra, o_ref = rest
    for r in range(h_ref.shape[0] // rows):
        rs = pl.ds(r * rows, rows)
        h = h_ref[rs, :]
        for t in range(w_ref.shape[1] // sub):
            cols = slice(t * sub, (t + 1) * sub)
            acc = jnp.dot(h, w_ref[:, cols], preferred_element_type=F32)
            o_ref[rs, cols] = epilogue(acc, rs, cols, *extra).astype(o_ref.dtype)


def _ep_plain(acc, rs, cols):
    return acc


def _ep_rope(acc, rs, cols, cos_ref, sin_ref, *, q_tiles=0, rope_from=0):
    if cols.start < rope_from:
        return acc
    cos, sin = cos_ref[rs, :], sin_ref[rs, :]
    n = acc.shape[1] // LANES
    out = jnp.concatenate(
        [_rope128(acc[:, c * LANES:(c + 1) * LANES], cos, sin) for c in range(n)], axis=1)
    if q_tiles:
        out = out * jnp.where(pl.program_id(1) < q_tiles, LOG2E, 1.0)
    return out


def _proj(h, w, epilogue, *, name, extras=(), extra_specs=None, bm_pref=1024, bn_pref=512):
    m, k = h.shape
    n = _wshape(w)[1]
    bm = _pick(m, bm_pref, BF16_SUBLANES)
    bn = _pick(n, bn_pref)
    specs = [pl.BlockSpec((bm, k), lambda i, j: (i, 0)), _wspec(w, (k, bn), lambda i, j: (0, j))]
    specs += list(extra_specs(bm, bn)) if extra_specs else []
    return pl.pallas_call(
        functools.partial(_proj_kernel, epilogue=epilogue, sub=_pick(bn, MXU_COLS),
                          rows=_pick(bm, PROJ_ROW_TILE, BF16_SUBLANES)),
        grid=(m // bm, n // bn),
        in_specs=specs,
        out_specs=pl.BlockSpec((bm, bn), lambda i, j: (i, j)),
        out_shape=jax.ShapeDtypeStruct((m, n), MXU_DTYPE),
        compiler_params=_cparams("parallel", "arbitrary"),
        name=name,
    )(h, _layered(w)[0], *extras)


def _rope_specs(bm, bn):
    tab = pl.BlockSpec((bm, LANES), lambda i, j: (i, 0))
    return [tab, tab]


def _proj_t_kernel(h_ref, wt_ref, o_ref):
    o_ref[...] = lax.dot_general(wt_ref[...], h_ref[...], _NT,
                                 preferred_element_type=F32).astype(o_ref.dtype)


def _proj_t(h, wt, *, name):
    m, k = h.shape
    n = _wshape(wt)[0]
    bm = _pick(m, 1024)
    bn = _pick(n, 512, BF16_SUBLANES)
    return pl.pallas_call(
        _proj_t_kernel,
        grid=(m // bm, n // bn),
        in_specs=[pl.BlockSpec((bm, k), lambda i, j: (i, 0)), _wspec(wt, (bn, k), lambda i, j: (j, 0))],
        out_specs=pl.BlockSpec((bn, bm), lambda i, j: (j, i)),
        out_shape=jax.ShapeDtypeStruct((n, m), MXU_DTYPE),
        compiler_params=_cparams("parallel", "arbitrary"),
        name=name,
    )(h, _layered(wt)[0])


def _latent_q_kernel(c_ref, g_ref, w_ref, cos_ref, sin_ref, o_ref, *, scale):
    cn = _rms(c_ref[...].astype(F32), g_ref[...]).astype(MXU_DTYPE)
    acc = jnp.dot(cn, w_ref[...], preferred_element_type=F32)
    cos, sin = cos_ref[...], sin_ref[...]
    cols = []
    for c in range(acc.shape[1] // LANES):
        t = acc[:, c * LANES:(c + 1) * LANES]
        cols.append(_rope128(t, cos, sin) if c % 2 else t)
    o_ref[...] = (jnp.concatenate(cols, axis=1) * scale).astype(o_ref.dtype)


def _latent_kv_kernel(c_ref, g_ref, wk_ref, wvt_ref, k_ref, vt_ref):
    cn = _rms(c_ref[...].astype(F32), g_ref[...]).astype(MXU_DTYPE)
    k_ref[...] = jnp.dot(cn, wk_ref[...], preferred_element_type=F32).astype(k_ref.dtype)
    vt_ref[...] = lax.dot_general(wvt_ref[...], cn, _NT, preferred_element_type=F32).astype(vt_ref.dtype)


def _latent_q(src, col_block, g, w, cos, sin, scale):
    m = src.shape[0]
    rank, n = w.shape
    bm = _pick(m, 512, BF16_SUBLANES)
    tab = pl.BlockSpec((bm, LANES), lambda i: (i, 0))
    return pl.pallas_call(
        functools.partial(_latent_q_kernel, scale=scale),
        grid=(m // bm,),
        in_specs=[pl.BlockSpec((bm, rank), lambda i: (i, col_block)),
                  pl.BlockSpec((1, rank), lambda i: (0, 0)),
                  pl.BlockSpec((rank, n), lambda i: (0, 0)), tab, tab],
        out_specs=pl.BlockSpec((bm, n), lambda i: (i, 0)),
        out_shape=jax.ShapeDtypeStruct((m, n), MXU_DTYPE),
        compiler_params=_cparams("parallel"),
        name="mla_q",
    )(src, g[None, :], w, cos, sin)


def _latent_kv(src, col_block, g, wk, wvt):
    m = src.shape[0]
    rank, nk = wk.shape
    nv = wvt.shape[0]
    bm = _pick(m, 512)
    return pl.pallas_call(
        _latent_kv_kernel,
        grid=(m // bm,),
        in_specs=[pl.BlockSpec((bm, rank), lambda i: (i, col_block)),
                  pl.BlockSpec((1, rank), lambda i: (0, 0)),
                  pl.BlockSpec((rank, nk), lambda i: (0, 0)),
                  pl.BlockSpec((nv, rank), lambda i: (0, 0))],
        out_specs=[pl.BlockSpec((bm, nk), lambda i: (i, 0)), pl.BlockSpec((nv, bm), lambda i: (0, i))],
        out_shape=[jax.ShapeDtypeStruct((m, nk), MXU_DTYPE), jax.ShapeDtypeStruct((nv, m), MXU_DTYPE)],
        compiler_params=_cparams("parallel"),
        name="mla_kv",
    )(src, g[None, :], wk, wvt)


def _flash(q_parts, load_k, load_vt, qi, *, bq, bk, dv, unroll):
    diag = bq // bk
    assert unroll % diag == 0
    rel = (lax.broadcasted_iota(jnp.int32, (bk, bq), 0) // CHUNK
           - lax.broadcasted_iota(jnp.int32, (bk, bq), 1) // CHUNK)

    def group(jg, carries, count, masked):
        blocks = [jg * unroll + u for u in range(count)]
        scores = [[lax.dot_general(load_k(j, part), q, _NT, preferred_element_type=F32)
                   for part, q in enumerate(q_parts)] for j in blocks]
        carries = list(carries)
        for u, j in enumerate(blocks):
            d = u - (count - masked)
            for part, s in enumerate(scores[u]):
                m, l, acc = carries[part]
                if d >= 0:
                    s = jnp.where(rel <= -d * (bk // CHUNK), s, -jnp.inf)
                m_new = jnp.maximum(m, s.max(axis=0, keepdims=True))
                alpha = jnp.exp2(m - m_new)
                p = jnp.exp2(s - m_new)
                l = alpha * l + p.sum(axis=0, keepdims=True)
                acc = alpha * acc + jnp.dot(load_vt(j, part), p.astype(MXU_DTYPE),
                                            preferred_element_type=F32)
                carries[part] = (m_new, l, acc)
        return tuple(carries)

    init = (jnp.full((1, bq), -jnp.inf, F32), jnp.zeros((1, bq), F32), jnp.zeros((dv, bq), F32))
    carries = tuple(init for _ in q_parts)
    n_blocks = (qi + 1) * diag
    n_groups = (n_blocks + unroll - 1) // unroll
    carries = lax.fori_loop(0, n_groups - 1, lambda jg, c: group(jg, c, unroll, 0), carries)
    last = (n_blocks - (n_groups - 1) * unroll) // diag - 1
    tails = [functools.partial(group, count=(t + 1) * diag, masked=diag) for t in range(unroll // diag)]
    carries = lax.switch(last, [lambda c, f=f: f(n_groups - 1, c) for f in tails], carries)
    return [(acc, l) for _, l, acc in carries]


KV_GROUP_KEYS = 1024


def _kv_group(seq, bq, bk):
    group = max(bq // bk, min(KV_GROUP_KEYS // bk, seq // bk))
    assert (seq // bk) % group == 0 and group % (bq // bk) == 0
    return group


DIFF_HEADS_PER_STEP = 2


def _diff_attn_kernel(lam_ref, q_ref, k_ref, vt_ref, g_ref, o_ref, *, bq, bk, lam_init, unroll):
    qi = pl.program_id(2)
    lam_v = lam_ref[...]
    lam = (jnp.exp(jnp.sum(lam_v[0:1] * lam_v[1:2], axis=1, keepdims=True))
           - jnp.exp(jnp.sum(lam_v[2:3] * lam_v[3:4], axis=1, keepdims=True)) + lam_init)
    hw = LANES
    is_q1 = (lax.broadcasted_iota(jnp.int32, (bq, hw), 1) & (DIFF_QK_DIM // 2)) == 0
    q_parts = []
    for hh in range(DIFF_HEADS_PER_STEP):
        q = q_ref[:, hh * hw:(hh + 1) * hw]
        zero = jnp.zeros_like(q)
        q_parts += [jnp.where(is_q1, q, zero), jnp.where(is_q1, zero, q)]

    def load_k(j, part):
        hh = part // 2
        return k_ref[pl.ds(pl.multiple_of(j * bk, bk), bk), hh * hw:(hh + 1) * hw]

    def load_vt(j, part):
        hh = part // 2
        return vt_ref[hh * DIFF_V_DIM:(hh + 1) * DIFF_V_DIM, pl.ds(pl.multiple_of(j * bk, bk), bk)]

    outs = _flash(q_parts, load_k, load_vt, qi, bq=bq, bk=bk, dv=DIFF_V_DIM, unroll=unroll)
    for hh in range(DIFF_HEADS_PER_STEP):
        (acc1, l1), (acc2, l2) = outs[2 * hh], outs[2 * hh + 1]
        o = acc1 * (1.0 / l1) - lam * (acc2 * (1.0 / l2))
        o = o * lax.rsqrt(jnp.mean(o * o, axis=0, keepdims=True) + EPS) * g_ref[...]
        o_ref[:, hh * DIFF_V_DIM:(hh + 1) * DIFF_V_DIM] = (o * (1.0 - lam_init)).T.astype(o_ref.dtype)


def _diff_attention(qk, vt, lam_vecs, g_sub, lam_init, batch, seq):
    m = qk.shape[0]
    bq = _pick(seq, 512, CHUNK)
    bk = _pick(bq, 512, CHUNK)
    nq = seq // bq
    hs = DIFF_HEADS_PER_STEP
    steps = DIFF_HEADS // hs
    unroll = _kv_group(seq, bq, bk)
    return pl.pallas_call(
        functools.partial(_diff_attn_kernel, bq=bq, bk=bk, lam_init=lam_init, unroll=unroll),
        grid=(batch, steps, nq),
        in_specs=[
            pl.BlockSpec((4, DIFF_QK_DIM), lambda b, h, i: (0, 0)),
            pl.BlockSpec((bq, hs * LANES), lambda b, h, i: (b * nq + i, h)),
            pl.BlockSpec((seq, hs * LANES), lambda b, h, i: (b, steps + h)),
            pl.BlockSpec((hs * DIFF_V_DIM, seq), lambda b, h, i: (h, b)),
            pl.BlockSpec((DIFF_V_DIM, 1), lambda b, h, i: (0, 0)),
        ],
        out_specs=pl.BlockSpec((bq, hs * DIFF_V_DIM), lambda b, h, i: (b * nq + i, h)),
        out_shape=jax.ShapeDtypeStruct((m, DIFF_WIDTH), MXU_DTYPE),
        compiler_params=_cparams("parallel", "parallel", "arbitrary"),
        name="diff_attention",
    )(lam_vecs, qk, qk, vt, g_sub[:, None])


MLA_HEADS_PER_STEP = 4


def _mla_attn_kernel(q_ref, kn_ref, kpe_ref, vt_ref, o_ref, kcat_ref, *, bq, bk, unroll):
    qi = pl.program_id(2)
    qw = 2 * LANES

    @pl.when(qi == 0)
    def _():
        for hh in range(MLA_HEADS_PER_STEP):
            kcat_ref[hh, :, :MLA_NOPE_DIM] = kn_ref[:, hh * MLA_NOPE_DIM:(hh + 1) * MLA_NOPE_DIM]
            kcat_ref[hh, :, MLA_NOPE_DIM:] = kpe_ref[...]

    def load_k(j, part):
        return kcat_ref[part, pl.ds(pl.multiple_of(j * bk, bk), bk), :]

    def load_vt(j, part):
        return vt_ref[part * MLA_V_DIM:(part + 1) * MLA_V_DIM, pl.ds(pl.multiple_of(j * bk, bk), bk)]

    q_parts = [q_ref[:, hh * qw:(hh + 1) * qw] for hh in range(MLA_HEADS_PER_STEP)]
    outs = _flash(q_parts, load_k, load_vt, qi, bq=bq, bk=bk, dv=MLA_V_DIM, unroll=unroll)
    for hh, (acc, l) in enumerate(outs):
        o_ref[:, hh * MLA_V_DIM:(hh + 1) * MLA_V_DIM] = (acc * (1.0 / l)).T.astype(o_ref.dtype)


def _mla_attention(q, k_nope, kpe_src, kpe_block, vt, batch, seq):
    m = q.shape[0]
    bq = _pick(seq, 512, CHUNK)
    bk = _pick(bq, 512, CHUNK)
    nq = seq // bq
    hs = MLA_HEADS_PER_STEP
    qw = 2 * LANES
    unroll = _kv_group(seq, bq, bk)
    return pl.pallas_call(
        functools.partial(_mla_attn_kernel, bq=bq, bk=bk, unroll=unroll),
        grid=(batch, MLA_HEADS // hs, nq),
        in_specs=[
            pl.BlockSpec((bq, hs * qw), lambda b, h, i: (b * nq + i, h)),
            pl.BlockSpec((seq, hs * MLA_NOPE_DIM), lambda b, h, i: (b, h)),
            pl.BlockSpec((seq, LANES), lambda b, h, i: (b, kpe_block)),
            pl.BlockSpec((hs * MLA_V_DIM, seq), lambda b, h, i: (h, b)),
        ],
        out_specs=pl.BlockSpec((bq, hs * MLA_V_DIM), lambda b, h, i: (b * nq + i, h)),
        out_shape=jax.ShapeDtypeStruct((m, MLA_WIDTH), MXU_DTYPE),
        scratch_shapes=[pltpu.VMEM((hs, seq, qw), MXU_DTYPE)],
        compiler_params=_cparams("parallel", "parallel", "arbitrary"),
        name="mla_attention",
    )(q, k_nope, kpe_src, vt)


def _merge_kernel(h_ref, od_ref, om_ref, wga_ref, wgb_ref, ba_ref, bb_ref, wd_ref, wm_ref, o_ref, *, sub, rows):
    for r in range(h_ref.shape[0] // rows):
        rs = pl.ds(r * rows, rows)
        h, od, om = h_ref[rs, :], od_ref[rs, :], om_ref[rs, :]
        for t in range(wd_ref.shape[1] // sub):
            cols = slice(t * sub, (t + 1) * sub)
            ga = jax.nn.sigmoid(jnp.dot(h, wga_ref[:, cols], preferred_element_type=F32) + ba_ref[:, cols])
            gb = jax.nn.sigmoid(jnp.dot(h, wgb_ref[:, cols], preferred_element_type=F32) + bb_ref[:, cols])
            a = jnp.dot(od, wd_ref[:, cols], preferred_element_type=F32)
            b = jnp.dot(om, wm_ref[:, cols], preferred_element_type=F32)
            o_ref[rs, cols] = (ga * a + gb * b).astype(o_ref.dtype)


def _merge(h, o_diff, o_mla, w_gate, b_gate, w_d, w_m):
    m, d = h.shape
    n = _wshape(w_d)[1]
    bm = _pick(m, 1024, BF16_SUBLANES)
    bn = _pick(n, 512)
    nb = n // bn
    lo = lambda i, j: (0, j)
    hi = lambda i, j: (0, j + nb)
    row = lambda i, j: (i, 0)
    return pl.pallas_call(
        functools.partial(_merge_kernel, sub=_pick(bn, MXU_COLS), rows=_pick(bm, PROJ_ROW_TILE, BF16_SUBLANES)),
        grid=(m // bm, nb),
        in_specs=[
            pl.BlockSpec((bm, d), row),
            pl.BlockSpec((bm, DIFF_WIDTH), row),
            pl.BlockSpec((bm, MLA_WIDTH), row),
            _wspec(w_gate, (d, bn), lo),
            _wspec(w_gate, (d, bn), hi),
            pl.BlockSpec((1, bn), lo),
            pl.BlockSpec((1, bn), hi),
            _wspec(w_d, (DIFF_WIDTH, bn), lo),
            _wspec(w_m, (MLA_WIDTH, bn), lo),
        ],
        out_specs=pl.BlockSpec((bm, bn), lambda i, j: (i, j)),
        out_shape=jax.ShapeDtypeStruct((m, n), MXU_DTYPE),
        compiler_params=_cparams("parallel", "arbitrary"),
        name="branch_merge",
    )(h, o_diff, o_mla, _layered(w_gate)[0], _layered(w_gate)[0], b_gate[None, :], b_gate[None, :],
      _layered(w_d)[0], _layered(w_m)[0])


def _residual_norm(y, x_ref, gp_ref, gn_ref, xo_ref, ho_ref):
    xn = x_ref[...] + _rms(y, gp_ref[...])
    xo_ref[...] = xn
    ho_ref[...] = _rms(xn, gn_ref[...]).astype(ho_ref.dtype)


def _out_proj_kernel(lhs_ref, w_ref, x_ref, gp_ref, gn_ref, xo_ref, ho_ref, *, row_tiles):
    rows = lhs_ref.shape[0] // row_tiles
    for t in range(row_tiles):
        r = pl.ds(t * rows, rows)
        y = jnp.dot(lhs_ref[r, :], w_ref[...], preferred_element_type=F32)
        _residual_norm(y, x_ref.at[r], gp_ref, gn_ref, xo_ref.at[r], ho_ref.at[r])


def _out_proj(lhs, w, x, g_post, g_next):
    m, kdim = lhs.shape
    n = _wshape(w)[1]
    bm = _pick(m, 512, BF16_SUBLANES)
    row = lambda i: (i, 0)
    const = lambda i: (0, 0)
    return pl.pallas_call(
        functools.partial(_out_proj_kernel, row_tiles=2 if bm % (2 * BF16_SUBLANES) == 0 else 1),
        grid=(m // bm,),
        in_specs=[
            pl.BlockSpec((bm, kdim), row),
            _wspec(w, (kdim, n), const),
            pl.BlockSpec((bm, n), row),
            pl.BlockSpec((1, n), const),
            pl.BlockSpec((1, n), const),
        ],
        out_specs=[pl.BlockSpec((bm, n), row), pl.BlockSpec((bm, n), row)],
        out_shape=[jax.ShapeDtypeStruct((m, n), F32), jax.ShapeDtypeStruct((m, n), MXU_DTYPE)],
        compiler_params=_cparams("parallel"),
        name="out_proj",
    )(lhs, _layered(w)[0], x, g_post[None, :], g_next[None, :])


def _mem_attn_kernel(h_ref, wq_ref, kv_ref, wo_ref, x_ref, gp_ref, gn_ref, xo_ref, ho_ref, o_ref, *, dh, scale):
    width = MEM_HEADS * dh
    h = h_ref[...]
    for hd in range(MEM_HEADS):
        cols = slice(hd * dh, (hd + 1) * dh)
        q = (jnp.dot(h, wq_ref[:, cols], preferred_element_type=F32) * scale).astype(MXU_DTYPE)
        k = kv_ref[:, cols]
        v = kv_ref[:, width + hd * dh:width + (hd + 1) * dh]
        s = lax.dot_general(q, k, _NT, preferred_element_type=F32)
        p = jnp.exp(s - s.max(axis=1, keepdims=True))
        l = p.sum(axis=1, keepdims=True)
        o_ref[:, cols] = (jnp.dot(p.astype(MXU_DTYPE), v, preferred_element_type=F32) / l).astype(o_ref.dtype)
    row_tiles = 2 if o_ref.shape[0] % (2 * BF16_SUBLANES) == 0 else 1
    rows = o_ref.shape[0] // row_tiles
    for t in range(row_tiles):
        r = pl.ds(t * rows, rows)
        y = jnp.dot(o_ref[r, :], wo_ref[...], preferred_element_type=F32)
        _residual_norm(y, x_ref.at[r], gp_ref, gn_ref, xo_ref.at[r], ho_ref.at[r])


def _mem_attention(h, w_q, kv, w_o, x, g_post, g_next, seq, mem_tokens):
    m, d = h.shape
    dh = d // MEM_HEADS
    bq = _pick(seq, 512, BF16_SUBLANES)
    per_batch = seq // bq
    row = lambda i: (i, 0)
    const = lambda i: (0, 0)
    resident = lambda w: _wspec(w, (d, d), const, pipeline_mode=pl.Buffered(1))
    return pl.pallas_call(
        functools.partial(_mem_attn_kernel, dh=dh, scale=dh ** -0.5),
        grid=(m // bq,),
        in_specs=[pl.BlockSpec((bq, d), row),
                  resident(w_q),
                  pl.BlockSpec((mem_tokens, 2 * d), lambda i: (i // per_batch, 0)),
                  resident(w_o),
                  pl.BlockSpec((bq, d), row),
                  pl.BlockSpec((1, d), const),
                  pl.BlockSpec((1, d), const)],
        out_specs=[pl.BlockSpec((bq, d), row), pl.BlockSpec((bq, d), row)],
        out_shape=[jax.ShapeDtypeStruct((m, d), F32), jax.ShapeDtypeStruct((m, d), MXU_DTYPE)],
        scratch_shapes=[pltpu.VMEM((bq, d), MXU_DTYPE)],
        compiler_params=_cparams("parallel"),
        name="mem_attention",
    )(h, _layered(w_q)[0], kv, _layered(w_o)[0], x, g_post[None, :], g_next[None, :])


HALO = BF16_SUBLANES


def _ffn_kernel(h_ref, halo_ref, wa_ref, wg_ref, cwa_ref, cwg_ref, cba_ref, cbg_ref, wd_ref, x_ref, gp_ref,
                gn_ref, xo_ref, ho_ref, hs_ref, u_ref, acc_ref, *, bm, blocks_per_seq, nb, sub, row_tiles):
    i = pl.program_id(0)
    j = pl.program_id(1)

    @pl.when(j == 0)
    def _():
        halo = halo_ref[...]
        first = (i % blocks_per_seq) == 0
        hs_ref[:HALO, :] = jnp.where(first, jnp.zeros_like(halo), halo)
        hs_ref[HALO:, :] = h_ref[...]
        acc_ref[...] = jnp.zeros_like(acc_ref)

    hs = hs_ref[...]

    tiles = [slice(t * sub, (t + 1) * sub) for t in range(wa_ref.shape[1] // sub)]
    for t, cols in enumerate(tiles):
        u_ref[2 * t] = jnp.dot(hs, wa_ref[:, cols], preferred_element_type=F32)
        u_ref[2 * t + 1] = jnp.dot(hs, wg_ref[:, cols], preferred_element_type=F32)

    def conv(slot, cols, row0, rows, cw_ref, cb_ref):
        cw = cw_ref[:, cols]
        c = cb_ref[:, cols]
        for tap in range(CONV_WIDTH):
            c = c + u_ref[slot, pl.ds(row0 + HALO - (CONV_WIDTH - 1) + tap, rows), :] * cw[tap:tap + 1]
        return c

    rows = bm // row_tiles
    for r in range(row_tiles):
        acts = []
        for t, cols in enumerate(tiles):
            a = conv(2 * t, cols, r * rows, rows, cwa_ref, cba_ref)
            g = conv(2 * t + 1, cols, r * rows, rows, cwg_ref, cbg_ref)
            acts.append((a * jax.nn.sigmoid(a) * g).astype(MXU_DTYPE))
        acc_ref[pl.ds(r * rows, rows), :] += jnp.dot(jnp.concatenate(acts, axis=1), wd_ref[...],
                                                     preferred_element_type=F32)

    @pl.when(j == nb - 1)
    def _():
        _residual_norm(acc_ref[...], x_ref, gp_ref, gn_ref, xo_ref, ho_ref)


def _ffn(h, w_up, conv_w, conv_b, w_down, x, g_post, g_next, seq):
    m, d = h.shape
    ff = _wshape(w_up)[1] // 2
    bm = _pick(seq, 512, HALO)
    bn = _pick(ff, 512)
    sub = _pick(bn, MXU_COLS)
    nb = ff // bn
    halo_blocks = bm // HALO
    row = lambda i, j: (i, 0)
    const = lambda i, j: (0, 0)
    up_a = lambda i, j: (0, j)
    up_g = lambda i, j: (0, j + nb)
    return pl.pallas_call(
        functools.partial(_ffn_kernel, bm=bm, blocks_per_seq=seq // bm, nb=nb, sub=sub,
                          row_tiles=2 if bm % (2 * BF16_SUBLANES) == 0 else 1),
        grid=(m // bm, nb),
        in_specs=[
            pl.BlockSpec((bm, d), row),
            pl.BlockSpec((HALO, d), lambda i, j: (jnp.maximum(i * halo_blocks - 1, 0), 0)),
            _wspec(w_up, (d, bn), up_a),
            _wspec(w_up, (d, bn), up_g),
            pl.BlockSpec((CONV_WIDTH, bn), up_a),
            pl.BlockSpec((CONV_WIDTH, bn), up_g),
            pl.BlockSpec((1, bn), up_a),
            pl.BlockSpec((1, bn), up_g),
            _wspec(w_down, (bn, d), lambda i, j: (j, 0)),
            pl.BlockSpec((bm, d), row),
            pl.BlockSpec((1, d), const),
            pl.BlockSpec((1, d), const),
        ],
        out_specs=[pl.BlockSpec((bm, d), row), pl.BlockSpec((bm, d), row)],
        out_shape=[jax.ShapeDtypeStruct((m, d), F32), jax.ShapeDtypeStruct((m, d), MXU_DTYPE)],
        scratch_shapes=[pltpu.VMEM((HALO + bm, d), MXU_DTYPE),
                        pltpu.VMEM((2 * (bn // sub), HALO + bm, sub), F32),
                        pltpu.VMEM((bm, d), F32)],
        compiler_params=_cparams("parallel", "arbitrary"),
        name="ffn",
    )(h, h, _layered(w_up)[0], _layered(w_up)[0], conv_w, conv_w, conv_b[None, :], conv_b[None, :],
      _layered(w_down)[0], x, g_post[None, :], g_next[None, :])


def kernel(x, mem, positions, g_pre_mix, w_in, b_gate, lam_q1, lam_k1, lam_q2, lam_k2, g_diff_sub, g_cq,
           w_uq, g_ckv, w_ukv, w_br_diff, w_br_mla, w_mix_out, g_post_mix, g_pre_x, g_mem, w_q_x, w_kv_x,
           w_o_x, g_post_x, g_pre_ffn, w_up, conv_w, conv_b, w_down, g_post_ffn):
    batch, seq, d = x.shape
    mem_tokens = mem.shape[1]
    depth = w_in.shape[0]
    m = batch * seq
    cast = lambda a: a.astype(MXU_DTYPE)

    qw = DIFF_HEADS * 2 * DIFF_QK_DIM
    o_v = 2 * qw
    o_cq = o_v + DIFF_WIDTH
    o_ckv = o_cq + MLA_Q_RANK
    o_kr = o_ckv + MLA_KV_RANK
    o_gt = o_kr + MLA_ROPE_DIM
    assert w_in.shape[2] == o_gt + 2 * d
    assert MLA_Q_RANK % MLA_KV_RANK == 0
    qhead = MLA_NOPE_DIM + MLA_ROPE_DIM
    lat_width = -(-(o_kr - o_cq + LANES) // MXU_COLS) * MXU_COLS
    kvhead = MLA_NOPE_DIM + MLA_V_DIM

    cos, sin = _rope_tables(positions)
    xf = x.reshape(m, d)
    memf = mem.reshape(batch * mem_tokens, d)
    h = _rmsnorm(xf, g_pre_mix[0])
    w_qk_all, w_vt_all, w_lat_all, w_gate_all = _split_w_in(
        w_in, lat_width, qw=qw, o_v=o_v, o_cq=o_cq, o_kr=o_kr, o_gt=o_gt)

    w_br_diff_c, w_br_mla_c, w_mix_out_c = cast(w_br_diff), cast(w_br_mla), cast(w_mix_out)
    w_q_x_c, w_kv_x_c, w_o_x_c = cast(w_q_x), cast(w_kv_x), cast(w_o_x)
    w_up_c, w_down_c = cast(w_up), cast(w_down)

    for l in range(depth):
        lam_init = 0.8 - 0.6 * math.exp(-0.3 * l)
        w_qk, w_vt, w_lat, w_gate = (w_qk_all, l), (w_vt_all, l), (w_lat_all, l), (w_gate_all, l)
        w_uq_h = jnp.pad(w_uq[l].reshape(MLA_Q_RANK, MLA_HEADS, qhead),
                         ((0, 0), (0, 0), (0, 2 * LANES - qhead)))
        w_uq_p = cast(jnp.concatenate([w_uq_h[:, :, :MLA_NOPE_DIM], _pair_halves(w_uq_h[:, :, MLA_NOPE_DIM:])],
                                      axis=2).reshape(MLA_Q_RANK, -1))
        w_ukv_h = w_ukv[l].reshape(MLA_KV_RANK, MLA_HEADS, kvhead)
        w_uk = cast(w_ukv_h[:, :, :MLA_NOPE_DIM].reshape(MLA_KV_RANK, -1))
        w_uvt = cast(w_ukv_h[:, :, MLA_NOPE_DIM:].reshape(MLA_KV_RANK, -1).T)

        bn_qk = _pick(2 * qw, 512)
        qk = _proj(h, w_qk, functools.partial(_ep_rope, q_tiles=qw // bn_qk), name="proj_qk",
                   extras=(cos, sin), extra_specs=_rope_specs)
        vt = _proj_t(h, w_vt, name="proj_vt")
        lat = _proj(h, w_lat, functools.partial(_ep_rope, rope_from=o_kr - o_cq), name="proj_latent",
                    extras=(cos, sin), extra_specs=_rope_specs, bn_pref=lat_width)
        q_mla = _latent_q(lat, 0, g_cq[l], w_uq_p, cos, sin, qhead ** -0.5 * LOG2E)
        k_mla, vt_mla = _latent_kv(lat, MLA_Q_RANK // MLA_KV_RANK, g_ckv[l], w_uk, w_uvt)
        lam_vecs = jnp.stack([lam_q1[l], lam_k1[l], lam_q2[l], lam_k2[l]])
        o_diff = _diff_attention(qk, vt, lam_vecs, g_diff_sub[l], lam_init, batch, seq)
        o_mla = _mla_attention(q_mla, k_mla, lat, (o_kr - o_cq) // LANES, vt_mla, batch, seq)
        merged = _merge(h, o_diff, o_mla, w_gate, b_gate[l], (w_br_diff_c, l), (w_br_mla_c, l))
        xf, h = _out_proj(merged, (w_mix_out_c, l), xf, g_post_mix[l], g_pre_x[l])

        kv_x = _proj(_rmsnorm(memf, g_mem[l]), (w_kv_x_c, l), _ep_plain, name="proj_kvx")
        xf, h = _mem_attention(h, (w_q_x_c, l), kv_x, (w_o_x_c, l), xf, g_post_x[l], g_pre_ffn[l],
                               seq, mem_tokens)

        g_next = g_pre_mix[l + 1] if l + 1 < depth else g_pre_mix[0]
        xf, h = _ffn(h, (w_up_c, l), conv_w[l], conv_b[l], (w_down_c, l), xf, g_post_ffn[l], g_next, seq)

    return xf.reshape(batch, seq, d)
```

```python
import functools
import math

import jax
import jax.numpy as jnp
from jax import lax
from jax.experimental import pallas as pl
from jax.experimental.pallas import tpu as pltpu

CHUNK = 64
ROPE_THETA = 10000.0
EPS = 1e-6
DIFF_HEADS = 8
DIFF_QK_DIM = 64
DIFF_V_DIM = 128
MLA_HEADS = 8
MLA_Q_RANK = 512
MLA_KV_RANK = 256
MLA_NOPE_DIM = 128
MLA_ROPE_DIM = 64
MLA_V_DIM = 128
MEM_HEADS = 4
CONV_WIDTH = 3
DIFF_WIDTH = DIFF_HEADS * DIFF_V_DIM
MLA_WIDTH = MLA_HEADS * MLA_V_DIM

LANES = 128
BF16_SUBLANES = 16
MXU_COLS = 256
VMEM_LIMIT_BYTES = 56 * 2**20

MXU_DTYPE = jnp.bfloat16
F32 = jnp.float32
LOG2E = math.log2(math.e)


def _pick(n, pref, mult=LANES):
    if n <= pref:
        return n
    best = None
    for d in range(mult, pref + 1, mult):
        if n % d == 0:
            best = d
    assert best is not None, (n, pref, mult)
    return best


def _cparams(*sem):
    return pltpu.CompilerParams(dimension_semantics=sem, vmem_limit_bytes=VMEM_LIMIT_BYTES)


def _layered(w):
    return w if isinstance(w, tuple) else (w, None)


def _wshape(w):
    return _layered(w)[0].shape[-2:]


def _wspec(w, block, index_map, **kwargs):
    layer = _layered(w)[1]
    if layer is None:
        return pl.BlockSpec(block, index_map, **kwargs)
    return pl.BlockSpec((None, *block), lambda *g: (layer, *index_map(*g)), **kwargs)


def _rms(x, g):
    return x * lax.rsqrt(jnp.mean(x * x, axis=-1, keepdims=True) + EPS) * g


def _rope128(t, cos, sin):
    return t * cos + pltpu.roll(t, LANES // 2, 1) * sin


def _pair_halves(w):
    q = LANES // 4
    shape = w.shape
    w = w.reshape(*shape[:-1], shape[-1] // LANES, 2, 2, q)
    return jnp.swapaxes(w, -3, -2).reshape(shape)


_NT = (((1,), (1,)), ((), ()))


def _w_in_kernel(w_ref, qk_ref, vt_ref, lat_ref, gate_ref, *, qw, o_v, o_cq, o_kr, o_gt):
    x = w_ref[...]
    q = LANES // 4
    qk = x[:, :o_v]
    col = lax.broadcasted_iota(jnp.int32, qk.shape, 1)
    qk = jnp.where(col < qw, qk * (DIFF_QK_DIM ** -0.5), qk)
    lane = col % LANES
    qk = jnp.where((lane >= q) & (lane < 2 * q), pltpu.roll(qk, o_v - q, 1),
                   jnp.where((lane >= 2 * q) & (lane < 3 * q), pltpu.roll(qk, q, 1), qk))
    qk_ref[...] = qk.astype(qk_ref.dtype)
    vt_ref[...] = x[:, o_v:o_cq].T.astype(vt_ref.dtype)
    t = x[:, o_kr:o_kr + LANES]
    lane = lax.broadcasted_iota(jnp.int32, t.shape, 1)
    kpe = jnp.where(lane < q, t, jnp.where((lane >= 2 * q) & (lane < 3 * q), pltpu.roll(t, q, 1), 0.0))
    pad = jnp.zeros((x.shape[0], lat_ref.shape[1] - (o_kr - o_cq) - LANES), F32)
    lat_ref[...] = jnp.concatenate([x[:, o_cq:o_kr], kpe, pad], axis=1).astype(lat_ref.dtype)
    gate_ref[...] = x[:, o_gt:].astype(gate_ref.dtype)


def _split_w_in(w_in, lat_width, *, qw, o_v, o_cq, o_kr, o_gt):
    depth, d, n_in = w_in.shape
    assert MLA_ROPE_DIM == 2 * (LANES // 4) and o_kr % LANES == 0 and o_kr + LANES <= n_in
    rows = _pick(d, 256, LANES)
    block = lambda width: pl.BlockSpec((None, rows, width), lambda l, r: (l, r, 0))
    return pl.pallas_call(
        functools.partial(_w_in_kernel, qw=qw, o_v=o_v, o_cq=o_cq, o_kr=o_kr, o_gt=o_gt),
        grid=(depth, d // rows),
        in_specs=[block(n_in)],
        out_specs=[block(o_v), pl.BlockSpec((None, o_cq - o_v, rows), lambda l, r: (l, 0, r)),
                   block(lat_width), block(n_in - o_gt)],
        out_shape=[jax.ShapeDtypeStruct((depth, d, o_v), MXU_DTYPE),
                   jax.ShapeDtypeStruct((depth, o_cq - o_v, d), MXU_DTYPE),
                   jax.ShapeDtypeStruct((depth, d, lat_width), MXU_DTYPE),
                   jax.ShapeDtypeStruct((depth, d, n_in - o_gt), MXU_DTYPE)],
        compiler_params=_cparams("parallel", "parallel"),
        name="split_w_in",
    )(w_in)


def _tables_kernel(pos_ref, inv_ref, sgn_ref, cos_ref, sin_ref):
    ang = pos_ref[...].astype(F32) * inv_ref[...]
    cos_ref[...] = jnp.cos(ang)
    sin_ref[...] = jnp.sin(ang) * sgn_ref[...]


def _rope_tables(positions):
    m = positions.size
    d = DIFF_QK_DIM
    inv = ROPE_THETA ** (-jnp.arange(0, d, 2, dtype=F32) / d)
    inv128 = jnp.tile(inv, LANES // (d // 2))[None, :]
    sgn128 = jnp.concatenate([-jnp.ones((LANES // 2,), F32), jnp.ones((LANES // 2,), F32)])[None, :]
    bm = _pick(m, 2048, 8)
    row = pl.BlockSpec((bm, LANES), lambda i: (i, 0))
    const = pl.BlockSpec((1, LANES), lambda i: (0, 0))
    return pl.pallas_call(
        _tables_kernel,
        grid=(m // bm,),
        in_specs=[pl.BlockSpec((bm, 1), lambda i: (i, 0)), const, const],
        out_specs=[row, row],
        out_shape=[jax.ShapeDtypeStruct((m, LANES), F32)] * 2,
        compiler_params=_cparams("parallel"),
        name="rope_tables",
    )(positions.reshape(m, 1), inv128, sgn128)


def _rmsnorm_kernel(x_ref, g_ref, o_ref):
    o_ref[...] = _rms(x_ref[...], g_ref[...]).astype(o_ref.dtype)


def _rmsnorm(x, g):
    m, d = x.shape
    bm = _pick(m, 512, 8)
    return pl.pallas_call(
        _rmsnorm_kernel,
        grid=(m // bm,),
        in_specs=[pl.BlockSpec((bm, d), lambda i: (i, 0)), pl.BlockSpec((1, d), lambda i: (0, 0))],
        out_specs=pl.BlockSpec((bm, d), lambda i: (i, 0)),
        out_shape=jax.ShapeDtypeStruct((m, d), MXU_DTYPE),
        compiler_params=_cparams("parallel"),
        name="rmsnorm",
    )(x, g[None, :])


PROJ_ROW_TILE = 256


def _proj_kernel(h_ref, w_ref, *rest, epilogue, sub, rows):
    *extra, o_ref = rest
    for r in range(h_ref.shape[0] // rows):
        rs = pl.ds(r * rows, rows)
        h = h_ref[rs, :]
        for t in range(w_ref.shape[1] // sub):
            cols = slice(t * sub, (t + 1) * sub)
            acc = jnp.dot(h, w_ref[:, cols], preferred_element_type=F32)
            o_ref[rs, cols] = epilogue(acc, rs, cols, *extra).astype(o_ref.dtype)


def _ep_plain(acc, rs, cols):
    return acc


def _ep_rope(acc, rs, cols, cos_ref, sin_ref, *, q_tiles=0, rope_from=0):
    if cols.start < rope_from:
        return acc
    cos, sin = cos_ref[rs, :], sin_ref[rs, :]
    n = acc.shape[1] // LANES
    out = jnp.concatenate(
        [_rope128(acc[:, c * LANES:(c + 1) * LANES], cos, sin) for c in range(n)], axis=1)
    if q_tiles:
        out = out * jnp.where(pl.program_id(1) < q_tiles, LOG2E, 1.0)
    return out


def _proj(h, w, epilogue, *, name, extras=(), extra_specs=None, bm_pref=1024, bn_pref=512):
    m, k = h.shape
    n = _wshape(w)[1]
    bm = _pick(m, bm_pref, BF16_SUBLANES)
    bn = _pick(n, bn_pref)
    specs = [pl.BlockSpec((bm, k), lambda i, j: (i, 0)), _wspec(w, (k, bn), lambda i, j: (0, j))]
    specs += list(extra_specs(bm, bn)) if extra_specs else []
    return pl.pallas_call(
        functools.partial(_proj_kernel, epilogue=epilogue, sub=_pick(bn, MXU_COLS),
                          rows=_pick(bm, PROJ_ROW_TILE, BF16_SUBLANES)),
        grid=(m // bm, n // bn),
        in_specs=specs,
        out_specs=pl.BlockSpec((bm, bn), lambda i, j: (i, j)),
        out_shape=jax.ShapeDtypeStruct((m, n), MXU_DTYPE),
        compiler_params=_cparams("parallel", "arbitrary"),
        name=name,
    )(h, _layered(w)[0], *extras)


def _rope_specs(bm, bn):
    tab = pl.BlockSpec((bm, LANES), lambda i, j: (i, 0))
    return [tab, tab]


def _proj_t_kernel(h_ref, wt_ref, o_ref):
    o_ref[...] = lax.dot_general(wt_ref[...], h_ref[...], _NT,
                                 preferred_element_type=F32).astype(o_ref.dtype)


def _proj_t(h, wt, *, name):
    m, k = h.shape
    n = _wshape(wt)[0]
    bm = _pick(m, 1024)
    bn = _pick(n, 512, BF16_SUBLANES)
    return pl.pallas_call(
        _proj_t_kernel,
        grid=(m // bm, n // bn),
        in_specs=[pl.BlockSpec((bm, k), lambda i, j: (i, 0)), _wspec(wt, (bn, k), lambda i, j: (j, 0))],
        out_specs=pl.BlockSpec((bn, bm), lambda i, j: (j, i)),
        out_shape=jax.ShapeDtypeStruct((n, m), MXU_DTYPE),
        compiler_params=_cparams("parallel", "arbitrary"),
        name=name,
    )(h, _layered(wt)[0])


def _latent_q_kernel(c_ref, g_ref, w_ref, cos_ref, sin_ref, o_ref, *, scale):
    cn = _rms(c_ref[...].astype(F32), g_ref[...]).astype(MXU_DTYPE)
    acc = jnp.dot(cn, w_ref[...], preferred_element_type=F32)
    cos, sin = cos_ref[...], sin_ref[...]
    cols = []
    for c in range(acc.shape[1] // LANES):
        t = acc[:, c * LANES:(c + 1) * LANES]
        cols.append(_rope128(t, cos, sin) if c % 2 else t)
    o_ref[...] = (jnp.concatenate(cols, axis=1) * scale).astype(o_ref.dtype)


def _latent_kv_kernel(c_ref, g_ref, wk_ref, wvt_ref, k_ref, vt_ref):
    cn = _rms(c_ref[...].astype(F32), g_ref[...]).astype(MXU_DTYPE)
    k_ref[...] = jnp.dot(cn, wk_ref[...], preferred_element_type=F32).astype(k_ref.dtype)
    vt_ref[...] = lax.dot_general(wvt_ref[...], cn, _NT, preferred_element_type=F32).astype(vt_ref.dtype)


def _latent_q(src, col_block, g, w, cos, sin, scale):
    m = src.shape[0]
    rank, n = w.shape
    bm = _pick(m, 512, BF16_SUBLANES)
    tab = pl.BlockSpec((bm, LANES), lambda i: (i, 0))
    return pl.pallas_call(
        functools.partial(_latent_q_kernel, scale=scale),
        grid=(m // bm,),
        in_specs=[pl.BlockSpec((bm, rank), lambda i: (i, col_block)),
                  pl.BlockSpec((1, rank), lambda i: (0, 0)),
                  pl.BlockSpec((rank, n), lambda i: (0, 0)), tab, tab],
        out_specs=pl.BlockSpec((bm, n), lambda i: (i, 0)),
        out_shape=jax.ShapeDtypeStruct((m, n), MXU_DTYPE),
        compiler_params=_cparams("parallel"),
        name="mla_q",
    )(src, g[None, :], w, cos, sin)


def _latent_kv(src, col_block, g, wk, wvt):
    m = src.shape[0]
    rank, nk = wk.shape
    nv = wvt.shape[0]
    bm = _pick(m, 512)
    return pl.pallas_call(
        _latent_kv_kernel,
        grid=(m // bm,),
        in_specs=[pl.BlockSpec((bm, rank), lambda i: (i, col_block)),
                  pl.BlockSpec((1, rank), lambda i: (0, 0)),
                  pl.BlockSpec((rank, nk), lambda i: (0, 0)),
                  pl.BlockSpec((nv, rank), lambda i: (0, 0))],
        out_specs=[pl.BlockSpec((bm, nk), lambda i: (i, 0)), pl.BlockSpec((nv, bm), lambda i: (0, i))],
        out_shape=[jax.ShapeDtypeStruct((m, nk), MXU_DTYPE), jax.ShapeDtypeStruct((nv, m), MXU_DTYPE)],
        compiler_params=_cparams("parallel"),
        name="mla_kv",
    )(src, g[None, :], wk, wvt)


def _flash(q_parts, load_k, load_vt, qi, *, bq, bk, dv, unroll):
    diag = bq // bk
    assert unroll % diag == 0
    rel = (lax.broadcasted_iota(jnp.int32, (bk, bq), 0) // CHUNK
           - lax.broadcasted_iota(jnp.int32, (bk, bq), 1) // CHUNK)

    def group(jg, carries, count, masked):
        blocks = [jg * unroll + u for u in range(count)]
        scores = [[lax.dot_general(load_k(j, part), q, _NT, preferred_element_type=F32)
                   for part, q in enumerate(q_parts)] for j in blocks]
        carries = list(carries)
        for u, j in enumerate(blocks):
            d = u - (count - masked)
            for part, s in enumerate(scores[u]):
                m, l, acc = carries[part]
                if d >= 0:
                    s = jnp.where(rel <= -d * (bk // CHUNK), s, -jnp.inf)
                m_new = jnp.maximum(m, s.max(axis=0, keepdims=True))
                alpha = jnp.exp2(m - m_new)
                p = jnp.exp2(s - m_new)
                l = alpha * l + p.sum(axis=0, keepdims=True)
                acc = alpha * acc + jnp.dot(load_vt(j, part), p.astype(MXU_DTYPE),
                                            preferred_element_type=F32)
                carries[part] = (m_new, l, acc)
        return tuple(carries)

    init = (jnp.full((1, bq), -jnp.inf, F32), jnp.zeros((1, bq), F32), jnp.zeros((dv, bq), F32))
    carries = tuple(init for _ in q_parts)
    n_blocks = (qi + 1) * diag
    n_groups = (n_blocks + unroll - 1) // unroll
    carries = lax.fori_loop(0, n_groups - 1, lambda jg, c: group(jg, c, unroll, 0), carries)
    last = (n_blocks - (n_groups - 1) * unroll) // diag - 1
    tails = [functools.partial(group, count=(t + 1) * diag, masked=diag) for t in range(unroll // diag)]
    carries = lax.switch(last, [lambda c, f=f: f(n_groups - 1, c) for f in tails], carries)
    return [(acc, l) for _, l, acc in carries]


KV_GROUP_KEYS = 1024


def _kv_group(seq, bq, bk):
    group = max(bq // bk, min(KV_GROUP_KEYS // bk, seq // bk))
    assert (seq // bk) % group == 0 and group % (bq // bk) == 0
    return group


DIFF_HEADS_PER_STEP = 2


def _diff_attn_kernel(lam_ref, q_ref, k_ref, vt_ref, g_ref, o_ref, *, bq, bk, lam_init, unroll):
    qi = pl.program_id(2)
    lam_v = lam_ref[...]
    lam = (jnp.exp(jnp.sum(lam_v[0:1] * lam_v[1:2], axis=1, keepdims=True))
           - jnp.exp(jnp.sum(lam_v[2:3] * lam_v[3:4], axis=1, keepdims=True)) + lam_init)
    hw = LANES
    is_q1 = (lax.broadcasted_iota(jnp.int32, (bq, hw), 1) & (DIFF_QK_DIM // 2)) == 0
    q_parts = []
    for hh in range(DIFF_HEADS_PER_STEP):
        q = q_ref[:, hh * hw:(hh + 1) * hw]
        zero = jnp.zeros_like(q)
        q_parts += [jnp.where(is_q1, q, zero), jnp.where(is_q1, zero, q)]

    def load_k(j, part):
        hh = part // 2
        return k_ref[pl.ds(pl.multiple_of(j * bk, bk), bk), hh * hw:(hh + 1) * hw]

    def load_vt(j, part):
        hh = part // 2
        return vt_ref[hh * DIFF_V_DIM:(hh + 1) * DIFF_V_DIM, pl.ds(pl.multiple_of(j * bk, bk), bk)]

    outs = _flash(q_parts, load_k, load_vt, qi, bq=bq, bk=bk, dv=DIFF_V_DIM, unroll=unroll)
    for hh in range(DIFF_HEADS_PER_STEP):
        (acc1, l1), (acc2, l2) = outs[2 * hh], outs[2 * hh + 1]
        o = acc1 * (1.0 / l1) - lam * (acc2 * (1.0 / l2))
        o = o * lax.rsqrt(jnp.mean(o * o, axis=0, keepdims=True) + EPS) * g_ref[...]
        o_ref[:, hh * DIFF_V_DIM:(hh + 1) * DIFF_V_DIM] = (o * (1.0 - lam_init)).T.astype(o_ref.dtype)


def _diff_attention(qk, vt, lam_vecs, g_sub, lam_init, batch, seq):
    m = qk.shape[0]
    bq = _pick(seq, 512, CHUNK)
    bk = _pick(bq, 512, CHUNK)
    nq = seq // bq
    hs = DIFF_HEADS_PER_STEP
    steps = DIFF_HEADS // hs
    unroll = _kv_group(seq, bq, bk)
    return pl.pallas_call(
        functools.partial(_diff_attn_kernel, bq=bq, bk=bk, lam_init=lam_init, unroll=unroll),
        grid=(batch, steps, nq),
        in_specs=[
            pl.BlockSpec((4, DIFF_QK_DIM), lambda b, h, i: (0, 0)),
            pl.BlockSpec((bq, hs * LANES), lambda b, h, i: (b * nq + i, h)),
            pl.BlockSpec((seq, hs * LANES), lambda b, h, i: (b, steps + h)),
            pl.BlockSpec((hs * DIFF_V_DIM, seq), lambda b, h, i: (h, b)),
            pl.BlockSpec((DIFF_V_DIM, 1), lambda b, h, i: (0, 0)),
        ],
        out_specs=pl.BlockSpec((bq, hs * DIFF_V_DIM), lambda b, h, i: (b * nq + i, h)),
        out_shape=jax.ShapeDtypeStruct((m, DIFF_WIDTH), MXU_DTYPE),
        compiler_params=_cparams("parallel", "parallel", "arbitrary"),
        name="diff_attention",
    )(lam_vecs, qk, qk, vt, g_sub[:, None])


MLA_HEADS_PER_STEP = 4


def _mla_attn_kernel(q_ref, kn_ref, kpe_ref, vt_ref, o_ref, kcat_ref, *, bq, bk, unroll):
    qi = pl.program_id(2)
    qw = 2 * LANES

    @pl.when(qi == 0)
    def _():
        for hh in range(MLA_HEADS_PER_STEP):
            kcat_ref[hh, :, :MLA_NOPE_DIM] = kn_ref[:, hh * MLA_NOPE_DIM:(hh + 1) * MLA_NOPE_DIM]
            kcat_ref[hh, :, MLA_NOPE_DIM:] = kpe_ref[...]

    def load_k(j, part):
        return kcat_ref[part, pl.ds(pl.multiple_of(j * bk, bk), bk), :]

    def load_vt(j, part):
        return vt_ref[part * MLA_V_DIM:(part + 1) * MLA_V_DIM, pl.ds(pl.multiple_of(j * bk, bk), bk)]

    q_parts = [q_ref[:, hh * qw:(hh + 1) * qw] for hh in range(MLA_HEADS_PER_STEP)]
    outs = _flash(q_parts, load_k, load_vt, qi, bq=bq, bk=bk, dv=MLA_V_DIM, unroll=unroll)
    for hh, (acc, l) in enumerate(outs):
        o_ref[:, hh * MLA_V_DIM:(hh + 1) * MLA_V_DIM] = (acc * (1.0 / l)).T.astype(o_ref.dtype)


def _mla_attention(q, k_nope, kpe_src, kpe_block, vt, batch, seq):
    m = q.shape[0]
    bq = _pick(seq, 512, CHUNK)
    bk = _pick(bq, 512, CHUNK)
    nq = seq // bq
    hs = MLA_HEADS_PER_STEP
    qw = 2 * LANES
    unroll = _kv_group(seq, bq, bk)
    return pl.pallas_call(
        functools.partial(_mla_attn_kernel, bq=bq, bk=bk, unroll=unroll),
        grid=(batch, MLA_HEADS // hs, nq),
        in_specs=[
            pl.BlockSpec((bq, hs * qw), lambda b, h, i: (b * nq + i, h)),
            pl.BlockSpec((seq, hs * MLA_NOPE_DIM), lambda b, h, i: (b, h)),
            pl.BlockSpec((seq, LANES), lambda b, h, i: (b, kpe_block)),
            pl.BlockSpec((hs * MLA_V_DIM, seq), lambda b, h, i: (h, b)),
        ],
        out_specs=pl.BlockSpec((bq, hs * MLA_V_DIM), lambda b, h, i: (b * nq + i, h)),
        out_shape=jax.ShapeDtypeStruct((m, MLA_WIDTH), MXU_DTYPE),
        scratch_shapes=[pltpu.VMEM((hs, seq, qw), MXU_DTYPE)],
        compiler_params=_cparams("parallel", "parallel", "arbitrary"),
        name="mla_attention",
    )(q, k_nope, kpe_src, vt)


def _merge_kernel(h_ref, od_ref, om_ref, wga_ref, wgb_ref, ba_ref, bb_ref, wd_ref, wm_ref, o_ref, *, sub, rows):
    for r in range(h_ref.shape[0] // rows):
        rs = pl.ds(r * rows, rows)
        h, od, om = h_ref[rs, :], od_ref[rs, :], om_ref[rs, :]
        for t in range(wd_ref.shape[1] // sub):
            cols = slice(t * sub, (t + 1) * sub)
            ga = jax.nn.sigmoid(jnp.dot(h, wga_ref[:, cols], preferred_element_type=F32) + ba_ref[:, cols])
            gb = jax.nn.sigmoid(jnp.dot(h, wgb_ref[:, cols], preferred_element_type=F32) + bb_ref[:, cols])
            a = jnp.dot(od, wd_ref[:, cols], preferred_element_type=F32)
            b = jnp.dot(om, wm_ref[:, cols], preferred_element_type=F32)
            o_ref[rs, cols] = (ga * a + gb * b).astype(o_ref.dtype)


def _merge(h, o_diff, o_mla, w_gate, b_gate, w_d, w_m):
    m, d = h.shape
    n = _wshape(w_d)[1]
    bm = _pick(m, 1024, BF16_SUBLANES)
    bn = _pick(n, 512)
    nb = n // bn
    lo = lambda i, j: (0, j)
    hi = lambda i, j: (0, j + nb)
    row = lambda i, j: (i, 0)
    return pl.pallas_call(
        functools.partial(_merge_kernel, sub=_pick(bn, MXU_COLS), rows=_pick(bm, PROJ_ROW_TILE, BF16_SUBLANES)),
        grid=(m // bm, nb),
        in_specs=[
            pl.BlockSpec((bm, d), row),
            pl.BlockSpec((bm, DIFF_WIDTH), row),
            pl.BlockSpec((bm, MLA_WIDTH), row),
            _wspec(w_gate, (d, bn), lo),
            _wspec(w_gate, (d, bn), hi),
            pl.BlockSpec((1, bn), lo),
            pl.BlockSpec((1, bn), hi),
            _wspec(w_d, (DIFF_WIDTH, bn), lo),
            _wspec(w_m, (MLA_WIDTH, bn), lo),
        ],
        out_specs=pl.BlockSpec((bm, bn), lambda i, j: (i, j)),
        out_shape=jax.ShapeDtypeStruct((m, n), MXU_DTYPE),
        compiler_params=_cparams("parallel", "arbitrary"),
        name="branch_merge",
    )(h, o_diff, o_mla, _layered(w_gate)[0], _layered(w_gate)[0], b_gate[None, :], b_gate[None, :],
      _layered(w_d)[0], _layered(w_m)[0])


def _residual_norm(y, x_ref, gp_ref, gn_ref, xo_ref, ho_ref):
    xn = x_ref[...] + _rms(y, gp_ref[...])
    xo_ref[...] = xn
    ho_ref[...] = _rms(xn, gn_ref[...]).astype(ho_ref.dtype)


def _out_proj_kernel(lhs_ref, w_ref, x_ref, gp_ref, gn_ref, xo_ref, ho_ref, *, row_tiles):
    rows = lhs_ref.shape[0] // row_tiles
    for t in range(row_tiles):
        r = pl.ds(t * rows, rows)
        y = jnp.dot(lhs_ref[r, :], w_ref[...], preferred_element_type=F32)
        _residual_norm(y, x_ref.at[r], gp_ref, gn_ref, xo_ref.at[r], ho_ref.at[r])


def _out_proj(lhs, w, x, g_post, g_next):
    m, kdim = lhs.shape
    n = _wshape(w)[1]
    bm = _pick(m, 512, BF16_SUBLANES)
    row = lambda i: (i, 0)
    const = lambda i: (0, 0)
    return pl.pallas_call(
        functools.partial(_out_proj_kernel, row_tiles=2 if bm % (2 * BF16_SUBLANES) == 0 else 1),
        grid=(m // bm,),
        in_specs=[
            pl.BlockSpec((bm, kdim), row),
            _wspec(w, (kdim, n), const),
            pl.BlockSpec((bm, n), row),
            pl.BlockSpec((1, n), const),
            pl.BlockSpec((1, n), const),
        ],
        out_specs=[pl.BlockSpec((bm, n), row), pl.BlockSpec((bm, n), row)],
        out_shape=[jax.ShapeDtypeStruct((m, n), F32), jax.ShapeDtypeStruct((m, n), MXU_DTYPE)],
        compiler_params=_cparams("parallel"),
        name="out_proj",
    )(lhs, _layered(w)[0], x, g_post[None, :], g_next[None, :])


def _mem_attn_kernel(h_ref, wq_ref, kv_ref, wo_ref, x_ref, gp_ref, gn_ref, xo_ref, ho_ref, o_ref, *, dh, scale):
    width = MEM_HEADS * dh
    h = h_ref[...]
    for hd in range(MEM_HEADS):
        cols = slice(hd * dh, (hd + 1) * dh)
        q = (jnp.dot(h, wq_ref[:, cols], preferred_element_type=F32) * scale).astype(MXU_DTYPE)
        k = kv_ref[:, cols]
        v = kv_ref[:, width + hd * dh:width + (hd + 1) * dh]
        s = lax.dot_general(q, k, _NT, preferred_element_type=F32)
        p = jnp.exp(s - s.max(axis=1, keepdims=True))
        l = p.sum(axis=1, keepdims=True)
        o_ref[:, cols] = (jnp.dot(p.astype(MXU_DTYPE), v, preferred_element_type=F32) / l).astype(o_ref.dtype)
    row_tiles = 2 if o_ref.shape[0] % (2 * BF16_SUBLANES) == 0 else 1
    rows = o_ref.shape[0] // row_tiles
    for t in range(row_tiles):
        r = pl.ds(t * rows, rows)
        y = jnp.dot(o_ref[r, :], wo_ref[...], preferred_element_type=F32)
        _residual_norm(y, x_ref.at[r], gp_ref, gn_ref, xo_ref.at[r], ho_ref.at[r])


def _mem_attention(h, w_q, kv, w_o, x, g_post, g_next, seq, mem_tokens):
    m, d = h.shape
    dh = d // MEM_HEADS
    bq = _pick(seq, 512, BF16_SUBLANES)
    per_batch = seq // bq
    row = lambda i: (i, 0)
    const = lambda i: (0, 0)
    resident = lambda w: _wspec(w, (d, d), const, pipeline_mode=pl.Buffered(1))
    return pl.pallas_call(
        functools.partial(_mem_attn_kernel, dh=dh, scale=dh ** -0.5),
        grid=(m // bq,),
        in_specs=[pl.BlockSpec((bq, d), row),
                  resident(w_q),
                  pl.BlockSpec((mem_tokens, 2 * d), lambda i: (i // per_batch, 0)),
                  resident(w_o),
                  pl.BlockSpec((bq, d), row),
                  pl.BlockSpec((1, d), const),
                  pl.BlockSpec((1, d), const)],
        out_specs=[pl.BlockSpec((bq, d), row), pl.BlockSpec((bq, d), row)],
        out_shape=[jax.ShapeDtypeStruct((m, d), F32), jax.ShapeDtypeStruct((m, d), MXU_DTYPE)],
        scratch_shapes=[pltpu.VMEM((bq, d), MXU_DTYPE)],
        compiler_params=_cparams("parallel"),
        name="mem_attention",
    )(h, _layered(w_q)[0], kv, _layered(w_o)[0], x, g_post[None, :], g_next[None, :])


HALO = BF16_SUBLANES


def _ffn_kernel(h_ref, halo_ref, wa_ref, wg_ref, cwa_ref, cwg_ref, cba_ref, cbg_ref, wd_ref, x_ref, gp_ref,
                gn_ref, xo_ref, ho_ref, hs_ref, u_ref, acc_ref, *, bm, blocks_per_seq, nb, sub, row_tiles):
    i = pl.program_id(0)
    j = pl.program_id(1)

    @pl.when(j == 0)
    def _():
        halo = halo_ref[...]
        first = (i % blocks_per_seq) == 0
        hs_ref[:HALO, :] = jnp.where(first, jnp.zeros_like(halo), halo)
        hs_ref[HALO:, :] = h_ref[...]
        acc_ref[...] = jnp.zeros_like(acc_ref)

    rows = bm // row_tiles
    tiles = [slice(t * sub, (t + 1) * sub) for t in range(wa_ref.shape[1] // sub)]
    bounds = [0] + [HALO + (r + 1) * rows for r in range(row_tiles)]
    for r in range(row_tiles):
        rs = pl.ds(bounds[r], bounds[r + 1] - bounds[r])
        hs = hs_ref[rs, :]
        for t, cols in enumerate(tiles):
            u_ref[2 * t, rs, :] = jnp.dot(hs, wa_ref[:, cols], preferred_element_type=F32)
            u_ref[2 * t + 1, rs, :] = jnp.dot(hs, wg_ref[:, cols], preferred_element_type=F32)

    def conv(slot, cols, row0, rows, cw_ref, cb_ref):
        cw = cw_ref[:, cols]
        c = cb_ref[:, cols]
        for tap in range(CONV_WIDTH):
            c = c + u_ref[slot, pl.ds(row0 + HALO - (CONV_WIDTH - 1) + tap, rows), :] * cw[tap:tap + 1]
        return c

    for r in range(row_tiles):
        acts = []
        for t, cols in enumerate(tiles):
            a = conv(2 * t, cols, r * rows, rows, cwa_ref, cba_ref)
            g = conv(2 * t + 1, cols, r * rows, rows, cwg_ref, cbg_ref)
            acts.append((a * jax.nn.sigmoid(a) * g).astype(MXU_DTYPE))
        acc_ref[pl.ds(r * rows, rows), :] += jnp.dot(jnp.concatenate(acts, axis=1), wd_ref[...],
                                                     preferred_element_type=F32)

    @pl.when(j == nb - 1)
    def _():
        _residual_norm(acc_ref[...], x_ref, gp_ref, gn_ref, xo_ref, ho_ref)


def _ffn(h, w_up, conv_w, conv_b, w_down, x, g_post, g_next, seq):
    m, d = h.shape
    ff = _wshape(w_up)[1] // 2
    bm = _pick(seq, 512, HALO)
    bn = _pick(ff, 512)
    sub = _pick(bn, MXU_COLS)
    nb = ff // bn
    halo_blocks = bm // HALO
    row = lambda i, j: (i, 0)
    const = lambda i, j: (0, 0)
    up_a = lambda i, j: (0, j)
    up_g = lambda i, j: (0, j + nb)
    return pl.pallas_call(
        functools.partial(_ffn_kernel, bm=bm, blocks_per_seq=seq // bm, nb=nb, sub=sub,
                          row_tiles=2 if bm % (2 * BF16_SUBLANES) == 0 else 1),
        grid=(m // bm, nb),
        in_specs=[
            pl.BlockSpec((bm, d), row),
            pl.BlockSpec((HALO, d), lambda i, j: (jnp.maximum(i * halo_blocks - 1, 0), 0)),
            _wspec(w_up, (d, bn), up_a),
            _wspec(w_up, (d, bn), up_g),
            pl.BlockSpec((CONV_WIDTH, bn), up_a),
            pl.BlockSpec((CONV_WIDTH, bn), up_g),
            pl.BlockSpec((1, bn), up_a),
            pl.BlockSpec((1, bn), up_g),
            _wspec(w_down, (bn, d), lambda i, j: (j, 0)),
            pl.BlockSpec((bm, d), row),
            pl.BlockSpec((1, d), const),
            pl.BlockSpec((1, d), const),
        ],
        out_specs=[pl.BlockSpec((bm, d), row), pl.BlockSpec((bm, d), row)],
        out_shape=[jax.ShapeDtypeStruct((m, d), F32), jax.ShapeDtypeStruct((m, d), MXU_DTYPE)],
        scratch_shapes=[pltpu.VMEM((HALO + bm, d), MXU_DTYPE),
                        pltpu.VMEM((2 * (bn // sub), HALO + bm, sub), F32),
                        pltpu.VMEM((bm, d), F32)],
        compiler_params=_cparams("parallel", "arbitrary"),
        name="ffn",
    )(h, h, _layered(w_up)[0], _layered(w_up)[0], conv_w, conv_w, conv_b[None, :], conv_b[None, :],
      _layered(w_down)[0], x, g_post[None, :], g_next[None, :])


def kernel(x, mem, positions, g_pre_mix, w_in, b_gate, lam_q1, lam_k1, lam_q2, lam_k2, g_diff_sub, g_cq,
           w_uq, g_ckv, w_ukv, w_br_diff, w_br_mla, w_mix_out, g_post_mix, g_pre_x, g_mem, w_q_x, w_kv_x,
           w_o_x, g_post_x, g_pre_ffn, w_up, conv_w, conv_b, w_down, g_post_ffn):
    batch, seq, d = x.shape
    mem_tokens = mem.shape[1]
    depth = w_in.shape[0]
    m = batch * seq
    cast = lambda a: a.astype(MXU_DTYPE)

    qw = DIFF_HEADS * 2 * DIFF_QK_DIM
    o_v = 2 * qw
    o_cq = o_v + DIFF_WIDTH
    o_ckv = o_cq + MLA_Q_RANK
    o_kr = o_ckv + MLA_KV_RANK
    o_gt = o_kr + MLA_ROPE_DIM
    assert w_in.shape[2] == o_gt + 2 * d
    assert MLA_Q_RANK % MLA_KV_RANK == 0
    qhead = MLA_NOPE_DIM + MLA_ROPE_DIM
    lat_width = -(-(o_kr - o_cq + LANES) // MXU_COLS) * MXU_COLS
    kvhead = MLA_NOPE_DIM + MLA_V_DIM

    cos, sin = _rope_tables(positions)
    xf = x.reshape(m, d)
    memf = mem.reshape(batch * mem_tokens, d)
    h = _rmsnorm(xf, g_pre_mix[0])
    w_qk_all, w_vt_all, w_lat_all, w_gate_all = _split_w_in(
        w_in, lat_width, qw=qw, o_v=o_v, o_cq=o_cq, o_kr=o_kr, o_gt=o_gt)

    w_br_diff_c, w_br_mla_c, w_mix_out_c = cast(w_br_diff), cast(w_br_mla), cast(w_mix_out)
    w_q_x_c, w_kv_x_c, w_o_x_c = cast(w_q_x), cast(w_kv_x), cast(w_o_x)
    w_up_c, w_down_c = cast(w_up), cast(w_down)

    for l in range(depth):
        lam_init = 0.8 - 0.6 * math.exp(-0.3 * l)
        w_qk, w_vt, w_lat, w_gate = (w_qk_all, l), (w_vt_all, l), (w_lat_all, l), (w_gate_all, l)
        w_uq_h = jnp.pad(w_uq[l].reshape(MLA_Q_RANK, MLA_HEADS, qhead),
                         ((0, 0), (0, 0), (0, 2 * LANES - qhead)))
        w_uq_p = cast(jnp.concatenate([w_uq_h[:, :, :MLA_NOPE_DIM], _pair_halves(w_uq_h[:, :, MLA_NOPE_DIM:])],
                                      axis=2).reshape(MLA_Q_RANK, -1))
        w_ukv_h = w_ukv[l].reshape(MLA_KV_RANK, MLA_HEADS, kvhead)
        w_uk = cast(w_ukv_h[:, :, :MLA_NOPE_DIM].reshape(MLA_KV_RANK, -1))
        w_uvt = cast(w_ukv_h[:, :, MLA_NOPE_DIM:].reshape(MLA_KV_RANK, -1).T)

        bn_qk = _pick(2 * qw, 512)
        qk = _proj(h, w_qk, functools.partial(_ep_rope, q_tiles=qw // bn_qk), name="proj_qk",
                   extras=(cos, sin), extra_specs=_rope_specs)
        vt = _proj_t(h, w_vt, name="proj_vt")
        lat = _proj(h, w_lat, functools.partial(_ep_rope, rope_from=o_kr - o_cq), name="proj_latent",
                    extras=(cos, sin), extra_specs=_rope_specs, bn_pref=lat_width)
        q_mla = _latent_q(lat, 0, g_cq[l], w_uq_p, cos, sin, qhead ** -0.5 * LOG2E)
        k_mla, vt_mla = _latent_kv(lat, MLA_Q_RANK // MLA_KV_RANK, g_ckv[l], w_uk, w_uvt)
        lam_vecs = jnp.stack([lam_q1[l], lam_k1[l], lam_q2[l], lam_k2[l]])
        o_diff = _diff_attention(qk, vt, lam_vecs, g_diff_sub[l], lam_init, batch, seq)
        o_mla = _mla_attention(q_mla, k_mla, lat, (o_kr - o_cq) // LANES, vt_mla, batch, seq)
        merged = _merge(h, o_diff, o_mla, w_gate, b_gate[l], (w_br_diff_c, l), (w_br_mla_c, l))
        xf, h = _out_proj(merged, (w_mix_out_c, l), xf, g_post_mix[l], g_pre_x[l])

        kv_x = _proj(_rmsnorm(memf, g_mem[l]), (w_kv_x_c, l), _ep_plain, name="proj_kvx")
        xf, h = _mem_attention(h, (w_q_x_c, l), kv_x, (w_o_x_c, l), xf, g_post_x[l], g_pre_ffn[l],
                               seq, mem_tokens)

        g_next = g_pre_mix[l + 1] if l + 1 < depth else g_pre_mix[0]
        xf, h = _ffn(h, (w_up_c, l), conv_w[l], conv_b[l], (w_down_c, l), xf, g_post_ffn[l], g_next, seq)

    return xf.reshape(batch, seq, d)
```

```python
import functools
import math

import jax
import jax.numpy as jnp
from jax import lax
from jax.experimental import pallas as pl
from jax.experimental.pallas import tpu as pltpu

CHUNK = 64
ROPE_THETA = 10000.0
EPS = 1e-6
DIFF_HEADS = 8
DIFF_QK_DIM = 64
DIFF_V_DIM = 128
MLA_HEADS = 8
MLA_Q_RANK = 512
MLA_KV_RANK = 256
MLA_NOPE_DIM = 128
MLA_ROPE_DIM = 64
MLA_V_DIM = 128
MEM_HEADS = 4
CONV_WIDTH = 3
DIFF_WIDTH = DIFF_HEADS * DIFF_V_DIM
MLA_WIDTH = MLA_HEADS * MLA_V_DIM

LANES = 128
BF16_SUBLANES = 16
MXU_COLS = 256
VMEM_LIMIT_BYTES = 56 * 2**20

MXU_DTYPE = jnp.bfloat16
F32 = jnp.float32
LOG2E = math.log2(math.e)


def _pick(n, pref, mult=LANES):
    if n <= pref:
        return n
    best = None
    for d in range(mult, pref + 1, mult):
        if n % d == 0:
            best = d
    assert best is not None, (n, pref, mult)
    return best


def _cparams(*sem):
    return pltpu.CompilerParams(dimension_semantics=sem, vmem_limit_bytes=VMEM_LIMIT_BYTES)


def _layered(w):
    return w if isinstance(w, tuple) else (w, None)


def _wshape(w):
    return _layered(w)[0].shape[-2:]


def _wspec(w, block, index_map, **kwargs):
    layer = _layered(w)[1]
    if layer is None:
        return pl.BlockSpec(block, index_map, **kwargs)
    return pl.BlockSpec((None, *block), lambda *g: (layer, *index_map(*g)), **kwargs)


def _rms(x, g):
    return x * lax.rsqrt(jnp.mean(x * x, axis=-1, keepdims=True) + EPS) * g


def _rope128(t, cos, sin):
    return t * cos + pltpu.roll(t, LANES // 2, 1) * sin


def _pair_halves(w):
    q = LANES // 4
    shape = w.shape
    w = w.reshape(*shape[:-1], shape[-1] // LANES, 2, 2, q)
    return jnp.swapaxes(w, -3, -2).reshape(shape)


_NT = (((1,), (1,)), ((), ()))


def _w_in_kernel(w_ref, qk_ref, vt_ref, lat_ref, gate_ref, *, qw, o_v, o_cq, o_kr, o_gt):
    x = w_ref[...]
    q = LANES // 4
    qk = x[:, :o_v]
    col = lax.broadcasted_iota(jnp.int32, qk.shape, 1)
    qk = jnp.where(col < qw, qk * (DIFF_QK_DIM ** -0.5), qk)
    lane = col % LANES
    qk = jnp.where((lane >= q) & (lane < 2 * q), pltpu.roll(qk, o_v - q, 1),
                   jnp.where((lane >= 2 * q) & (lane < 3 * q), pltpu.roll(qk, q, 1), qk))
    qk_ref[...] = qk.astype(qk_ref.dtype)
    vt_ref[...] = x[:, o_v:o_cq].T.astype(vt_ref.dtype)
    t = x[:, o_kr:o_kr + LANES]
    lane = lax.broadcasted_iota(jnp.int32, t.shape, 1)
    kpe = jnp.where(lane < q, t, jnp.where((lane >= 2 * q) & (lane < 3 * q), pltpu.roll(t, q, 1), 0.0))
    pad = jnp.zeros((x.shape[0], lat_ref.shape[1] - (o_kr - o_cq) - LANES), F32)
    lat_ref[...] = jnp.concatenate([x[:, o_cq:o_kr], kpe, pad], axis=1).astype(lat_ref.dtype)
    gate_ref[...] = x[:, o_gt:].astype(gate_ref.dtype)


def _split_w_in(w_in, lat_width, *, qw, o_v, o_cq, o_kr, o_gt):
    depth, d, n_in = w_in.shape
    assert MLA_ROPE_DIM == 2 * (LANES // 4) and o_kr % LANES == 0 and o_kr + LANES <= n_in
    rows = _pick(d, 256, LANES)
    block = lambda width: pl.BlockSpec((None, rows, width), lambda l, r: (l, r, 0))
    return pl.pallas_call(
        functools.partial(_w_in_kernel, qw=qw, o_v=o_v, o_cq=o_cq, o_kr=o_kr, o_gt=o_gt),
        grid=(depth, d // rows),
        in_specs=[block(n_in)],
        out_specs=[block(o_v), pl.BlockSpec((None, o_cq - o_v, rows), lambda l, r: (l, 0, r)),
                   block(lat_width), block(n_in - o_gt)],
        out_shape=[jax.ShapeDtypeStruct((depth, d, o_v), MXU_DTYPE),
                   jax.ShapeDtypeStruct((depth, o_cq - o_v, d), MXU_DTYPE),
                   jax.ShapeDtypeStruct((depth, d, lat_width), MXU_DTYPE),
                   jax.ShapeDtypeStruct((depth, d, n_in - o_gt), MXU_DTYPE)],
        compiler_params=_cparams("parallel", "parallel"),
        name="split_w_in",
    )(w_in)


def _tables_kernel(pos_ref, inv_ref, sgn_ref, cos_ref, sin_ref):
    ang = pos_ref[...].astype(F32) * inv_ref[...]
    cos_ref[...] = jnp.cos(ang)
    sin_ref[...] = jnp.sin(ang) * sgn_ref[...]


def _rope_tables(positions):
    m = positions.size
    d = DIFF_QK_DIM
    inv = ROPE_THETA ** (-jnp.arange(0, d, 2, dtype=F32) / d)
    inv128 = jnp.tile(inv, LANES // (d // 2))[None, :]
    sgn128 = jnp.concatenate([-jnp.ones((LANES // 2,), F32), jnp.ones((LANES // 2,), F32)])[None, :]
    bm = _pick(m, 2048, 8)
    row = pl.BlockSpec((bm, LANES), lambda i: (i, 0))
    const = pl.BlockSpec((1, LANES), lambda i: (0, 0))
    return pl.pallas_call(
        _tables_kernel,
        grid=(m // bm,),
        in_specs=[pl.BlockSpec((bm, 1), lambda i: (i, 0)), const, const],
        out_specs=[row, row],
        out_shape=[jax.ShapeDtypeStruct((m, LANES), F32)] * 2,
        compiler_params=_cparams("parallel"),
        name="rope_tables",
    )(positions.reshape(m, 1), inv128, sgn128)


def _rmsnorm_kernel(x_ref, g_ref, o_ref):
    o_ref[...] = _rms(x_ref[...], g_ref[...]).astype(o_ref.dtype)


def _rmsnorm(x, g):
    m, d = x.shape
    bm = _pick(m, 512, 8)
    return pl.pallas_call(
        _rmsnorm_kernel,
        grid=(m // bm,),
        in_specs=[pl.BlockSpec((bm, d), lambda i: (i, 0)), pl.BlockSpec((1, d), lambda i: (0, 0))],
        out_specs=pl.BlockSpec((bm, d), lambda i: (i, 0)),
        out_shape=jax.ShapeDtypeStruct((m, d), MXU_DTYPE),
        compiler_params=_cparams("parallel"),
        name="rmsnorm",
    )(x, g[None, :])


PROJ_ROW_TILE = 256


def _proj_kernel(h_ref, w_ref, *rest, epilogue, sub, rows):
    *extra, o_ref = rest
    for r in range(h_ref.shape[0] // rows):
        rs = pl.ds(r * rows, rows)
        h = h_ref[rs, :]
        for t in range(w_ref.shape[1] // sub):
            cols = slice(t * sub, (t + 1) * sub)
            acc = jnp.dot(h, w_ref[:, cols], preferred_element_type=F32)
            o_ref[rs, cols] = epilogue(acc, rs, cols, *extra).astype(o_ref.dtype)


def _ep_plain(acc, rs, cols):
    return acc


def _ep_rope(acc, rs, cols, cos_ref, sin_ref, *, q_tiles=0, rope_from=0):
    if cols.start < rope_from:
        return acc
    cos, sin = cos_ref[rs, :], sin_ref[rs, :]
    n = acc.shape[1] // LANES
    out = jnp.concatenate(
        [_rope128(acc[:, c * LANES:(c + 1) * LANES], cos, sin) for c in range(n)], axis=1)
    if q_tiles:
        out = out * jnp.where(pl.program_id(1) < q_tiles, LOG2E, 1.0)
    return out


def _proj(h, w, epilogue, *, name, extras=(), extra_specs=None, bm_pref=1024, bn_pref=512):
    m, k = h.shape
    n = _wshape(w)[1]
    bm = _pick(m, bm_pref, BF16_SUBLANES)
    bn = _pick(n, bn_pref)
    specs = [pl.BlockSpec((bm, k), lambda i, j: (i, 0)), _wspec(w, (k, bn), lambda i, j: (0, j))]
    specs += list(extra_specs(bm, bn)) if extra_specs else []
    return pl.pallas_call(
        functools.partial(_proj_kernel, epilogue=epilogue, sub=_pick(bn, MXU_COLS),
                          rows=_pick(bm, PROJ_ROW_TILE, BF16_SUBLANES)),
        grid=(m // bm, n // bn),
        in_specs=specs,
        out_specs=pl.BlockSpec((bm, bn), lambda i, j: (i, j)),
        out_shape=jax.ShapeDtypeStruct((m, n), MXU_DTYPE),
        compiler_params=_cparams("parallel", "arbitrary"),
        name=name,
    )(h, _layered(w)[0], *extras)


def _rope_specs(bm, bn):
    tab = pl.BlockSpec((bm, LANES), lambda i, j: (i, 0))
    return [tab, tab]


def _proj_t_kernel(h_ref, wt_ref, o_ref):
    o_ref[...] = lax.dot_general(wt_ref[...], h_ref[...], _NT,
                                 preferred_element_type=F32).astype(o_ref.dtype)


def _proj_t(h, wt, *, name):
    m, k = h.shape
    n = _wshape(wt)[0]
    bm = _pick(m, 1024)
    bn = _pick(n, 512, BF16_SUBLANES)
    return pl.pallas_call(
        _proj_t_kernel,
        grid=(m // bm, n // bn),
        in_specs=[pl.BlockSpec((bm, k), lambda i, j: (i, 0)), _wspec(wt, (bn, k), lambda i, j: (j, 0))],
        out_specs=pl.BlockSpec((bn, bm), lambda i, j: (j, i)),
        out_shape=jax.ShapeDtypeStruct((n, m), MXU_DTYPE),
        compiler_params=_cparams("parallel", "arbitrary"),
        name=name,
    )(h, _layered(wt)[0])


def _latent_q_kernel(c_ref, g_ref, w_ref, cos_ref, sin_ref, o_ref, *, scale):
    cn = _rms(c_ref[...].astype(F32), g_ref[...]).astype(MXU_DTYPE)
    acc = jnp.dot(cn, w_ref[...], preferred_element_type=F32)
    cos, sin = cos_ref[...], sin_ref[...]
    cols = []
    for c in range(acc.shape[1] // LANES):
        t = acc[:, c * LANES:(c + 1) * LANES]
        cols.append(_rope128(t, cos, sin) if c % 2 else t)
    o_ref[...] = (jnp.concatenate(cols, axis=1) * scale).astype(o_ref.dtype)


def _latent_kv_kernel(c_ref, g_ref, wk_ref, wvt_ref, k_ref, vt_ref):
    cn = _rms(c_ref[...].astype(F32), g_ref[...]).astype(MXU_DTYPE)
    k_ref[...] = jnp.dot(cn, wk_ref[...], preferred_element_type=F32).astype(k_ref.dtype)
    vt_ref[...] = lax.dot_general(wvt_ref[...], cn, _NT, preferred_element_type=F32).astype(vt_ref.dtype)


def _latent_q(src, col_block, g, w, cos, sin, scale):
    m = src.shape[0]
    rank, n = w.shape
    bm = _pick(m, 512, BF16_SUBLANES)
    tab = pl.BlockSpec((bm, LANES), lambda i: (i, 0))
    return pl.pallas_call(
        functools.partial(_latent_q_kernel, scale=scale),
        grid=(m // bm,),
        in_specs=[pl.BlockSpec((bm, rank), lambda i: (i, col_block)),
                  pl.BlockSpec((1, rank), lambda i: (0, 0)),
                  pl.BlockSpec((rank, n), lambda i: (0, 0)), tab, tab],
        out_specs=pl.BlockSpec((bm, n), lambda i: (i, 0)),
        out_shape=jax.ShapeDtypeStruct((m, n), MXU_DTYPE),
        compiler_params=_cparams("parallel"),
        name="mla_q",
    )(src, g[None, :], w, cos, sin)


def _latent_kv(src, col_block, g, wk, wvt):
    m = src.shape[0]
    rank, nk = wk.shape
    nv = wvt.shape[0]
    bm = _pick(m, 512)
    return pl.pallas_call(
        _latent_kv_kernel,
        grid=(m // bm,),
        in_specs=[pl.BlockSpec((bm, rank), lambda i: (i, col_block)),
                  pl.BlockSpec((1, rank), lambda i: (0, 0)),
                  pl.BlockSpec((rank, nk), lambda i: (0, 0)),
                  pl.BlockSpec((nv, rank), lambda i: (0, 0))],
        out_specs=[pl.BlockSpec((bm, nk), lambda i: (i, 0)), pl.BlockSpec((nv, bm), lambda i: (0, i))],
        out_shape=[jax.ShapeDtypeStruct((m, nk), MXU_DTYPE), jax.ShapeDtypeStruct((nv, m), MXU_DTYPE)],
        compiler_params=_cparams("parallel"),
        name="mla_kv",
    )(src, g[None, :], wk, wvt)


def _flash(q_parts, load_k, load_vt, qi, *, bq, bk, dv, unroll):
    diag = bq // bk
    assert unroll % diag == 0
    rel = (lax.broadcasted_iota(jnp.int32, (bk, bq), 0) // CHUNK
           - lax.broadcasted_iota(jnp.int32, (bk, bq), 1) // CHUNK)

    def group(jg, carries, count, masked):
        blocks = [jg * unroll + u for u in range(count)]
        scores = [[lax.dot_general(load_k(j, part), q, _NT, preferred_element_type=F32)
                   for part, q in enumerate(q_parts)] for j in blocks]
        carries = list(carries)
        for u, j in enumerate(blocks):
            d = u - (count - masked)
            for part, s in enumerate(scores[u]):
                m, l, acc = carries[part]
                if d >= 0:
                    s = jnp.where(rel <= -d * (bk // CHUNK), s, -jnp.inf)
                m_new = jnp.maximum(m, s.max(axis=0, keepdims=True))
                alpha = jnp.exp2(m - m_new)
                p = jnp.exp2(s - m_new)
                l = alpha * l + p.sum(axis=0, keepdims=True)
                acc = alpha * acc + jnp.dot(load_vt(j, part), p.astype(MXU_DTYPE),
                                            preferred_element_type=F32)
                carries[part] = (m_new, l, acc)
        return tuple(carries)

    init = (jnp.full((1, bq), -jnp.inf, F32), jnp.zeros((1, bq), F32), jnp.zeros((dv, bq), F32))
    carries = tuple(init for _ in q_parts)
    n_blocks = (qi + 1) * diag
    n_groups = (n_blocks + unroll - 1) // unroll
    carries = lax.fori_loop(0, n_groups - 1, lambda jg, c: group(jg, c, unroll, 0), carries)
    last = (n_blocks - (n_groups - 1) * unroll) // diag - 1
    tails = [functools.partial(group, count=(t + 1) * diag, masked=diag) for t in range(unroll // diag)]
    carries = lax.switch(last, [lambda c, f=f: f(n_groups - 1, c) for f in tails], carries)
    return [(acc, l) for _, l, acc in carries]


KV_GROUP_KEYS = 1024


def _kv_group(seq, bq, bk):
    group = max(bq // bk, min(KV_GROUP_KEYS // bk, seq // bk))
    assert (seq // bk) % group == 0 and group % (bq // bk) == 0
    return group


DIFF_HEADS_PER_STEP = 2


def _diff_attn_kernel(lam_ref, q_ref, k_ref, vt_ref, g_ref, o_ref, *, bq, bk, lam_init, unroll):
    qi = pl.program_id(2)
    lam_v = lam_ref[...]
    lam = (jnp.exp(jnp.sum(lam_v[0:1] * lam_v[1:2], axis=1, keepdims=True))
           - jnp.exp(jnp.sum(lam_v[2:3] * lam_v[3:4], axis=1, keepdims=True)) + lam_init)
    hw = LANES
    is_q1 = (lax.broadcasted_iota(jnp.int32, (bq, hw), 1) & (DIFF_QK_DIM // 2)) == 0
    q_parts = []
    for hh in range(DIFF_HEADS_PER_STEP):
        q = q_ref[:, hh * hw:(hh + 1) * hw]
        zero = jnp.zeros_like(q)
        q_parts += [jnp.where(is_q1, q, zero), jnp.where(is_q1, zero, q)]

    def load_k(j, part):
        hh = part // 2
        return k_ref[pl.ds(pl.multiple_of(j * bk, bk), bk), hh * hw:(hh + 1) * hw]

    def load_vt(j, part):
        hh = part // 2
        return vt_ref[hh * DIFF_V_DIM:(hh + 1) * DIFF_V_DIM, pl.ds(pl.multiple_of(j * bk, bk), bk)]

    outs = _flash(q_parts, load_k, load_vt, qi, bq=bq, bk=bk, dv=DIFF_V_DIM, unroll=unroll)
    for hh in range(DIFF_HEADS_PER_STEP):
        (acc1, l1), (acc2, l2) = outs[2 * hh], outs[2 * hh + 1]
        o = acc1 * (1.0 / l1) - lam * (acc2 * (1.0 / l2))
        o = o * lax.rsqrt(jnp.mean(o * o, axis=0, keepdims=True) + EPS) * g_ref[...]
        o_ref[:, hh * DIFF_V_DIM:(hh + 1) * DIFF_V_DIM] = (o * (1.0 - lam_init)).T.astype(o_ref.dtype)


def _diff_attention(qk, vt, lam_vecs, g_sub, lam_init, batch, seq):
    m = qk.shape[0]
    bq = _pick(seq, 512, CHUNK)
    bk = _pick(bq, 512, CHUNK)
    nq = seq // bq
    hs = DIFF_HEADS_PER_STEP
    steps = DIFF_HEADS // hs
    unroll = _kv_group(seq, bq, bk)
    return pl.pallas_call(
        functools.partial(_diff_attn_kernel, bq=bq, bk=bk, lam_init=lam_init, unroll=unroll),
        grid=(batch, steps, nq),
        in_specs=[
            pl.BlockSpec((4, DIFF_QK_DIM), lambda b, h, i: (0, 0)),
            pl.BlockSpec((bq, hs * LANES), lambda b, h, i: (b * nq + i, h)),
            pl.BlockSpec((seq, hs * LANES), lambda b, h, i: (b, steps + h)),
            pl.BlockSpec((hs * DIFF_V_DIM, seq), lambda b, h, i: (h, b)),
            pl.BlockSpec((DIFF_V_DIM, 1), lambda b, h, i: (0, 0)),
        ],
        out_specs=pl.BlockSpec((bq, hs * DIFF_V_DIM), lambda b, h, i: (b * nq + i, h)),
        out_shape=jax.ShapeDtypeStruct((m, DIFF_WIDTH), MXU_DTYPE),
        compiler_params=_cparams("parallel", "parallel", "arbitrary"),
        name="diff_attention",
    )(lam_vecs, qk, qk, vt, g_sub[:, None])


MLA_HEADS_PER_STEP = 4


def _mla_attn_kernel(q_ref, kn_ref, kpe_ref, vt_ref, o_ref, kcat_ref, *, bq, bk, unroll):
    qi = pl.program_id(2)
    qw = 2 * LANES

    @pl.when(qi == 0)
    def _():
        for hh in range(MLA_HEADS_PER_STEP):
            kcat_ref[hh, :, :MLA_NOPE_DIM] = kn_ref[:, hh * MLA_NOPE_DIM:(hh + 1) * MLA_NOPE_DIM]
            kcat_ref[hh, :, MLA_NOPE_DIM:] = kpe_ref[...]

    def load_k(j, part):
        return kcat_ref[part, pl.ds(pl.multiple_of(j * bk, bk), bk), :]

    def load_vt(j, part):
        return vt_ref[part * MLA_V_DIM:(part + 1) * MLA_V_DIM, pl.ds(pl.multiple_of(j * bk, bk), bk)]

    q_parts = [q_ref[:, hh * qw:(hh + 1) * qw] for hh in range(MLA_HEADS_PER_STEP)]
    outs = _flash(q_parts, load_k, load_vt, qi, bq=bq, bk=bk, dv=MLA_V_DIM, unroll=unroll)
    for hh, (acc, l) in enumerate(outs):
        o_ref[:, hh * MLA_V_DIM:(hh + 1) * MLA_V_DIM] = (acc * (1.0 / l)).T.astype(o_ref.dtype)


def _mla_attention(q, k_nope, kpe_src, kpe_block, vt, batch, seq):
    m = q.shape[0]
    bq = _pick(seq, 512, CHUNK)
    bk = _pick(bq, 512, CHUNK)
    nq = seq // bq
    hs = MLA_HEADS_PER_STEP
    qw = 2 * LANES
    unroll = _kv_group(seq, bq, bk)
    return pl.pallas_call(
        functools.partial(_mla_attn_kernel, bq=bq, bk=bk, unroll=unroll),
        grid=(batch, MLA_HEADS // hs, nq),
        in_specs=[
            pl.BlockSpec((bq, hs * qw), lambda b, h, i: (b * nq + i, h)),
            pl.BlockSpec((seq, hs * MLA_NOPE_DIM), lambda b, h, i: (b, h)),
            pl.BlockSpec((seq, LANES), lambda b, h, i: (b, kpe_block)),
            pl.BlockSpec((hs * MLA_V_DIM, seq), lambda b, h, i: (h, b)),
        ],
        out_specs=pl.BlockSpec((bq, hs * MLA_V_DIM), lambda b, h, i: (b * nq + i, h)),
        out_shape=jax.ShapeDtypeStruct((m, MLA_WIDTH), MXU_DTYPE),
        scratch_shapes=[pltpu.VMEM((hs, seq, qw), MXU_DTYPE)],
        compiler_params=_cparams("parallel", "parallel", "arbitrary"),
        name="mla_attention",
    )(q, k_nope, kpe_src, vt)


def _merge_kernel(h_ref, od_ref, om_ref, wga_ref, wgb_ref, ba_ref, bb_ref, wd_ref, wm_ref, o_ref, *, sub, rows):
    for r in range(h_ref.shape[0] // rows):
        rs = pl.ds(r * rows, rows)
        h, od, om = h_ref[rs, :], od_ref[rs, :], om_ref[rs, :]
        for t in range(wd_ref.shape[1] // sub):
            cols = slice(t * sub, (t + 1) * sub)
            ga = jax.nn.sigmoid(jnp.dot(h, wga_ref[:, cols], preferred_element_type=F32) + ba_ref[:, cols])
            gb = jax.nn.sigmoid(jnp.dot(h, wgb_ref[:, cols], preferred_element_type=F32) + bb_ref[:, cols])
            a = jnp.dot(od, wd_ref[:, cols], preferred_element_type=F32)
            b = jnp.dot(om, wm_ref[:, cols], preferred_element_type=F32)
            o_ref[rs, cols] = (ga * a + gb * b).astype(o_ref.dtype)


def _merge(h, o_diff, o_mla, w_gate, b_gate, w_d, w_m):
    m, d = h.shape
    n = _wshape(w_d)[1]
    bm = _pick(m, 1024, BF16_SUBLANES)
    bn = _pick(n, 512)
    nb = n // bn
    lo = lambda i, j: (0, j)
    hi = lambda i, j: (0, j + nb)
    row = lambda i, j: (i, 0)
    return pl.pallas_call(
        functools.partial(_merge_kernel, sub=_pick(bn, MXU_COLS), rows=_pick(bm, PROJ_ROW_TILE, BF16_SUBLANES)),
        grid=(m // bm, nb),
        in_specs=[
            pl.BlockSpec((bm, d), row),
            pl.BlockSpec((bm, DIFF_WIDTH), row),
            pl.BlockSpec((bm, MLA_WIDTH), row),
            _wspec(w_gate, (d, bn), lo),
            _wspec(w_gate, (d, bn), hi),
            pl.BlockSpec((1, bn), lo),
            pl.BlockSpec((1, bn), hi),
            _wspec(w_d, (DIFF_WIDTH, bn), lo),
            _wspec(w_m, (MLA_WIDTH, bn), lo),
        ],
        out_specs=pl.BlockSpec((bm, bn), lambda i, j: (i, j)),
        out_shape=jax.ShapeDtypeStruct((m, n), MXU_DTYPE),
        compiler_params=_cparams("parallel", "arbitrary"),
        name="branch_merge",
    )(h, o_diff, o_mla, _layered(w_gate)[0], _layered(w_gate)[0], b_gate[None, :], b_gate[None, :],
      _layered(w_d)[0], _layered(w_m)[0])


def _residual_norm(y, x_ref, gp_ref, gn_ref, xo_ref, ho_ref):
    xn = x_ref[...] + _rms(y, gp_ref[...])
    xo_ref[...] = xn
    ho_ref[...] = _rms(xn, gn_ref[...]).astype(ho_ref.dtype)


def _out_proj_kernel(lhs_ref, w_ref, x_ref, gp_ref, gn_ref, xo_ref, ho_ref, *, row_tiles):
    rows = lhs_ref.shape[0] // row_tiles
    for t in range(row_tiles):
        r = pl.ds(t * rows, rows)
        y = jnp.dot(lhs_ref[r, :], w_ref[...], preferred_element_type=F32)
        _residual_norm(y, x_ref.at[r], gp_ref, gn_ref, xo_ref.at[r], ho_ref.at[r])


def _out_proj(lhs, w, x, g_post, g_next):
    m, kdim = lhs.shape
    n = _wshape(w)[1]
    bm = _pick(m, 512, BF16_SUBLANES)
    row = lambda i: (i, 0)
    const = lambda i: (0, 0)
    return pl.pallas_call(
        functools.partial(_out_proj_kernel, row_tiles=2 if bm % (2 * BF16_SUBLANES) == 0 else 1),
        grid=(m // bm,),
        in_specs=[
            pl.BlockSpec((bm, kdim), row),
            _wspec(w, (kdim, n), const),
            pl.BlockSpec((bm, n), row),
            pl.BlockSpec((1, n), const),
            pl.BlockSpec((1, n), const),
        ],
        out_specs=[pl.BlockSpec((bm, n), row), pl.BlockSpec((bm, n), row)],
        out_shape=[jax.ShapeDtypeStruct((m, n), F32), jax.ShapeDtypeStruct((m, n), MXU_DTYPE)],
        compiler_params=_cparams("parallel"),
        name="out_proj",
    )(lhs, _layered(w)[0], x, g_post[None, :], g_next[None, :])


def _mem_attn_kernel(h_ref, wq_ref, kv_ref, wo_ref, x_ref, gp_ref, gn_ref, xo_ref, ho_ref, o_ref, *, dh, scale):
    width = MEM_HEADS * dh
    h = h_ref[...]
    head_cols = [slice(hd * dh, (hd + 1) * dh) for hd in range(MEM_HEADS)]

    def project(cols):
        return (jnp.dot(h, wq_ref[:, cols], preferred_element_type=F32) * scale).astype(MXU_DTYPE)

    q = project(head_cols[0])
    for hd, cols in enumerate(head_cols):
        s = lax.dot_general(q, kv_ref[:, cols], _NT, preferred_element_type=F32)
        if hd + 1 < MEM_HEADS:
            q = project(head_cols[hd + 1])
        p = jnp.exp(s - s.max(axis=1, keepdims=True))
        l = p.sum(axis=1, keepdims=True)
        v = kv_ref[:, width + hd * dh:width + (hd + 1) * dh]
        o_ref[:, cols] = (jnp.dot(p.astype(MXU_DTYPE), v, preferred_element_type=F32) / l).astype(o_ref.dtype)
    row_tiles = 2 if o_ref.shape[0] % (2 * BF16_SUBLANES) == 0 else 1
    rows = o_ref.shape[0] // row_tiles
    for t in range(row_tiles):
        r = pl.ds(t * rows, rows)
        y = jnp.dot(o_ref[r, :], wo_ref[...], preferred_element_type=F32)
        _residual_norm(y, x_ref.at[r], gp_ref, gn_ref, xo_ref.at[r], ho_ref.at[r])


def _mem_attention(h, w_q, kv, w_o, x, g_post, g_next, seq, mem_tokens):
    m, d = h.shape
    dh = d // MEM_HEADS
    bq = _pick(seq, 512, BF16_SUBLANES)
    per_batch = seq // bq
    row = lambda i: (i, 0)
    const = lambda i: (0, 0)
    resident = lambda w: _wspec(w, (d, d), const, pipeline_mode=pl.Buffered(1))
    return pl.pallas_call(
        functools.partial(_mem_attn_kernel, dh=dh, scale=dh ** -0.5),
        grid=(m // bq,),
        in_specs=[pl.BlockSpec((bq, d), row),
                  resident(w_q),
                  pl.BlockSpec((mem_tokens, 2 * d), lambda i: (i // per_batch, 0)),
                  resident(w_o),
                  pl.BlockSpec((bq, d), row),
                  pl.BlockSpec((1, d), const),
                  pl.BlockSpec((1, d), const)],
        out_specs=[pl.BlockSpec((bq, d), row), pl.BlockSpec((bq, d), row)],
        out_shape=[jax.ShapeDtypeStruct((m, d), F32), jax.ShapeDtypeStruct((m, d), MXU_DTYPE)],
        scratch_shapes=[pltpu.VMEM((bq, d), MXU_DTYPE)],
        compiler_params=_cparams("parallel"),
        name="mem_attention",
    )(h, _layered(w_q)[0], kv, _layered(w_o)[0], x, g_post[None, :], g_next[None, :])


HALO = BF16_SUBLANES


def _ffn_kernel(h_ref, halo_ref, wa_ref, wg_ref, cwa_ref, cwg_ref, cba_ref, cbg_ref, wd_ref, x_ref, gp_ref,
                gn_ref, xo_ref, ho_ref, hs_ref, u_ref, acc_ref, *, bm, blocks_per_seq, nb, sub, row_tiles):
    i = pl.program_id(0)
    j = pl.program_id(1)

    @pl.when(j == 0)
    def _():
        halo = halo_ref[...]
        first = (i % blocks_per_seq) == 0
        hs_ref[:HALO, :] = jnp.where(first, jnp.zeros_like(halo), halo)
        hs_ref[HALO:, :] = h_ref[...]
        acc_ref[...] = jnp.zeros_like(acc_ref)

    rows = bm // row_tiles
    tiles = [slice(t * sub, (t + 1) * sub) for t in range(wa_ref.shape[1] // sub)]
    bounds = [0] + [HALO + (r + 1) * rows for r in range(row_tiles)]
    for r in range(row_tiles):
        rs = pl.ds(bounds[r], bounds[r + 1] - bounds[r])
        hs = hs_ref[rs, :]
        for t, cols in enumerate(tiles):
            u_ref[2 * t, rs, :] = jnp.dot(hs, wa_ref[:, cols], preferred_element_type=F32)
            u_ref[2 * t + 1, rs, :] = jnp.dot(hs, wg_ref[:, cols], preferred_element_type=F32)

    def conv(slot, cols, row0, rows, cw_ref, cb_ref):
        cw = cw_ref[:, cols]
        c = cb_ref[:, cols]
        for tap in range(CONV_WIDTH):
            c = c + u_ref[slot, pl.ds(row0 + HALO - (CONV_WIDTH - 1) + tap, rows), :] * cw[tap:tap + 1]
        return c

    for r in range(row_tiles):
        acts = []
        for t, cols in enumerate(tiles):
            a = conv(2 * t, cols, r * rows, rows, cwa_ref, cba_ref)
            g = conv(2 * t + 1, cols, r * rows, rows, cwg_ref, cbg_ref)
            acts.append((a * jax.nn.sigmoid(a) * g).astype(MXU_DTYPE))
        acc_ref[pl.ds(r * rows, rows), :] += jnp.dot(jnp.concatenate(acts, axis=1), wd_ref[...],
                                                     preferred_element_type=F32)

    @pl.when(j == nb - 1)
    def _():
        _residual_norm(acc_ref[...], x_ref, gp_ref, gn_ref, xo_ref, ho_ref)


def _ffn(h, w_up, conv_w, conv_b, w_down, x, g_post, g_next, seq):
    m, d = h.shape
    ff = _wshape(w_up)[1] // 2
    bm = _pick(seq, 512, HALO)
    bn = _pick(ff, 512)
    sub = _pick(bn, MXU_COLS)
    nb = ff // bn
    halo_blocks = bm // HALO
    row = lambda i, j: (i, 0)
    const = lambda i, j: (0, 0)
    up_a = lambda i, j: (0, j)
    up_g = lambda i, j: (0, j + nb)
    return pl.pallas_call(
        functools.partial(_ffn_kernel, bm=bm, blocks_per_seq=seq // bm, nb=nb, sub=sub,
                          row_tiles=2 if bm % (2 * BF16_SUBLANES) == 0 else 1),
        grid=(m // bm, nb),
        in_specs=[
            pl.BlockSpec((bm, d), row),
            pl.BlockSpec((HALO, d), lambda i, j: (jnp.maximum(i * halo_blocks - 1, 0), 0)),
            _wspec(w_up, (d, bn), up_a),
            _wspec(w_up, (d, bn), up_g),
            pl.BlockSpec((CONV_WIDTH, bn), up_a),
            pl.BlockSpec((CONV_WIDTH, bn), up_g),
            pl.BlockSpec((1, bn), up_a),
            pl.BlockSpec((1, bn), up_g),
            _wspec(w_down, (bn, d), lambda i, j: (j, 0)),
            pl.BlockSpec((bm, d), row),
            pl.BlockSpec((1, d), const),
            pl.BlockSpec((1, d), const),
        ],
        out_specs=[pl.BlockSpec((bm, d), row), pl.BlockSpec((bm, d), row)],
        out_shape=[jax.ShapeDtypeStruct((m, d), F32), jax.ShapeDtypeStruct((m, d), MXU_DTYPE)],
        scratch_shapes=[pltpu.VMEM((HALO + bm, d), MXU_DTYPE),
                        pltpu.VMEM((2 * (bn // sub), HALO + bm, sub), F32),
                        pltpu.VMEM((bm, d), F32)],
        compiler_params=_cparams("parallel", "arbitrary"),
        name="ffn",
    )(h, h, _layered(w_up)[0], _layered(w_up)[0], conv_w, conv_w, conv_b[None, :], conv_b[None, :],
      _layered(w_down)[0], x, g_post[None, :], g_next[None, :])


def kernel(x, mem, positions, g_pre_mix, w_in, b_gate, lam_q1, lam_k1, lam_q2, lam_k2, g_diff_sub, g_cq,
           w_uq, g_ckv, w_ukv, w_br_diff, w_br_mla, w_mix_out, g_post_mix, g_pre_x, g_mem, w_q_x, w_kv_x,
           w_o_x, g_post_x, g_pre_ffn, w_up, conv_w, conv_b, w_down, g_post_ffn):
    batch, seq, d = x.shape
    mem_tokens = mem.shape[1]
    depth = w_in.shape[0]
    m = batch * seq
    cast = lambda a: a.astype(MXU_DTYPE)

    qw = DIFF_HEADS * 2 * DIFF_QK_DIM
    o_v = 2 * qw
    o_cq = o_v + DIFF_WIDTH
    o_ckv = o_cq + MLA_Q_RANK
    o_kr = o_ckv + MLA_KV_RANK
    o_gt = o_kr + MLA_ROPE_DIM
    assert w_in.shape[2] == o_gt + 2 * d
    assert MLA_Q_RANK % MLA_KV_RANK == 0
    qhead = MLA_NOPE_DIM + MLA_ROPE_DIM
    lat_width = -(-(o_kr - o_cq + LANES) // MXU_COLS) * MXU_COLS
    kvhead = MLA_NOPE_DIM + MLA_V_DIM

    cos, sin = _rope_tables(positions)
    xf = x.reshape(m, d)
    memf = mem.reshape(batch * mem_tokens, d)
    h = _rmsnorm(xf, g_pre_mix[0])
    w_qk_all, w_vt_all, w_lat_all, w_gate_all = _split_w_in(
        w_in, lat_width, qw=qw, o_v=o_v, o_cq=o_cq, o_kr=o_kr, o_gt=o_gt)

    w_br_diff_c, w_br_mla_c, w_mix_out_c = cast(w_br_diff), cast(w_br_mla), cast(w_mix_out)
    w_q_x_c, w_kv_x_c, w_o_x_c = cast(w_q_x), cast(w_kv_x), cast(w_o_x)
    w_up_c, w_down_c = cast(w_up), cast(w_down)

    for l in range(depth):
        lam_init = 0.8 - 0.6 * math.exp(-0.3 * l)
        w_qk, w_vt, w_lat, w_gate = (w_qk_all, l), (w_vt_all, l), (w_lat_all, l), (w_gate_all, l)
        w_uq_h = jnp.pad(w_uq[l].reshape(MLA_Q_RANK, MLA_HEADS, qhead),
                         ((0, 0), (0, 0), (0, 2 * LANES - qhead)))
        w_uq_p = cast(jnp.concatenate([w_uq_h[:, :, :MLA_NOPE_DIM], _pair_halves(w_uq_h[:, :, MLA_NOPE_DIM:])],
                                      axis=2).reshape(MLA_Q_RANK, -1))
        w_ukv_h = w_ukv[l].reshape(MLA_KV_RANK, MLA_HEADS, kvhead)
        w_uk = cast(w_ukv_h[:, :, :MLA_NOPE_DIM].reshape(MLA_KV_RANK, -1))
        w_uvt = cast(w_ukv_h[:, :, MLA_NOPE_DIM:].reshape(MLA_KV_RANK, -1).T)

        bn_qk = _pick(2 * qw, 512)
        qk = _proj(h, w_qk, functools.partial(_ep_rope, q_tiles=qw // bn_qk), name="proj_qk",
                   extras=(cos, sin), extra_specs=_rope_specs)
        vt = _proj_t(h, w_vt, name="proj_vt")
        lat = _proj(h, w_lat, functools.partial(_ep_rope, rope_from=o_kr - o_cq), name="proj_latent",
                    extras=(cos, sin), extra_specs=_rope_specs, bn_pref=lat_width)
        q_mla = _latent_q(lat, 0, g_cq[l], w_uq_p, cos, sin, qhead ** -0.5 * LOG2E)
        k_mla, vt_mla = _latent_kv(lat, MLA_Q_RANK // MLA_KV_RANK, g_ckv[l], w_uk, w_uvt)
        lam_vecs = jnp.stack([lam_q1[l], lam_k1[l], lam_q2[l], lam_k2[l]])
        o_diff = _diff_attention(qk, vt, lam_vecs, g_diff_sub[l], lam_init, batch, seq)
        o_mla = _mla_attention(q_mla, k_mla, lat, (o_kr - o_cq) // LANES, vt_mla, batch, seq)
        merged = _merge(h, o_diff, o_mla, w_gate, b_gate[l], (w_br_diff_c, l), (w_br_mla_c, l))
        xf, h = _out_proj(merged, (w_mix_out_c, l), xf, g_post_mix[l], g_pre_x[l])

        kv_x = _proj(_rmsnorm(memf, g_mem[l]), (w_kv_x_c, l), _ep_plain, name="proj_kvx")
        xf, h = _mem_attention(h, (w_q_x_c, l), kv_x, (w_o_x_c, l), xf, g_post_x[l], g_pre_ffn[l],
                               seq, mem_tokens)

        g_next = g_pre_mix[l + 1] if l + 1 < depth else g_pre_mix[0]
        xf, h = _ffn(h, (w_up_c, l), conv_w[l], conv_b[l], (w_down_c, l), xf, g_post_ffn[l], g_next, seq)

    return xf.reshape(batch, seq, d)
```

```python
import functools
import math

import jax
import jax.numpy as jnp
from jax import lax
from jax.experimental import pallas as pl
from jax.experimental.pallas import tpu as pltpu

CHUNK = 64
ROPE_THETA = 10000.0
EPS = 1e-6
DIFF_HEADS = 8
DIFF_QK_DIM = 64
DIFF_V_DIM = 128
MLA_HEADS = 8
MLA_Q_RANK = 512
MLA_KV_RANK = 256
MLA_NOPE_DIM = 128
MLA_ROPE_DIM = 64
MLA_V_DIM = 128
MEM_HEADS = 4
CONV_WIDTH = 3
DIFF_WIDTH = DIFF_HEADS * DIFF_V_DIM
MLA_WIDTH = MLA_HEADS * MLA_V_DIM

LANES = 128
BF16_SUBLANES = 16
MXU_COLS = 256
VMEM_LIMIT_BYTES = 56 * 2**20

MXU_DTYPE = jnp.bfloat16
F32 = jnp.float32
LOG2E = math.log2(math.e)


def _pick(n, pref, mult=LANES):
    if n <= pref:
        return n
    best = None
    for d in range(mult, pref + 1, mult):
        if n % d == 0:
            best = d
    assert best is not None, (n, pref, mult)
    return best


def _cparams(*sem):
    return pltpu.CompilerParams(dimension_semantics=sem, vmem_limit_bytes=VMEM_LIMIT_BYTES)


def _layered(w):
    return w if isinstance(w, tuple) else (w, None)


def _wshape(w):
    return _layered(w)[0].shape[-2:]


def _wspec(w, block, index_map, **kwargs):
    layer = _layered(w)[1]
    if layer is None:
        return pl.BlockSpec(block, index_map, **kwargs)
    return pl.BlockSpec((None, *block), lambda *g: (layer, *index_map(*g)), **kwargs)


def _rms(x, g):
    return x * lax.rsqrt(jnp.mean(x * x, axis=-1, keepdims=True) + EPS) * g


def _rope128(t, cos, sin):
    return t * cos + pltpu.roll(t, LANES // 2, 1) * sin


def _pair_halves(w):
    q = LANES // 4
    shape = w.shape
    w = w.reshape(*shape[:-1], shape[-1] // LANES, 2, 2, q)
    return jnp.swapaxes(w, -3, -2).reshape(shape)


_NT = (((1,), (1,)), ((), ()))


def _w_in_kernel(w_ref, qk_ref, vt_ref, lat_ref, gate_ref, *, qw, o_v, o_cq, o_kr, o_gt):
    x = w_ref[...]
    q = LANES // 4
    qk = x[:, :o_v]
    col = lax.broadcasted_iota(jnp.int32, qk.shape, 1)
    qk = jnp.where(col < qw, qk * (DIFF_QK_DIM ** -0.5), qk)
    lane = col % LANES
    qk = jnp.where((lane >= q) & (lane < 2 * q), pltpu.roll(qk, o_v - q, 1),
                   jnp.where((lane >= 2 * q) & (lane < 3 * q), pltpu.roll(qk, q, 1), qk))
    qk_ref[...] = qk.astype(qk_ref.dtype)
    vt_ref[...] = x[:, o_v:o_cq].T.astype(vt_ref.dtype)
    t = x[:, o_kr:o_kr + LANES]
    lane = lax.broadcasted_iota(jnp.int32, t.shape, 1)
    kpe = jnp.where(lane < q, t, jnp.where((lane >= 2 * q) & (lane < 3 * q), pltpu.roll(t, q, 1), 0.0))
    pad = jnp.zeros((x.shape[0], lat_ref.shape[1] - (o_kr - o_cq) - LANES), F32)
    lat_ref[...] = jnp.concatenate([x[:, o_cq:o_kr], kpe, pad], axis=1).astype(lat_ref.dtype)
    gate_ref[...] = x[:, o_gt:].astype(gate_ref.dtype)


def _split_w_in(w_in, lat_width, *, qw, o_v, o_cq, o_kr, o_gt):
    depth, d, n_in = w_in.shape
    assert MLA_ROPE_DIM == 2 * (LANES // 4) and o_kr % LANES == 0 and o_kr + LANES <= n_in
    rows = _pick(d, 256, LANES)
    block = lambda width: pl.BlockSpec((None, rows, width), lambda l, r: (l, r, 0))
    return pl.pallas_call(
        functools.partial(_w_in_kernel, qw=qw, o_v=o_v, o_cq=o_cq, o_kr=o_kr, o_gt=o_gt),
        grid=(depth, d // rows),
        in_specs=[block(n_in)],
        out_specs=[block(o_v), pl.BlockSpec((None, o_cq - o_v, rows), lambda l, r: (l, 0, r)),
                   block(lat_width), block(n_in - o_gt)],
        out_shape=[jax.ShapeDtypeStruct((depth, d, o_v), MXU_DTYPE),
                   jax.ShapeDtypeStruct((depth, o_cq - o_v, d), MXU_DTYPE),
                   jax.ShapeDtypeStruct((depth, d, lat_width), MXU_DTYPE),
                   jax.ShapeDtypeStruct((depth, d, n_in - o_gt), MXU_DTYPE)],
        compiler_params=_cparams("parallel", "parallel"),
        name="split_w_in",
    )(w_in)


def _tables_kernel(pos_ref, inv_ref, sgn_ref, cos_ref, sin_ref):
    ang = pos_ref[...].astype(F32) * inv_ref[...]
    cos_ref[...] = jnp.cos(ang)
    sin_ref[...] = jnp.sin(ang) * sgn_ref[...]


def _rope_tables(positions):
    m = positions.size
    d = DIFF_QK_DIM
    inv = ROPE_THETA ** (-jnp.arange(0, d, 2, dtype=F32) / d)
    inv128 = jnp.tile(inv, LANES // (d // 2))[None, :]
    sgn128 = jnp.concatenate([-jnp.ones((LANES // 2,), F32), jnp.ones((LANES // 2,), F32)])[None, :]
    bm = _pick(m, 2048, 8)
    row = pl.BlockSpec((bm, LANES), lambda i: (i, 0))
    const = pl.BlockSpec((1, LANES), lambda i: (0, 0))
    return pl.pallas_call(
        _tables_kernel,
        grid=(m // bm,),
        in_specs=[pl.BlockSpec((bm, 1), lambda i: (i, 0)), const, const],
        out_specs=[row, row],
        out_shape=[jax.ShapeDtypeStruct((m, LANES), F32)] * 2,
        compiler_params=_cparams("parallel"),
        name="rope_tables",
    )(positions.reshape(m, 1), inv128, sgn128)


def _rmsnorm_kernel(x_ref, g_ref, o_ref):
    o_ref[...] = _rms(x_ref[...], g_ref[...]).astype(o_ref.dtype)


def _rmsnorm(x, g):
    m, d = x.shape
    bm = _pick(m, 512, 8)
    return pl.pallas_call(
        _rmsnorm_kernel,
        grid=(m // bm,),
        in_specs=[pl.BlockSpec((bm, d), lambda i: (i, 0)), pl.BlockSpec((1, d), lambda i: (0, 0))],
        out_specs=pl.BlockSpec((bm, d), lambda i: (i, 0)),
        out_shape=jax.ShapeDtypeStruct((m, d), MXU_DTYPE),
        compiler_params=_cparams("parallel"),
        name="rmsnorm",
    )(x, g[None, :])


PROJ_ROW_TILE = 256


def _proj_kernel(h_ref, w_ref, *rest, epilogue, sub, rows):
    *extra, o_ref = rest
    for r in range(h_ref.shape[0] // rows):
        rs = pl.ds(r * rows, rows)
        h = h_ref[rs, :]
        for t in range(w_ref.shape[1] // sub):
            cols = slice(t * sub, (t + 1) * sub)
            acc = jnp.dot(h, w_ref[:, cols], preferred_element_type=F32)
            o_ref[rs, cols] = epilogue(acc, rs, cols, *extra).astype(o_ref.dtype)


def _ep_plain(acc, rs, cols):
    return acc


def _ep_rope(acc, rs, cols, cos_ref, sin_ref, *, q_tiles=0, rope_from=0):
    if cols.start < rope_from:
        return acc
    cos, sin = cos_ref[rs, :], sin_ref[rs, :]
    n = acc.shape[1] // LANES
    out = jnp.concatenate(
        [_rope128(acc[:, c * LANES:(c + 1) * LANES], cos, sin) for c in range(n)], axis=1)
    if q_tiles:
        out = out * jnp.where(pl.program_id(1) < q_tiles, LOG2E, 1.0)
    return out


def _proj(h, w, epilogue, *, name, extras=(), extra_specs=None, bm_pref=1024, bn_pref=512):
    m, k = h.shape
    n = _wshape(w)[1]
    bm = _pick(m, bm_pref, BF16_SUBLANES)
    bn = _pick(n, bn_pref)
    specs = [pl.BlockSpec((bm, k), lambda i, j: (i, 0)), _wspec(w, (k, bn), lambda i, j: (0, j))]
    specs += list(extra_specs(bm, bn)) if extra_specs else []
    return pl.pallas_call(
        functools.partial(_proj_kernel, epilogue=epilogue, sub=_pick(bn, MXU_COLS),
                          rows=_pick(bm, PROJ_ROW_TILE, BF16_SUBLANES)),
        grid=(m // bm, n // bn),
        in_specs=specs,
        out_specs=pl.BlockSpec((bm, bn), lambda i, j: (i, j)),
        out_shape=jax.ShapeDtypeStruct((m, n), MXU_DTYPE),
        compiler_params=_cparams("parallel", "arbitrary"),
        name=name,
    )(h, _layered(w)[0], *extras)


def _rope_specs(bm, bn):
    tab = pl.BlockSpec((bm, LANES), lambda i, j: (i, 0))
    return [tab, tab]


def _proj_t_kernel(h_ref, wt_ref, o_ref):
    o_ref[...] = lax.dot_general(wt_ref[...], h_ref[...], _NT,
                                 preferred_element_type=F32).astype(o_ref.dtype)


def _proj_t(h, wt, *, name):
    m, k = h.shape
    n = _wshape(wt)[0]
    bm = _pick(m, 1024)
    bn = _pick(n, 512, BF16_SUBLANES)
    return pl.pallas_call(
        _proj_t_kernel,
        grid=(m // bm, n // bn),
        in_specs=[pl.BlockSpec((bm, k), lambda i, j: (i, 0)), _wspec(wt, (bn, k), lambda i, j: (j, 0))],
        out_specs=pl.BlockSpec((bn, bm), lambda i, j: (j, i)),
        out_shape=jax.ShapeDtypeStruct((n, m), MXU_DTYPE),
        compiler_params=_cparams("parallel", "arbitrary"),
        name=name,
    )(h, _layered(wt)[0])


def _latent_q_kernel(c_ref, g_ref, w_ref, cos_ref, sin_ref, o_ref, *, scale):
    cn = _rms(c_ref[...].astype(F32), g_ref[...]).astype(MXU_DTYPE)
    acc = jnp.dot(cn, w_ref[...], preferred_element_type=F32)
    cos, sin = cos_ref[...], sin_ref[...]
    cols = []
    for c in range(acc.shape[1] // LANES):
        t = acc[:, c * LANES:(c + 1) * LANES]
        cols.append(_rope128(t, cos, sin) if c % 2 else t)
    o_ref[...] = (jnp.concatenate(cols, axis=1) * scale).astype(o_ref.dtype)


def _latent_kv_kernel(c_ref, g_ref, wk_ref, wvt_ref, k_ref, vt_ref):
    cn = _rms(c_ref[...].astype(F32), g_ref[...]).astype(MXU_DTYPE)
    k_ref[...] = jnp.dot(cn, wk_ref[...], preferred_element_type=F32).astype(k_ref.dtype)
    vt_ref[...] = lax.dot_general(wvt_ref[...], cn, _NT, preferred_element_type=F32).astype(vt_ref.dtype)


def _latent_q(src, col_block, g, w, cos, sin, scale):
    m = src.shape[0]
    rank, n = w.shape
    bm = _pick(m, 512, BF16_SUBLANES)
    tab = pl.BlockSpec((bm, LANES), lambda i: (i, 0))
    return pl.pallas_call(
        functools.partial(_latent_q_kernel, scale=scale),
        grid=(m // bm,),
        in_specs=[pl.BlockSpec((bm, rank), lambda i: (i, col_block)),
                  pl.BlockSpec((1, rank), lambda i: (0, 0)),
                  pl.BlockSpec((rank, n), lambda i: (0, 0)), tab, tab],
        out_specs=pl.BlockSpec((bm, n), lambda i: (i, 0)),
        out_shape=jax.ShapeDtypeStruct((m, n), MXU_DTYPE),
        compiler_params=_cparams("parallel"),
        name="mla_q",
    )(src, g[None, :], w, cos, sin)


def _latent_kv(src, col_block, g, wk, wvt):
    m = src.shape[0]
    rank, nk = wk.shape
    nv = wvt.shape[0]
    bm = _pick(m, 512)
    return pl.pallas_call(
        _latent_kv_kernel,
        grid=(m // bm,),
        in_specs=[pl.BlockSpec((bm, rank), lambda i: (i, col_block)),
                  pl.BlockSpec((1, rank), lambda i: (0, 0)),
                  pl.BlockSpec((rank, nk), lambda i: (0, 0)),
                  pl.BlockSpec((nv, rank), lambda i: (0, 0))],
        out_specs=[pl.BlockSpec((bm, nk), lambda i: (i, 0)), pl.BlockSpec((nv, bm), lambda i: (0, i))],
        out_shape=[jax.ShapeDtypeStruct((m, nk), MXU_DTYPE), jax.ShapeDtypeStruct((nv, m), MXU_DTYPE)],
        compiler_params=_cparams("parallel"),
        name="mla_kv",
    )(src, g[None, :], wk, wvt)


def _flash(q_parts, load_k, load_vt, qi, *, bq, bk, dv, unroll):
    diag = bq // bk
    assert unroll % diag == 0
    rel = (lax.broadcasted_iota(jnp.int32, (bk, bq), 0) // CHUNK
           - lax.broadcasted_iota(jnp.int32, (bk, bq), 1) // CHUNK)

    def group(jg, carries, count, masked):
        blocks = [jg * unroll + u for u in range(count)]
        scores = [[lax.dot_general(load_k(j, part), q, _NT, preferred_element_type=F32)
                   for part, q in enumerate(q_parts)] for j in blocks]
        carries = list(carries)
        for u, j in enumerate(blocks):
            d = u - (count - masked)
            for part, s in enumerate(scores[u]):
                m, acc = carries[part]
                if d >= 0:
                    s = jnp.where(rel <= -d * (bk // CHUNK), s, -jnp.inf)
                m_new = jnp.maximum(m, s.max(axis=0, keepdims=True))
                alpha = jnp.exp2(m - m_new)
                p = jnp.exp2((s - m_new).astype(MXU_DTYPE))
                vt_ones = jnp.concatenate([load_vt(j, part), ones], axis=0)
                acc = alpha * acc + jnp.dot(vt_ones, p, preferred_element_type=F32)
                carries[part] = (m_new, acc)
        return tuple(carries)

    pad = BF16_SUBLANES
    ones = jnp.ones((pad, bk), MXU_DTYPE)
    init = (jnp.full((1, bq), -jnp.inf, F32), jnp.zeros((dv + pad, bq), F32))
    carries = tuple(init for _ in q_parts)
    n_blocks = (qi + 1) * diag
    n_groups = (n_blocks + unroll - 1) // unroll
    carries = lax.fori_loop(0, n_groups - 1, lambda jg, c: group(jg, c, unroll, 0), carries)
    last = (n_blocks - (n_groups - 1) * unroll) // diag - 1
    tails = [functools.partial(group, count=(t + 1) * diag, masked=diag) for t in range(unroll // diag)]
    carries = lax.switch(last, [lambda c, f=f: f(n_groups - 1, c) for f in tails], carries)
    return [(acc[:dv], acc[dv:dv + 1]) for _, acc in carries]


KV_GROUP_KEYS = 1024


def _kv_group(seq, bq, bk):
    group = max(bq // bk, min(KV_GROUP_KEYS // bk, seq // bk))
    assert (seq // bk) % group == 0 and group % (bq // bk) == 0
    return group


DIFF_HEADS_PER_STEP = 2


def _diff_attn_kernel(lam_ref, q_ref, k_ref, vt_ref, g_ref, o_ref, *, bq, bk, lam_init, unroll):
    qi = pl.program_id(2)
    lam_v = lam_ref[...]
    lam = (jnp.exp(jnp.sum(lam_v[0:1] * lam_v[1:2], axis=1, keepdims=True))
           - jnp.exp(jnp.sum(lam_v[2:3] * lam_v[3:4], axis=1, keepdims=True)) + lam_init)
    hw = LANES
    is_q1 = (lax.broadcasted_iota(jnp.int32, (bq, hw), 1) & (DIFF_QK_DIM // 2)) == 0
    q_parts = []
    for hh in range(DIFF_HEADS_PER_STEP):
        q = q_ref[:, hh * hw:(hh + 1) * hw]
        zero = jnp.zeros_like(q)
        q_parts += [jnp.where(is_q1, q, zero), jnp.where(is_q1, zero, q)]

    def load_k(j, part):
        hh = part // 2
        return k_ref[pl.ds(pl.multiple_of(j * bk, bk), bk), hh * hw:(hh + 1) * hw]

    def load_vt(j, part):
        hh = part // 2
        return vt_ref[hh * DIFF_V_DIM:(hh + 1) * DIFF_V_DIM, pl.ds(pl.multiple_of(j * bk, bk), bk)]

    outs = _flash(q_parts, load_k, load_vt, qi, bq=bq, bk=bk, dv=DIFF_V_DIM, unroll=unroll)
    for hh in range(DIFF_HEADS_PER_STEP):
        (acc1, l1), (acc2, l2) = outs[2 * hh], outs[2 * hh + 1]
        o = acc1 * (1.0 / l1) - lam * (acc2 * (1.0 / l2))
        o = o * lax.rsqrt(jnp.mean(o * o, axis=0, keepdims=True) + EPS) * g_ref[...]
        o_ref[:, hh * DIFF_V_DIM:(hh + 1) * DIFF_V_DIM] = (o * (1.0 - lam_init)).T.astype(o_ref.dtype)


def _diff_attention(qk, vt, lam_vecs, g_sub, lam_init, batch, seq):
    m = qk.shape[0]
    bq = _pick(seq, 512, CHUNK)
    bk = _pick(bq, 512, CHUNK)
    nq = seq // bq
    hs = DIFF_HEADS_PER_STEP
    steps = DIFF_HEADS // hs
    unroll = _kv_group(seq, bq, bk)
    return pl.pallas_call(
        functools.partial(_diff_attn_kernel, bq=bq, bk=bk, lam_init=lam_init, unroll=unroll),
        grid=(batch, steps, nq),
        in_specs=[
            pl.BlockSpec((4, DIFF_QK_DIM), lambda b, h, i: (0, 0)),
            pl.BlockSpec((bq, hs * LANES), lambda b, h, i: (b * nq + i, h)),
            pl.BlockSpec((seq, hs * LANES), lambda b, h, i: (b, steps + h)),
            pl.BlockSpec((hs * DIFF_V_DIM, seq), lambda b, h, i: (h, b)),
            pl.BlockSpec((DIFF_V_DIM, 1), lambda b, h, i: (0, 0)),
        ],
        out_specs=pl.BlockSpec((bq, hs * DIFF_V_DIM), lambda b, h, i: (b * nq + i, h)),
        out_shape=jax.ShapeDtypeStruct((m, DIFF_WIDTH), MXU_DTYPE),
        compiler_params=_cparams("parallel", "parallel", "arbitrary"),
        name="diff_attention",
    )(lam_vecs, qk, qk, vt, g_sub[:, None])


MLA_HEADS_PER_STEP = 4


def _mla_attn_kernel(q_ref, kn_ref, kpe_ref, vt_ref, o_ref, kcat_ref, *, bq, bk, unroll):
    qi = pl.program_id(2)
    qw = 2 * LANES

    @pl.when(qi == 0)
    def _():
        for hh in range(MLA_HEADS_PER_STEP):
            kcat_ref[hh, :, :MLA_NOPE_DIM] = kn_ref[:, hh * MLA_NOPE_DIM:(hh + 1) * MLA_NOPE_DIM]
            kcat_ref[hh, :, MLA_NOPE_DIM:] = kpe_ref[...]

    def load_k(j, part):
        return kcat_ref[part, pl.ds(pl.multiple_of(j * bk, bk), bk), :]

    def load_vt(j, part):
        return vt_ref[part * MLA_V_DIM:(part + 1) * MLA_V_DIM, pl.ds(pl.multiple_of(j * bk, bk), bk)]

    q_parts = [q_ref[:, hh * qw:(hh + 1) * qw] for hh in range(MLA_HEADS_PER_STEP)]
    outs = _flash(q_parts, load_k, load_vt, qi, bq=bq, bk=bk, dv=MLA_V_DIM, unroll=unroll)
    for hh, (acc, l) in enumerate(outs):
        o_ref[:, hh * MLA_V_DIM:(hh + 1) * MLA_V_DIM] = (acc * (1.0 / l)).T.astype(o_ref.dtype)


def _mla_attention(q, k_nope, kpe_src, kpe_block, vt, batch, seq):
    m = q.shape[0]
    bq = _pick(seq, 512, CHUNK)
    bk = _pick(bq, 512, CHUNK)
    nq = seq // bq
    hs = MLA_HEADS_PER_STEP
    qw = 2 * LANES
    unroll = _kv_group(seq, bq, bk)
    return pl.pallas_call(
        functools.partial(_mla_attn_kernel, bq=bq, bk=bk, unroll=unroll),
        grid=(batch, MLA_HEADS // hs, nq),
        in_specs=[
            pl.BlockSpec((bq, hs * qw), lambda b, h, i: (b * nq + i, h)),
            pl.BlockSpec((seq, hs * MLA_NOPE_DIM), lambda b, h, i: (b, h)),
            pl.BlockSpec((seq, LANES), lambda b, h, i: (b, kpe_block)),
            pl.BlockSpec((hs * MLA_V_DIM, seq), lambda b, h, i: (h, b)),
        ],
        out_specs=pl.BlockSpec((bq, hs * MLA_V_DIM), lambda b, h, i: (b * nq + i, h)),
        out_shape=jax.ShapeDtypeStruct((m, MLA_WIDTH), MXU_DTYPE),
        scratch_shapes=[pltpu.VMEM((hs, seq, qw), MXU_DTYPE)],
        compiler_params=_cparams("parallel", "parallel", "arbitrary"),
        name="mla_attention",
    )(q, k_nope, kpe_src, vt)


def _merge_kernel(h_ref, od_ref, om_ref, wga_ref, wgb_ref, ba_ref, bb_ref, wd_ref, wm_ref, o_ref, *, sub, rows):
    for r in range(h_ref.shape[0] // rows):
        rs = pl.ds(r * rows, rows)
        h, od, om = h_ref[rs, :], od_ref[rs, :], om_ref[rs, :]
        for t in range(wd_ref.shape[1] // sub):
            cols = slice(t * sub, (t + 1) * sub)
            ga = jax.nn.sigmoid(jnp.dot(h, wga_ref[:, cols], preferred_element_type=F32) + ba_ref[:, cols])
            gb = jax.nn.sigmoid(jnp.dot(h, wgb_ref[:, cols], preferred_element_type=F32) + bb_ref[:, cols])
            a = jnp.dot(od, wd_ref[:, cols], preferred_element_type=F32)
            b = jnp.dot(om, wm_ref[:, cols], preferred_element_type=F32)
            o_ref[rs, cols] = (ga * a + gb * b).astype(o_ref.dtype)


def _merge(h, o_diff, o_mla, w_gate, b_gate, w_d, w_m):
    m, d = h.shape
    n = _wshape(w_d)[1]
    bm = _pick(m, 1024, BF16_SUBLANES)
    bn = _pick(n, 512)
    nb = n // bn
    lo = lambda i, j: (0, j)
    hi = lambda i, j: (0, j + nb)
    row = lambda i, j: (i, 0)
    return pl.pallas_call(
        functools.partial(_merge_kernel, sub=_pick(bn, MXU_COLS), rows=_pick(bm, PROJ_ROW_TILE, BF16_SUBLANES)),
        grid=(m // bm, nb),
        in_specs=[
            pl.BlockSpec((bm, d), row),
            pl.BlockSpec((bm, DIFF_WIDTH), row),
            pl.BlockSpec((bm, MLA_WIDTH), row),
            _wspec(w_gate, (d, bn), lo),
            _wspec(w_gate, (d, bn), hi),
            pl.BlockSpec((1, bn), lo),
            pl.BlockSpec((1, bn), hi),
            _wspec(w_d, (DIFF_WIDTH, bn), lo),
            _wspec(w_m, (MLA_WIDTH, bn), lo),
        ],
        out_specs=pl.BlockSpec((bm, bn), lambda i, j: (i, j)),
        out_shape=jax.ShapeDtypeStruct((m, n), MXU_DTYPE),
        compiler_params=_cparams("parallel", "arbitrary"),
        name="branch_merge",
    )(h, o_diff, o_mla, _layered(w_gate)[0], _layered(w_gate)[0], b_gate[None, :], b_gate[None, :],
      _layered(w_d)[0], _layered(w_m)[0])


def _residual_norm(y, x_ref, gp_ref, gn_ref, xo_ref, ho_ref):
    xn = x_ref[...] + _rms(y, gp_ref[...])
    xo_ref[...] = xn
    ho_ref[...] = _rms(xn, gn_ref[...]).astype(ho_ref.dtype)


def _out_proj_kernel(lhs_ref, w_ref, x_ref, gp_ref, gn_ref, xo_ref, ho_ref, *, row_tiles):
    rows = lhs_ref.shape[0] // row_tiles
    for t in range(row_tiles):
        r = pl.ds(t * rows, rows)
        y = jnp.dot(lhs_ref[r, :], w_ref[...], preferred_element_type=F32)
        _residual_norm(y, x_ref.at[r], gp_ref, gn_ref, xo_ref.at[r], ho_ref.at[r])


def _out_proj(lhs, w, x, g_post, g_next):
    m, kdim = lhs.shape
    n = _wshape(w)[1]
    bm = _pick(m, 512, BF16_SUBLANES)
    row = lambda i: (i, 0)
    const = lambda i: (0, 0)
    return pl.pallas_call(
        functools.partial(_out_proj_kernel, row_tiles=2 if bm % (2 * BF16_SUBLANES) == 0 else 1),
        grid=(m // bm,),
        in_specs=[
            pl.BlockSpec((bm, kdim), row),
            _wspec(w, (kdim, n), const),
            pl.BlockSpec((bm, n), row),
            pl.BlockSpec((1, n), const),
            pl.BlockSpec((1, n), const),
        ],
        out_specs=[pl.BlockSpec((bm, n), row), pl.BlockSpec((bm, n), row)],
        out_shape=[jax.ShapeDtypeStruct((m, n), F32), jax.ShapeDtypeStruct((m, n), MXU_DTYPE)],
        compiler_params=_cparams("parallel"),
        name="out_proj",
    )(lhs, _layered(w)[0], x, g_post[None, :], g_next[None, :])


def _mem_attn_kernel(h_ref, wq_ref, kv_ref, wo_ref, x_ref, gp_ref, gn_ref, xo_ref, ho_ref, o_ref, *, dh, scale):
    width = MEM_HEADS * dh
    h = h_ref[...]
    head_cols = [slice(hd * dh, (hd + 1) * dh) for hd in range(MEM_HEADS)]

    def project(cols):
        return (jnp.dot(h, wq_ref[:, cols], preferred_element_type=F32) * scale).astype(MXU_DTYPE)

    q = project(head_cols[0])
    for hd, cols in enumerate(head_cols):
        s = lax.dot_general(q, kv_ref[:, cols], _NT, preferred_element_type=F32)
        if hd + 1 < MEM_HEADS:
            q = project(head_cols[hd + 1])
        p = jnp.exp(s - s.max(axis=1, keepdims=True))
        l = p.sum(axis=1, keepdims=True)
        v = kv_ref[:, width + hd * dh:width + (hd + 1) * dh]
        o_ref[:, cols] = (jnp.dot(p.astype(MXU_DTYPE), v, preferred_element_type=F32) / l).astype(o_ref.dtype)
    row_tiles = 2 if o_ref.shape[0] % (2 * BF16_SUBLANES) == 0 else 1
    rows = o_ref.shape[0] // row_tiles
    for t in range(row_tiles):
        r = pl.ds(t * rows, rows)
        y = jnp.dot(o_ref[r, :], wo_ref[...], preferred_element_type=F32)
        _residual_norm(y, x_ref.at[r], gp_ref, gn_ref, xo_ref.at[r], ho_ref.at[r])


def _mem_attention(h, w_q, kv, w_o, x, g_post, g_next, seq, mem_tokens):
    m, d = h.shape
    dh = d // MEM_HEADS
    bq = _pick(seq, 512, BF16_SUBLANES)
    per_batch = seq // bq
    row = lambda i: (i, 0)
    const = lambda i: (0, 0)
    resident = lambda w: _wspec(w, (d, d), const, pipeline_mode=pl.Buffered(1))
    return pl.pallas_call(
        functools.partial(_mem_attn_kernel, dh=dh, scale=dh ** -0.5),
        grid=(m // bq,),
        in_specs=[pl.BlockSpec((bq, d), row),
                  resident(w_q),
                  pl.BlockSpec((mem_tokens, 2 * d), lambda i: (i // per_batch, 0)),
                  resident(w_o),
                  pl.BlockSpec((bq, d), row),
                  pl.BlockSpec((1, d), const),
                  pl.BlockSpec((1, d), const)],
        out_specs=[pl.BlockSpec((bq, d), row), pl.BlockSpec((bq, d), row)],
        out_shape=[jax.ShapeDtypeStruct((m, d), F32), jax.ShapeDtypeStruct((m, d), MXU_DTYPE)],
        scratch_shapes=[pltpu.VMEM((bq, d), MXU_DTYPE)],
        compiler_params=_cparams("parallel"),
        name="mem_attention",
    )(h, _layered(w_q)[0], kv, _layered(w_o)[0], x, g_post[None, :], g_next[None, :])


HALO = BF16_SUBLANES


def _ffn_kernel(h_ref, halo_ref, wa_ref, wg_ref, cwa_ref, cwg_ref, cba_ref, cbg_ref, wd_ref, x_ref, gp_ref,
                gn_ref, xo_ref, ho_ref, hs_ref, u_ref, acc_ref, *, bm, blocks_per_seq, nb, sub, row_tiles):
    i = pl.program_id(0)
    j = pl.program_id(1)

    @pl.when(j == 0)
    def _():
        halo = halo_ref[...]
        first = (i % blocks_per_seq) == 0
        hs_ref[:HALO, :] = jnp.where(first, jnp.zeros_like(halo), halo)
        hs_ref[HALO:, :] = h_ref[...]
        acc_ref[...] = jnp.zeros_like(acc_ref)

    rows = bm // row_tiles
    tiles = [slice(t * sub, (t + 1) * sub) for t in range(wa_ref.shape[1] // sub)]
    bounds = [0] + [HALO + (r + 1) * rows for r in range(row_tiles)]
    for r in range(row_tiles):
        rs = pl.ds(bounds[r], bounds[r + 1] - bounds[r])
        hs = hs_ref[rs, :]
        for t, cols in enumerate(tiles):
            u_ref[2 * t, rs, :] = jnp.dot(hs, wa_ref[:, cols], preferred_element_type=F32)
            u_ref[2 * t + 1, rs, :] = jnp.dot(hs, wg_ref[:, cols], preferred_element_type=F32)

    def conv(slot, cols, row0, rows, cw_ref, cb_ref):
        cw = cw_ref[:, cols]
        c = cb_ref[:, cols]
        for tap in range(CONV_WIDTH):
            c = c + u_ref[slot, pl.ds(row0 + HALO - (CONV_WIDTH - 1) + tap, rows), :] * cw[tap:tap + 1]
        return c

    for r in range(row_tiles):
        acts = []
        for t, cols in enumerate(tiles):
            a = conv(2 * t, cols, r * rows, rows, cwa_ref, cba_ref)
            g = conv(2 * t + 1, cols, r * rows, rows, cwg_ref, cbg_ref)
            acts.append((a * jax.nn.sigmoid(a) * g).astype(MXU_DTYPE))
        acc_ref[pl.ds(r * rows, rows), :] += jnp.dot(jnp.concatenate(acts, axis=1), wd_ref[...],
                                                     preferred_element_type=F32)

    @pl.when(j == nb - 1)
    def _():
        _residual_norm(acc_ref[...], x_ref, gp_ref, gn_ref, xo_ref, ho_ref)


def _ffn(h, w_up, conv_w, conv_b, w_down, x, g_post, g_next, seq):
    m, d = h.shape
    ff = _wshape(w_up)[1] // 2
    bm = _pick(seq, 512, HALO)
    bn = _pick(ff, 512)
    sub = _pick(bn, MXU_COLS)
    nb = ff // bn
    halo_blocks = bm // HALO
    row = lambda i, j: (i, 0)
    const = lambda i, j: (0, 0)
    up_a = lambda i, j: (0, j)
    up_g = lambda i, j: (0, j + nb)
    return pl.pallas_call(
        functools.partial(_ffn_kernel, bm=bm, blocks_per_seq=seq // bm, nb=nb, sub=sub,
                          row_tiles=2 if bm % (2 * BF16_SUBLANES) == 0 else 1),
        grid=(m // bm, nb),
        in_specs=[
            pl.BlockSpec((bm, d), row),
            pl.BlockSpec((HALO, d), lambda i, j: (jnp.maximum(i * halo_blocks - 1, 0), 0)),
            _wspec(w_up, (d, bn), up_a),
            _wspec(w_up, (d, bn), up_g),
            pl.BlockSpec((CONV_WIDTH, bn), up_a),
            pl.BlockSpec((CONV_WIDTH, bn), up_g),
            pl.BlockSpec((1, bn), up_a),
            pl.BlockSpec((1, bn), up_g),
            _wspec(w_down, (bn, d), lambda i, j: (j, 0)),
            pl.BlockSpec((bm, d), row),
            pl.BlockSpec((1, d), const),
            pl.BlockSpec((1, d), const),
        ],
        out_specs=[pl.BlockSpec((bm, d), row), pl.BlockSpec((bm, d), row)],
        out_shape=[jax.ShapeDtypeStruct((m, d), F32), jax.ShapeDtypeStruct((m, d), MXU_DTYPE)],
        scratch_shapes=[pltpu.VMEM((HALO + bm, d), MXU_DTYPE),
                        pltpu.VMEM((2 * (bn // sub), HALO + bm, sub), F32),
                        pltpu.VMEM((bm, d), F32)],
        compiler_params=_cparams("parallel", "arbitrary"),
        name="ffn",
    )(h, h, _layered(w_up)[0], _layered(w_up)[0], conv_w, conv_w, conv_b[None, :], conv_b[None, :],
      _layered(w_down)[0], x, g_post[None, :], g_next[None, :])


def kernel(x, mem, positions, g_pre_mix, w_in, b_gate, lam_q1, lam_k1, lam_q2, lam_k2, g_diff_sub, g_cq,
           w_uq, g_ckv, w_ukv, w_br_diff, w_br_mla, w_mix_out, g_post_mix, g_pre_x, g_mem, w_q_x, w_kv_x,
           w_o_x, g_post_x, g_pre_ffn, w_up, conv_w, conv_b, w_down, g_post_ffn):
    batch, seq, d = x.shape
    mem_tokens = mem.shape[1]
    depth = w_in.shape[0]
    m = batch * seq
    cast = lambda a: a.astype(MXU_DTYPE)

    qw = DIFF_HEADS * 2 * DIFF_QK_DIM
    o_v = 2 * qw
    o_cq = o_v + DIFF_WIDTH
    o_ckv = o_cq + MLA_Q_RANK
    o_kr = o_ckv + MLA_KV_RANK
    o_gt = o_kr + MLA_ROPE_DIM
    assert w_in.shape[2] == o_gt + 2 * d
    assert MLA_Q_RANK % MLA_KV_RANK == 0
    qhead = MLA_NOPE_DIM + MLA_ROPE_DIM
    lat_width = -(-(o_kr - o_cq + LANES) // MXU_COLS) * MXU_COLS
    kvhead = MLA_NOPE_DIM + MLA_V_DIM

    cos, sin = _rope_tables(positions)
    xf = x.reshape(m, d)
    memf = mem.reshape(batch * mem_tokens, d)
    h = _rmsnorm(xf, g_pre_mix[0])
    w_qk_all, w_vt_all, w_lat_all, w_gate_all = _split_w_in(
        w_in, lat_width, qw=qw, o_v=o_v, o_cq=o_cq, o_kr=o_kr, o_gt=o_gt)

    w_br_diff_c, w_br_mla_c, w_mix_out_c = cast(w_br_diff), cast(w_br_mla), cast(w_mix_out)
    w_q_x_c, w_kv_x_c, w_o_x_c = cast(w_q_x), cast(w_kv_x), cast(w_o_x)
    w_up_c, w_down_c = cast(w_up), cast(w_down)

    for l in range(depth):
        lam_init = 0.8 - 0.6 * math.exp(-0.3 * l)
        w_qk, w_vt, w_lat, w_gate = (w_qk_all, l), (w_vt_all, l), (w_lat_all, l), (w_gate_all, l)
        w_uq_h = jnp.pad(w_uq[l].reshape(MLA_Q_RANK, MLA_HEADS, qhead),
                         ((0, 0), (0, 0), (0, 2 * LANES - qhead)))
        w_uq_p = cast(jnp.concatenate([w_uq_h[:, :, :MLA_NOPE_DIM], _pair_halves(w_uq_h[:, :, MLA_NOPE_DIM:])],
                                      axis=2).reshape(MLA_Q_RANK, -1))
        w_ukv_h = w_ukv[l].reshape(MLA_KV_RANK, MLA_HEADS, kvhead)
        w_uk = cast(w_ukv_h[:, :, :MLA_NOPE_DIM].reshape(MLA_KV_RANK, -1))
        w_uvt = cast(w_ukv_h[:, :, MLA_NOPE_DIM:].reshape(MLA_KV_RANK, -1).T)

        bn_qk = _pick(2 * qw, 512)
        qk = _proj(h, w_qk, functools.partial(_ep_rope, q_tiles=qw // bn_qk), name="proj_qk",
                   extras=(cos, sin), extra_specs=_rope_specs)
        vt = _proj_t(h, w_vt, name="proj_vt")
        lat = _proj(h, w_lat, functools.partial(_ep_rope, rope_from=o_kr - o_cq), name="proj_latent",
                    extras=(cos, sin), extra_specs=_rope_specs, bn_pref=lat_width)
        q_mla = _latent_q(lat, 0, g_cq[l], w_uq_p, cos, sin, qhead ** -0.5 * LOG2E)
        k_mla, vt_mla = _latent_kv(lat, MLA_Q_RANK // MLA_KV_RANK, g_ckv[l], w_uk, w_uvt)
        lam_vecs = jnp.stack([lam_q1[l], lam_k1[l], lam_q2[l], lam_k2[l]])
        o_diff = _diff_attention(qk, vt, lam_vecs, g_diff_sub[l], lam_init, batch, seq)
        o_mla = _mla_attention(q_mla, k_mla, lat, (o_kr - o_cq) // LANES, vt_mla, batch, seq)
        merged = _merge(h, o_diff, o_mla, w_gate, b_gate[l], (w_br_diff_c, l), (w_br_mla_c, l))
        xf, h = _out_proj(merged, (w_mix_out_c, l), xf, g_post_mix[l], g_pre_x[l])

        kv_x = _proj(_rmsnorm(memf, g_mem[l]), (w_kv_x_c, l), _ep_plain, name="proj_kvx")
        xf, h = _mem_attention(h, (w_q_x_c, l), kv_x, (w_o_x_c, l), xf, g_post_x[l], g_pre_ffn[l],
                               seq, mem_tokens)

        g_next = g_pre_mix[l + 1] if l + 1 < depth else g_pre_mix[0]
        xf, h = _ffn(h, (w_up_c, l), conv_w[l], conv_b[l], (w_down_c, l), xf, g_post_ffn[l], g_next, seq)

    return xf.reshape(batch, seq, d)
```

```python
import functools
import math

import jax
import jax.numpy as jnp
from jax import lax
from jax.experimental import pallas as pl
from jax.experimental.pallas import tpu as pltpu

CHUNK = 64
ROPE_THETA = 10000.0
EPS = 1e-6
DIFF_HEADS = 8
DIFF_QK_DIM = 64
DIFF_V_DIM = 128
MLA_HEADS = 8
MLA_Q_RANK = 512
MLA_KV_RANK = 256
MLA_NOPE_DIM = 128
MLA_ROPE_DIM = 64
MLA_V_DIM = 128
MEM_HEADS = 4
CONV_WIDTH = 3
DIFF_WIDTH = DIFF_HEADS * DIFF_V_DIM
MLA_WIDTH = MLA_HEADS * MLA_V_DIM

LANES = 128
BF16_SUBLANES = 16
MXU_COLS = 256
VMEM_LIMIT_BYTES = 56 * 2**20

MXU_DTYPE = jnp.bfloat16
F32 = jnp.float32
LOG2E = math.log2(math.e)


def _pick(n, pref, mult=LANES):
    if n <= pref:
        return n
    best = None
    for d in range(mult, pref + 1, mult):
        if n % d == 0:
            best = d
    assert best is not None, (n, pref, mult)
    return best


def _cparams(*sem):
    return pltpu.CompilerParams(dimension_semantics=sem, vmem_limit_bytes=VMEM_LIMIT_BYTES)


def _layered(w):
    return w if isinstance(w, tuple) else (w, None)


def _wshape(w):
    return _layered(w)[0].shape[-2:]


def _wspec(w, block, index_map, **kwargs):
    layer = _layered(w)[1]
    if layer is None:
        return pl.BlockSpec(block, index_map, **kwargs)
    return pl.BlockSpec((None, *block), lambda *g: (layer, *index_map(*g)), **kwargs)


def _rms(x, g):
    return x * lax.rsqrt(jnp.mean(x * x, axis=-1, keepdims=True) + EPS) * g


def _rope128(t, cos, sin):
    return t * cos + pltpu.roll(t, LANES // 2, 1) * sin


def _pair_halves(w):
    q = LANES // 4
    shape = w.shape
    w = w.reshape(*shape[:-1], shape[-1] // LANES, 2, 2, q)
    return jnp.swapaxes(w, -3, -2).reshape(shape)


_NT = (((1,), (1,)), ((), ()))


def _w_in_kernel(w_ref, qk_ref, vt_ref, lat_ref, gate_ref, *, qw, o_v, o_cq, o_kr, o_gt):
    x = w_ref[...]
    q = LANES // 4
    qk = x[:, :o_v]
    col = lax.broadcasted_iota(jnp.int32, qk.shape, 1)
    qk = jnp.where(col < qw, qk * (DIFF_QK_DIM ** -0.5), qk)
    lane = col % LANES
    qk = jnp.where((lane >= q) & (lane < 2 * q), pltpu.roll(qk, o_v - q, 1),
                   jnp.where((lane >= 2 * q) & (lane < 3 * q), pltpu.roll(qk, q, 1), qk))
    qk_ref[...] = qk.astype(qk_ref.dtype)
    vt_ref[...] = x[:, o_v:o_cq].astype(vt_ref.dtype)
    t = x[:, o_kr:o_kr + LANES]
    lane = lax.broadcasted_iota(jnp.int32, t.shape, 1)
    kpe = jnp.where(lane < q, t, jnp.where((lane >= 2 * q) & (lane < 3 * q), pltpu.roll(t, q, 1), 0.0))
    pad = jnp.zeros((x.shape[0], lat_ref.shape[1] - (o_kr - o_cq) - LANES), F32)
    lat_ref[...] = jnp.concatenate([x[:, o_cq:o_kr], kpe, pad], axis=1).astype(lat_ref.dtype)
    gate_ref[...] = x[:, o_gt:].astype(gate_ref.dtype)


def _split_w_in(w_in, lat_width, *, qw, o_v, o_cq, o_kr, o_gt):
    depth, d, n_in = w_in.shape
    assert MLA_ROPE_DIM == 2 * (LANES // 4) and o_kr % LANES == 0 and o_kr + LANES <= n_in
    rows = _pick(d, 256, LANES)
    block = lambda width: pl.BlockSpec((None, rows, width), lambda l, r: (l, r, 0))
    return pl.pallas_call(
        functools.partial(_w_in_kernel, qw=qw, o_v=o_v, o_cq=o_cq, o_kr=o_kr, o_gt=o_gt),
        grid=(depth, d // rows),
        in_specs=[block(n_in)],
        out_specs=[block(o_v), block(o_cq - o_v),
                   block(lat_width), block(n_in - o_gt)],
        out_shape=[jax.ShapeDtypeStruct((depth, d, o_v), MXU_DTYPE),
                   jax.ShapeDtypeStruct((depth, d, o_cq - o_v), MXU_DTYPE),
                   jax.ShapeDtypeStruct((depth, d, lat_width), MXU_DTYPE),
                   jax.ShapeDtypeStruct((depth, d, n_in - o_gt), MXU_DTYPE)],
        compiler_params=_cparams("parallel", "parallel"),
        name="split_w_in",
    )(w_in)


def _tables_kernel(pos_ref, inv_ref, sgn_ref, cos_ref, sin_ref):
    ang = pos_ref[...].astype(F32) * inv_ref[...]
    cos_ref[...] = jnp.cos(ang)
    sin_ref[...] = jnp.sin(ang) * sgn_ref[...]


def _rope_tables(positions):
    m = positions.size
    d = DIFF_QK_DIM
    inv = ROPE_THETA ** (-jnp.arange(0, d, 2, dtype=F32) / d)
    inv128 = jnp.tile(inv, LANES // (d // 2))[None, :]
    sgn128 = jnp.concatenate([-jnp.ones((LANES // 2,), F32), jnp.ones((LANES // 2,), F32)])[None, :]
    bm = _pick(m, 2048, 8)
    row = pl.BlockSpec((bm, LANES), lambda i: (i, 0))
    const = pl.BlockSpec((1, LANES), lambda i: (0, 0))
    return pl.pallas_call(
        _tables_kernel,
        grid=(m // bm,),
        in_specs=[pl.BlockSpec((bm, 1), lambda i: (i, 0)), const, const],
        out_specs=[row, row],
        out_shape=[jax.ShapeDtypeStruct((m, LANES), F32)] * 2,
        compiler_params=_cparams("parallel"),
        name="rope_tables",
    )(positions.reshape(m, 1), inv128, sgn128)


def _rmsnorm_kernel(x_ref, g_ref, o_ref):
    o_ref[...] = _rms(x_ref[...], g_ref[...]).astype(o_ref.dtype)


def _rmsnorm(x, g):
    m, d = x.shape
    bm = _pick(m, 512, 8)
    return pl.pallas_call(
        _rmsnorm_kernel,
        grid=(m // bm,),
        in_specs=[pl.BlockSpec((bm, d), lambda i: (i, 0)), pl.BlockSpec((1, d), lambda i: (0, 0))],
        out_specs=pl.BlockSpec((bm, d), lambda i: (i, 0)),
        out_shape=jax.ShapeDtypeStruct((m, d), MXU_DTYPE),
        compiler_params=_cparams("parallel"),
        name="rmsnorm",
    )(x, g[None, :])


PROJ_ROW_TILE = 256


def _proj_kernel(h_ref, w_ref, *rest, epilogue, sub, rows):
    *extra, o_ref = rest
    for r in range(h_ref.shape[0] // rows):
        rs = pl.ds(r * rows, rows)
        h = h_ref[rs, :]
        for t in range(w_ref.shape[1] // sub):
            cols = slice(t * sub, (t + 1) * sub)
            acc = jnp.dot(h, w_ref[:, cols], preferred_element_type=F32)
            o_ref[rs, cols] = epilogue(acc, rs, cols, *extra).astype(o_ref.dtype)


def _ep_plain(acc, rs, cols):
    return acc


def _ep_rope(acc, rs, cols, cos_ref, sin_ref, *, q_tiles=0, rope_from=0):
    if cols.start < rope_from:
        return acc
    cos, sin = cos_ref[rs, :], sin_ref[rs, :]
    n = acc.shape[1] // LANES
    out = jnp.concatenate(
        [_rope128(acc[:, c * LANES:(c + 1) * LANES], cos, sin) for c in range(n)], axis=1)
    if q_tiles:
        out = out * jnp.where(pl.program_id(1) < q_tiles, LOG2E, 1.0)
    return out


def _proj(h, w, epilogue, *, name, extras=(), extra_specs=None, bm_pref=1024, bn_pref=512):
    m, k = h.shape
    n = _wshape(w)[1]
    bm = _pick(m, bm_pref, BF16_SUBLANES)
    bn = _pick(n, bn_pref)
    specs = [pl.BlockSpec((bm, k), lambda i, j: (i, 0)), _wspec(w, (k, bn), lambda i, j: (0, j))]
    specs += list(extra_specs(bm, bn)) if extra_specs else []
    return pl.pallas_call(
        functools.partial(_proj_kernel, epilogue=epilogue, sub=_pick(bn, MXU_COLS),
                          rows=_pick(bm, PROJ_ROW_TILE, BF16_SUBLANES)),
        grid=(m // bm, n // bn),
        in_specs=specs,
        out_specs=pl.BlockSpec((bm, bn), lambda i, j: (i, j)),
        out_shape=jax.ShapeDtypeStruct((m, n), MXU_DTYPE),
        compiler_params=_cparams("parallel", "arbitrary"),
        name=name,
    )(h, _layered(w)[0], *extras)


def _rope_specs(bm, bn):
    tab = pl.BlockSpec((bm, LANES), lambda i, j: (i, 0))
    return [tab, tab]


def _proj_t_kernel(h_ref, w_ref, o_ref):
    o_ref[...] = jnp.dot(h_ref[...], w_ref[...], preferred_element_type=F32).T.astype(o_ref.dtype)


def _proj_t(h, w, *, name):
    m, k = h.shape
    n = _wshape(w)[1]
    bm = _pick(m, 1024)
    bn = _pick(n, 512)
    return pl.pallas_call(
        _proj_t_kernel,
        grid=(m // bm, n // bn),
        in_specs=[pl.BlockSpec((bm, k), lambda i, j: (i, 0)), _wspec(w, (k, bn), lambda i, j: (0, j))],
        out_specs=pl.BlockSpec((bn, bm), lambda i, j: (j, i)),
        out_shape=jax.ShapeDtypeStruct((n, m), MXU_DTYPE),
        compiler_params=_cparams("parallel", "arbitrary"),
        name=name,
    )(h, _layered(w)[0])


def _latent_q_kernel(c_ref, g_ref, w_ref, cos_ref, sin_ref, o_ref, *, scale):
    cn = _rms(c_ref[...].astype(F32), g_ref[...]).astype(MXU_DTYPE)
    acc = jnp.dot(cn, w_ref[...], preferred_element_type=F32)
    cos, sin = cos_ref[...], sin_ref[...]
    cols = []
    for c in range(acc.shape[1] // LANES):
        t = acc[:, c * LANES:(c + 1) * LANES]
        cols.append(_rope128(t, cos, sin) if c % 2 else t)
    o_ref[...] = (jnp.concatenate(cols, axis=1) * scale).astype(o_ref.dtype)


def _latent_kv_kernel(c_ref, g_ref, wk_ref, wvt_ref, k_ref, vt_ref):
    cn = _rms(c_ref[...].astype(F32), g_ref[...]).astype(MXU_DTYPE)
    k_ref[...] = jnp.dot(cn, wk_ref[...], preferred_element_type=F32).astype(k_ref.dtype)
    vt_ref[...] = lax.dot_general(wvt_ref[...], cn, _NT, preferred_element_type=F32).astype(vt_ref.dtype)


def _latent_q(src, col_block, g, w, cos, sin, scale):
    m = src.shape[0]
    rank, n = w.shape
    bm = _pick(m, 512, BF16_SUBLANES)
    tab = pl.BlockSpec((bm, LANES), lambda i: (i, 0))
    return pl.pallas_call(
        functools.partial(_latent_q_kernel, scale=scale),
        grid=(m // bm,),
        in_specs=[pl.BlockSpec((bm, rank), lambda i: (i, col_block)),
                  pl.BlockSpec((1, rank), lambda i: (0, 0)),
                  pl.BlockSpec((rank, n), lambda i: (0, 0)), tab, tab],
        out_specs=pl.BlockSpec((bm, n), lambda i: (i, 0)),
        out_shape=jax.ShapeDtypeStruct((m, n), MXU_DTYPE),
        compiler_params=_cparams("parallel"),
        name="mla_q",
    )(src, g[None, :], w, cos, sin)


def _latent_kv(src, col_block, g, wk, wvt):
    m = src.shape[0]
    rank, nk = wk.shape
    nv = wvt.shape[0]
    bm = _pick(m, 512)
    return pl.pallas_call(
        _latent_kv_kernel,
        grid=(m // bm,),
        in_specs=[pl.BlockSpec((bm, rank), lambda i: (i, col_block)),
                  pl.BlockSpec((1, rank), lambda i: (0, 0)),
                  pl.BlockSpec((rank, nk), lambda i: (0, 0)),
                  pl.BlockSpec((nv, rank), lambda i: (0, 0))],
        out_specs=[pl.BlockSpec((bm, nk), lambda i: (i, 0)), pl.BlockSpec((nv, bm), lambda i: (0, i))],
        out_shape=[jax.ShapeDtypeStruct((m, nk), MXU_DTYPE), jax.ShapeDtypeStruct((nv, m), MXU_DTYPE)],
        compiler_params=_cparams("parallel"),
        name="mla_kv",
    )(src, g[None, :], wk, wvt)


def _flash(q_parts, load_k, load_vt, qi, *, bq, bk, dv, unroll):
    diag = bq // bk
    assert unroll % diag == 0
    rel = (lax.broadcasted_iota(jnp.int32, (bk, bq), 0) // CHUNK
           - lax.broadcasted_iota(jnp.int32, (bk, bq), 1) // CHUNK)

    def group(jg, carries, count, masked):
        blocks = [jg * unroll + u for u in range(count)]
        scores = [[lax.dot_general(load_k(j, part), q, _NT, preferred_element_type=F32)
                   for part, q in enumerate(q_parts)] for j in blocks]
        carries = list(carries)
        for u, j in enumerate(blocks):
            d = u - (count - masked)
            for part, s in enumerate(scores[u]):
                m, l, acc = carries[part]
                if d >= 0:
                    s = jnp.where(rel <= -d * (bk // CHUNK), s, -jnp.inf)
                m_new = jnp.maximum(m, s.max(axis=0, keepdims=True))
                alpha = jnp.exp2(m - m_new)
                p = jnp.exp2(s - m_new)
                l = alpha * l + p.sum(axis=0, keepdims=True)
                acc = alpha * acc + jnp.dot(load_vt(j, part), p.astype(MXU_DTYPE),
                                            preferred_element_type=F32)
                carries[part] = (m_new, l, acc)
        return tuple(carries)

    init = (jnp.full((1, bq), -jnp.inf, F32), jnp.zeros((1, bq), F32), jnp.zeros((dv, bq), F32))
    carries = tuple(init for _ in q_parts)
    n_blocks = (qi + 1) * diag
    n_groups = (n_blocks + unroll - 1) // unroll
    carries = lax.fori_loop(0, n_groups - 1, lambda jg, c: group(jg, c, unroll, 0), carries)
    last = (n_blocks - (n_groups - 1) * unroll) // diag - 1
    tails = [functools.partial(group, count=(t + 1) * diag, masked=diag) for t in range(unroll // diag)]
    carries = lax.switch(last, [lambda c, f=f: f(n_groups - 1, c) for f in tails], carries)
    return [(acc, l) for _, l, acc in carries]


KV_GROUP_KEYS = 1024


def _kv_group(seq, bq, bk):
    group = max(bq // bk, min(KV_GROUP_KEYS // bk, seq // bk))
    assert (seq // bk) % group == 0 and group % (bq // bk) == 0
    return group


DIFF_HEADS_PER_STEP = 2


def _diff_attn_kernel(lam_ref, q_ref, k_ref, vt_ref, g_ref, o_ref, *, bq, bk, lam_init, unroll):
    qi = pl.program_id(2)
    lam_v = lam_ref[...]
    lam = (jnp.exp(jnp.sum(lam_v[0:1] * lam_v[1:2], axis=1, keepdims=True))
           - jnp.exp(jnp.sum(lam_v[2:3] * lam_v[3:4], axis=1, keepdims=True)) + lam_init)
    hw = LANES
    is_q1 = (lax.broadcasted_iota(jnp.int32, (bq, hw), 1) & (DIFF_QK_DIM // 2)) == 0
    q_parts = []
    for hh in range(DIFF_HEADS_PER_STEP):
        q = q_ref[:, hh * hw:(hh + 1) * hw]
        zero = jnp.zeros_like(q)
        q_parts += [jnp.where(is_q1, q, zero), jnp.where(is_q1, zero, q)]

    def load_k(j, part):
        hh = part // 2
        return k_ref[pl.ds(pl.multiple_of(j * bk, bk), bk), hh * hw:(hh + 1) * hw]

    def load_vt(j, part):
        hh = part // 2
        return vt_ref[hh * DIFF_V_DIM:(hh + 1) * DIFF_V_DIM, pl.ds(pl.multiple_of(j * bk, bk), bk)]

    outs = _flash(q_parts, load_k, load_vt, qi, bq=bq, bk=bk, dv=DIFF_V_DIM, unroll=unroll)
    for hh in range(DIFF_HEADS_PER_STEP):
        (acc1, l1), (acc2, l2) = outs[2 * hh], outs[2 * hh + 1]
        o = acc1 * (1.0 / l1) - lam * (acc2 * (1.0 / l2))
        o = o * lax.rsqrt(jnp.mean(o * o, axis=0, keepdims=True) + EPS) * g_ref[...]
        o_ref[:, hh * DIFF_V_DIM:(hh + 1) * DIFF_V_DIM] = (o * (1.0 - lam_init)).T.astype(o_ref.dtype)


def _diff_attention(qk, vt, lam_vecs, g_sub, lam_init, batch, seq):
    m = qk.shape[0]
    bq = _pick(seq, 512, CHUNK)
    bk = _pick(bq, 512, CHUNK)
    nq = seq // bq
    hs = DIFF_HEADS_PER_STEP
    steps = DIFF_HEADS // hs
    unroll = _kv_group(seq, bq, bk)
    return pl.pallas_call(
        functools.partial(_diff_attn_kernel, bq=bq, bk=bk, lam_init=lam_init, unroll=unroll),
        grid=(batch, steps, nq),
        in_specs=[
            pl.BlockSpec((4, DIFF_QK_DIM), lambda b, h, i: (0, 0)),
            pl.BlockSpec((bq, hs * LANES), lambda b, h, i: (b * nq + i, h)),
            pl.BlockSpec((seq, hs * LANES), lambda b, h, i: (b, steps + h)),
            pl.BlockSpec((hs * DIFF_V_DIM, seq), lambda b, h, i: (h, b)),
            pl.BlockSpec((DIFF_V_DIM, 1), lambda b, h, i: (0, 0)),
        ],
        out_specs=pl.BlockSpec((bq, hs * DIFF_V_DIM), lambda b, h, i: (b * nq + i, h)),
        out_shape=jax.ShapeDtypeStruct((m, DIFF_WIDTH), MXU_DTYPE),
        compiler_params=_cparams("parallel", "parallel", "arbitrary"),
        name="diff_attention",
    )(lam_vecs, qk, qk, vt, g_sub[:, None])


MLA_HEADS_PER_STEP = 4


def _mla_attn_kernel(q_ref, kn_ref, kpe_ref, vt_ref, o_ref, kcat_ref, *, bq, bk, unroll):
    qi = pl.program_id(2)
    qw = 2 * LANES

    @pl.when(qi == 0)
    def _():
        for hh in range(MLA_HEADS_PER_STEP):
            kcat_ref[hh, :, :MLA_NOPE_DIM] = kn_ref[:, hh * MLA_NOPE_DIM:(hh + 1) * MLA_NOPE_DIM]
            kcat_ref[hh, :, MLA_NOPE_DIM:] = kpe_ref[...]

    def load_k(j, part):
        return kcat_ref[part, pl.ds(pl.multiple_of(j * bk, bk), bk), :]

    def load_vt(j, part):
        return vt_ref[part * MLA_V_DIM:(part + 1) * MLA_V_DIM, pl.ds(pl.multiple_of(j * bk, bk), bk)]

    q_parts = [q_ref[:, hh * qw:(hh + 1) * qw] for hh in range(MLA_HEADS_PER_STEP)]
    outs = _flash(q_parts, load_k, load_vt, qi, bq=bq, bk=bk, dv=MLA_V_DIM, unroll=unroll)
    for hh, (acc, l) in enumerate(outs):
        o_ref[:, hh * MLA_V_DIM:(hh + 1) * MLA_V_DIM] = (acc * (1.0 / l)).T.astype(o_ref.dtype)


def _mla_attention(q, k_nope, kpe_src, kpe_block, vt, batch, seq):
    m = q.shape[0]
    bq = _pick(seq, 512, CHUNK)
    bk = _pick(bq, 512, CHUNK)
    nq = seq // bq
    hs = MLA_HEADS_PER_STEP
    qw = 2 * LANES
    unroll = _kv_group(seq, bq, bk)
    return pl.pallas_call(
        functools.partial(_mla_attn_kernel, bq=bq, bk=bk, unroll=unroll),
        grid=(batch, MLA_HEADS // hs, nq),
        in_specs=[
            pl.BlockSpec((bq, hs * qw), lambda b, h, i: (b * nq + i, h)),
            pl.BlockSpec((seq, hs * MLA_NOPE_DIM), lambda b, h, i: (b, h)),
            pl.BlockSpec((seq, LANES), lambda b, h, i: (b, kpe_block)),
            pl.BlockSpec((hs * MLA_V_DIM, seq), lambda b, h, i: (h, b)),
        ],
        out_specs=pl.BlockSpec((bq, hs * MLA_V_DIM), lambda b, h, i: (b * nq + i, h)),
        out_shape=jax.ShapeDtypeStruct((m, MLA_WIDTH), MXU_DTYPE),
        scratch_shapes=[pltpu.VMEM((hs, seq, qw), MXU_DTYPE)],
        compiler_params=_cparams("parallel", "parallel", "arbitrary"),
        name="mla_attention",
    )(q, k_nope, kpe_src, vt)


def _merge_kernel(h_ref, od_ref, om_ref, wga_ref, wgb_ref, ba_ref, bb_ref, wd_ref, wm_ref, o_ref, *, sub, rows):
    for r in range(h_ref.shape[0] // rows):
        rs = pl.ds(r * rows, rows)
        h, od, om = h_ref[rs, :], od_ref[rs, :], om_ref[rs, :]
        for t in range(wd_ref.shape[1] // sub):
            cols = slice(t * sub, (t + 1) * sub)
            ga = jax.nn.sigmoid(jnp.dot(h, wga_ref[:, cols], preferred_element_type=F32) + ba_ref[:, cols])
            gb = jax.nn.sigmoid(jnp.dot(h, wgb_ref[:, cols], preferred_element_type=F32) + bb_ref[:, cols])
            a = jnp.dot(od, wd_ref[:, cols], preferred_element_type=F32)
            b = jnp.dot(om, wm_ref[:, cols], preferred_element_type=F32)
            o_ref[rs, cols] = (ga * a + gb * b).astype(o_ref.dtype)


def _merge(h, o_diff, o_mla, w_gate, b_gate, w_d, w_m):
    m, d = h.shape
    n = _wshape(w_d)[1]
    bm = _pick(m, 1024, BF16_SUBLANES)
    bn = _pick(n, 512)
    nb = n // bn
    lo = lambda i, j: (0, j)
    hi = lambda i, j: (0, j + nb)
    row = lambda i, j: (i, 0)
    return pl.pallas_call(
        functools.partial(_merge_kernel, sub=_pick(bn, MXU_COLS), rows=_pick(bm, PROJ_ROW_TILE, BF16_SUBLANES)),
        grid=(m // bm, nb),
        in_specs=[
            pl.BlockSpec((bm, d), row),
            pl.BlockSpec((bm, DIFF_WIDTH), row),
            pl.BlockSpec((bm, MLA_WIDTH), row),
            _wspec(w_gate, (d, bn), lo),
            _wspec(w_gate, (d, bn), hi),
            pl.BlockSpec((1, bn), lo),
            pl.BlockSpec((1, bn), hi),
            _wspec(w_d, (DIFF_WIDTH, bn), lo),
            _wspec(w_m, (MLA_WIDTH, bn), lo),
        ],
        out_specs=pl.BlockSpec((bm, bn), lambda i, j: (i, j)),
        out_shape=jax.ShapeDtypeStruct((m, n), MXU_DTYPE),
        compiler_params=_cparams("parallel", "arbitrary"),
        name="branch_merge",
    )(h, o_diff, o_mla, _layered(w_gate)[0], _layered(w_gate)[0], b_gate[None, :], b_gate[None, :],
      _layered(w_d)[0], _layered(w_m)[0])


def _residual_norm(y, x_ref, gp_ref, gn_ref, xo_ref, ho_ref):
    xn = x_ref[...] + _rms(y, gp_ref[...])
    xo_ref[...] = xn
    ho_ref[...] = _rms(xn, gn_ref[...]).astype(ho_ref.dtype)


def _out_proj_kernel(lhs_ref, w_ref, x_ref, gp_ref, gn_ref, xo_ref, ho_ref, *, row_tiles):
    rows = lhs_ref.shape[0] // row_tiles
    for t in range(row_tiles):
        r = pl.ds(t * rows, rows)
        y = jnp.dot(lhs_ref[r, :], w_ref[...], preferred_element_type=F32)
        _residual_norm(y, x_ref.at[r], gp_ref, gn_ref, xo_ref.at[r], ho_ref.at[r])


def _out_proj(lhs, w, x, g_post, g_next):
    m, kdim = lhs.shape
    n = _wshape(w)[1]
    bm = _pick(m, 512, BF16_SUBLANES)
    row = lambda i: (i, 0)
    const = lambda i: (0, 0)
    return pl.pallas_call(
        functools.partial(_out_proj_kernel, row_tiles=2 if bm % (2 * BF16_SUBLANES) == 0 else 1),
        grid=(m // bm,),
        in_specs=[
            pl.BlockSpec((bm, kdim), row),
            _wspec(w, (kdim, n), const),
            pl.BlockSpec((bm, n), row),
            pl.BlockSpec((1, n), const),
            pl.BlockSpec((1, n), const),
        ],
        out_specs=[pl.BlockSpec((bm, n), row), pl.BlockSpec((bm, n), row)],
        out_shape=[jax.ShapeDtypeStruct((m, n), F32), jax.ShapeDtypeStruct((m, n), MXU_DTYPE)],
        compiler_params=_cparams("parallel"),
        name="out_proj",
    )(lhs, _layered(w)[0], x, g_post[None, :], g_next[None, :])


def _mem_attn_kernel(h_ref, wq_ref, kv_ref, wo_ref, x_ref, gp_ref, gn_ref, xo_ref, ho_ref, o_ref, *, dh, scale):
    width = MEM_HEADS * dh
    h = h_ref[...]
    head_cols = [slice(hd * dh, (hd + 1) * dh) for hd in range(MEM_HEADS)]

    def project(cols):
        return (jnp.dot(h, wq_ref[:, cols], preferred_element_type=F32) * scale).astype(MXU_DTYPE)

    q = project(head_cols[0])
    for hd, cols in enumerate(head_cols):
        s = lax.dot_general(q, kv_ref[:, cols], _NT, preferred_element_type=F32)
        if hd + 1 < MEM_HEADS:
            q = project(head_cols[hd + 1])
        p = jnp.exp(s - s.max(axis=1, keepdims=True))
        l = p.sum(axis=1, keepdims=True)
        v = kv_ref[:, width + hd * dh:width + (hd + 1) * dh]
        o_ref[:, cols] = (jnp.dot(p.astype(MXU_DTYPE), v, preferred_element_type=F32) / l).astype(o_ref.dtype)
    row_tiles = 2 if o_ref.shape[0] % (2 * BF16_SUBLANES) == 0 else 1
    rows = o_ref.shape[0] // row_tiles
    for t in range(row_tiles):
        r = pl.ds(t * rows, rows)
        y = jnp.dot(o_ref[r, :], wo_ref[...], preferred_element_type=F32)
        _residual_norm(y, x_ref.at[r], gp_ref, gn_ref, xo_ref.at[r], ho_ref.at[r])


def _mem_attention(h, w_q, kv, w_o, x, g_post, g_next, seq, mem_tokens):
    m, d = h.shape
    dh = d // MEM_HEADS
    bq = _pick(seq, 512, BF16_SUBLANES)
    per_batch = seq // bq
    row = lambda i: (i, 0)
    const = lambda i: (0, 0)
    resident = lambda w: _wspec(w, (d, d), const, pipeline_mode=pl.Buffered(1))
    return pl.pallas_call(
        functools.partial(_mem_attn_kernel, dh=dh, scale=dh ** -0.5),
        grid=(m // bq,),
        in_specs=[pl.BlockSpec((bq, d), row),
                  resident(w_q),
                  pl.BlockSpec((mem_tokens, 2 * d), lambda i: (i // per_batch, 0)),
                  resident(w_o),
                  pl.BlockSpec((bq, d), row),
                  pl.BlockSpec((1, d), const),
                  pl.BlockSpec((1, d), const)],
        out_specs=[pl.BlockSpec((bq, d), row), pl.BlockSpec((bq, d), row)],
        out_shape=[jax.ShapeDtypeStruct((m, d), F32), jax.ShapeDtypeStruct((m, d), MXU_DTYPE)],
        scratch_shapes=[pltpu.VMEM((bq, d), MXU_DTYPE)],
        compiler_params=_cparams("parallel"),
        name="mem_attention",
    )(h, _layered(w_q)[0], kv, _layered(w_o)[0], x, g_post[None, :], g_next[None, :])


HALO = BF16_SUBLANES


def _ffn_kernel(h_ref, halo_ref, wa_ref, wg_ref, cwa_ref, cwg_ref, cba_ref, cbg_ref, wd_ref, x_ref, gp_ref,
                gn_ref, xo_ref, ho_ref, hs_ref, u_ref, acc_ref, *, bm, blocks_per_seq, nb, sub, row_tiles):
    i = pl.program_id(0)
    j = pl.program_id(1)

    @pl.when(j == 0)
    def _():
        halo = halo_ref[...]
        first = (i % blocks_per_seq) == 0
        hs_ref[:HALO, :] = jnp.where(first, jnp.zeros_like(halo), halo)
        hs_ref[HALO:, :] = h_ref[...]
        acc_ref[...] = jnp.zeros_like(acc_ref)

    rows = bm // row_tiles
    tiles = [slice(t * sub, (t + 1) * sub) for t in range(wa_ref.shape[1] // sub)]
    bounds = [0] + [HALO + (r + 1) * rows for r in range(row_tiles)]
    for r in range(row_tiles):
        rs = pl.ds(bounds[r], bounds[r + 1] - bounds[r])
        hs = hs_ref[rs, :]
        for t, cols in enumerate(tiles):
            u_ref[2 * t, rs, :] = jnp.dot(hs, wa_ref[:, cols], preferred_element_type=F32)
            u_ref[2 * t + 1, rs, :] = jnp.dot(hs, wg_ref[:, cols], preferred_element_type=F32)

    def conv(slot, cols, row0, rows, cw_ref, cb_ref):
        cw = cw_ref[:, cols]
        c = cb_ref[:, cols]
        for tap in range(CONV_WIDTH):
            c = c + u_ref[slot, pl.ds(row0 + HALO - (CONV_WIDTH - 1) + tap, rows), :] * cw[tap:tap + 1]
        return c

    for r in range(row_tiles):
        acts = []
        for t, cols in enumerate(tiles):
            a = conv(2 * t, cols, r * rows, rows, cwa_ref, cba_ref)
            g = conv(2 * t + 1, cols, r * rows, rows, cwg_ref, cbg_ref)
            acts.append((a * jax.nn.sigmoid(a) * g).astype(MXU_DTYPE))
        acc_ref[pl.ds(r * rows, rows), :] += jnp.dot(jnp.concatenate(acts, axis=1), wd_ref[...],
                                                     preferred_element_type=F32)

    @pl.when(j == nb - 1)
    def _():
        _residual_norm(acc_ref[...], x_ref, gp_ref, gn_ref, xo_ref, ho_ref)


def _ffn(h, w_up, conv_w, conv_b, w_down, x, g_post, g_next, seq):
    m, d = h.shape
    ff = _wshape(w_up)[1] // 2
    bm = _pick(seq, 512, HALO)
    bn = _pick(ff, 512)
    sub = _pick(bn, MXU_COLS)
    nb = ff // bn
    halo_blocks = bm // HALO
    row = lambda i, j: (i, 0)
    const = lambda i, j: (0, 0)
    up_a = lambda i, j: (0, j)
    up_g = lambda i, j: (0, j + nb)
    return pl.pallas_call(
        functools.partial(_ffn_kernel, bm=bm, blocks_per_seq=seq // bm, nb=nb, sub=sub,
                          row_tiles=2 if bm % (2 * BF16_SUBLANES) == 0 else 1),
        grid=(m // bm, nb),
        in_specs=[
            pl.BlockSpec((bm, d), row),
            pl.BlockSpec((HALO, d), lambda i, j: (jnp.maximum(i * halo_blocks - 1, 0), 0)),
            _wspec(w_up, (d, bn), up_a),
            _wspec(w_up, (d, bn), up_g),
            pl.BlockSpec((CONV_WIDTH, bn), up_a),
            pl.BlockSpec((CONV_WIDTH, bn), up_g),
            pl.BlockSpec((1, bn), up_a),
            pl.BlockSpec((1, bn), up_g),
            _wspec(w_down, (bn, d), lambda i, j: (j, 0)),
            pl.BlockSpec((bm, d), row),
            pl.BlockSpec((1, d), const),
            pl.BlockSpec((1, d), const),
        ],
        out_specs=[pl.BlockSpec((bm, d), row), pl.BlockSpec((bm, d), row)],
        out_shape=[jax.ShapeDtypeStruct((m, d), F32), jax.ShapeDtypeStruct((m, d), MXU_DTYPE)],
        scratch_shapes=[pltpu.VMEM((HALO + bm, d), MXU_DTYPE),
                        pltpu.VMEM((2 * (bn // sub), HALO + bm, sub), F32),
                        pltpu.VMEM((bm, d), F32)],
        compiler_params=_cparams("parallel", "arbitrary"),
        name="ffn",
    )(h, h, _layered(w_up)[0], _layered(w_up)[0], conv_w, conv_w, conv_b[None, :], conv_b[None, :],
      _layered(w_down)[0], x, g_post[None, :], g_next[None, :])


def kernel(x, mem, positions, g_pre_mix, w_in, b_gate, lam_q1, lam_k1, lam_q2, lam_k2, g_diff_sub, g_cq,
           w_uq, g_ckv, w_ukv, w_br_diff, w_br_mla, w_mix_out, g_post_mix, g_pre_x, g_mem, w_q_x, w_kv_x,
           w_o_x, g_post_x, g_pre_ffn, w_up, conv_w, conv_b, w_down, g_post_ffn):
    batch, seq, d = x.shape
    mem_tokens = mem.shape[1]
    depth = w_in.shape[0]
    m = batch * seq
    cast = lambda a: a.astype(MXU_DTYPE)

    qw = DIFF_HEADS * 2 * DIFF_QK_DIM
    o_v = 2 * qw
    o_cq = o_v + DIFF_WIDTH
    o_ckv = o_cq + MLA_Q_RANK
    o_kr = o_ckv + MLA_KV_RANK
    o_gt = o_kr + MLA_ROPE_DIM
    assert w_in.shape[2] == o_gt + 2 * d
    assert MLA_Q_RANK % MLA_KV_RANK == 0
    qhead = MLA_NOPE_DIM + MLA_ROPE_DIM
    lat_width = -(-(o_kr - o_cq + LANES) // MXU_COLS) * MXU_COLS
    kvhead = MLA_NOPE_DIM + MLA_V_DIM

    cos, sin = _rope_tables(positions)
    xf = x.reshape(m, d)
    memf = mem.reshape(batch * mem_tokens, d)
    h = _rmsnorm(xf, g_pre_mix[0])
    w_qk_all, w_vt_all, w_lat_all, w_gate_all = _split_w_in(
        w_in, lat_width, qw=qw, o_v=o_v, o_cq=o_cq, o_kr=o_kr, o_gt=o_gt)

    w_br_diff_c, w_br_mla_c, w_mix_out_c = cast(w_br_diff), cast(w_br_mla), cast(w_mix_out)
    w_q_x_c, w_kv_x_c, w_o_x_c = cast(w_q_x), cast(w_kv_x), cast(w_o_x)
    w_up_c, w_down_c = cast(w_up), cast(w_down)

    for l in range(depth):
        lam_init = 0.8 - 0.6 * math.exp(-0.3 * l)
        w_qk, w_vt, w_lat, w_gate = (w_qk_all, l), (w_vt_all, l), (w_lat_all, l), (w_gate_all, l)
        w_uq_h = jnp.pad(w_uq[l].reshape(MLA_Q_RANK, MLA_HEADS, qhead),
                         ((0, 0), (0, 0), (0, 2 * LANES - qhead)))
        w_uq_p = cast(jnp.concatenate([w_uq_h[:, :, :MLA_NOPE_DIM], _pair_halves(w_uq_h[:, :, MLA_NOPE_DIM:])],
                                      axis=2).reshape(MLA_Q_RANK, -1))
        w_ukv_h = w_ukv[l].reshape(MLA_KV_RANK, MLA_HEADS, kvhead)
        w_uk = cast(w_ukv_h[:, :, :MLA_NOPE_DIM].reshape(MLA_KV_RANK, -1))
        w_uvt = cast(w_ukv_h[:, :, MLA_NOPE_DIM:].reshape(MLA_KV_RANK, -1).T)

        bn_qk = _pick(2 * qw, 512)
        qk = _proj(h, w_qk, functools.partial(_ep_rope, q_tiles=qw // bn_qk), name="proj_qk",
                   extras=(cos, sin), extra_specs=_rope_specs)
        vt = _proj_t(h, w_vt, name="proj_vt")
        lat = _proj(h, w_lat, functools.partial(_ep_rope, rope_from=o_kr - o_cq), name="proj_latent",
                    extras=(cos, sin), extra_specs=_rope_specs, bn_pref=lat_width)
        q_mla = _latent_q(lat, 0, g_cq[l], w_uq_p, cos, sin, qhead ** -0.5 * LOG2E)
        k_mla, vt_mla = _latent_kv(lat, MLA_Q_RANK // MLA_KV_RANK, g_ckv[l], w_uk, w_uvt)
        lam_vecs = jnp.stack([lam_q1[l], lam_k1[l], lam_q2[l], lam_k2[l]])
        o_diff = _diff_attention(qk, vt, lam_vecs, g_diff_sub[l], lam_init, batch, seq)
        o_mla = _mla_attention(q_mla, k_mla, lat, (o_kr - o_cq) // LANES, vt_mla, batch, seq)
        merged = _merge(h, o_diff, o_mla, w_gate, b_gate[l], (w_br_diff_c, l), (w_br_mla_c, l))
        xf, h = _out_proj(merged, (w_mix_out_c, l), xf, g_post_mix[l], g_pre_x[l])

        kv_x = _proj(_rmsnorm(memf, g_mem[l]), (w_kv_x_c, l), _ep_plain, name="proj_kvx")
        xf, h = _mem_attention(h, (w_q_x_c, l), kv_x, (w_o_x_c, l), xf, g_post_x[l], g_pre_ffn[l],
                               seq, mem_tokens)

        g_next = g_pre_mix[l + 1] if l + 1 < depth else g_pre_mix[0]
        xf, h = _ffn(h, (w_up_c, l), conv_w[l], conv_b[l], (w_down_c, l), xf, g_post_ffn[l], g_next, seq)

    return xf.reshape(batch, seq, d)
```
